```python
import math
import jax
import jax.numpy as jnp
from jax import lax
import numpy as np

D_MODEL = 1024
BATCH = 4
SEQ = 8192
DEPTH = 4

GRID_W = 64
CTX_LEN = 256
N_MOD = 6
EPS = 1e-6
SSD_HEADS = 16
SSD_HEAD_DIM = 64
SSD_INNER = SSD_HEADS * SSD_HEAD_DIM
SSD_GROUPS = 2
SSD_STATE = 128
SSD_BC = SSD_GROUPS * SSD_STATE
SSD_CONV = 5
SSD_CONV_DIM = SSD_INNER + 2 * SSD_BC
SSD_CHUNK = 128
MLA_HEADS = 8
MLA_Q_RANK = 384
MLA_KV_RANK = 256
MLA_NOPE = 128
MLA_ROPE = 64
MLA_V = 128
MLA_SCALE = (MLA_NOPE + MLA_ROPE) ** -0.5
ATTN_BLOCK = 128
ROPE_BASE = 10000.0
HYB_SPLITS = (SSD_INNER, SSD_CONV_DIM, SSD_HEADS, MLA_Q_RANK, MLA_KV_RANK, MLA_ROPE)
HYB_IN = SSD_INNER + SSD_CONV_DIM + SSD_HEADS + MLA_Q_RANK + MLA_KV_RANK + MLA_ROPE
HYB_CAT = SSD_INNER + MLA_HEADS * MLA_V
GM_CHUNK = 128
GM_INNER = 2 * D_MODEL
GM_GROUPS = 8
PEER_HEADS = 8
PEER_NKEYS = 128
PEER_EXPERTS = PEER_NKEYS * PEER_NKEYS
PEER_DKEY = 256
PEER_TOPK = 16
PEER_BLOCK = 128

kernel_name = 'hybrid_ssd_mla_gmlp_peer_dit'


def _rmsnorm(x, g):
    xf = x.astype(jnp.float32)
    y = xf * lax.rsqrt(jnp.mean(xf * xf, axis=-1, keepdims=True) + EPS)
    return y.astype(x.dtype) * g


def _layernorm(x, g, b):
    xf = x.astype(jnp.float32)
    mu = jnp.mean(xf, axis=-1, keepdims=True)
    var = jnp.mean(jnp.square(xf - mu), axis=-1, keepdims=True)
    return ((xf - mu) * lax.rsqrt(var + EPS)).astype(x.dtype) * g + b


def _modulate(h, shift, scale):
    return h * (1 + scale) + shift


def _split_cols(p, sizes):
    out, start = [], 0
    for s in sizes:
        out.append(p[..., start:start + s])
        start += s
    return out


def _axial_rope(n_lat):
    rows = n_lat // GRID_W
    row = jnp.broadcast_to(jnp.arange(rows, dtype=jnp.float32)[:, None], (rows, GRID_W)).reshape(-1)
    col = jnp.broadcast_to(jnp.arange(GRID_W, dtype=jnp.float32)[None, :], (rows, GRID_W)).reshape(-1)
    n_freq = MLA_ROPE // 4
    inv = ROPE_BASE ** (-jnp.arange(n_freq, dtype=jnp.float32) / n_freq)
    ang = jnp.concatenate([row[:, None] * inv, col[:, None] * inv], axis=-1)
    return jnp.cos(ang), jnp.sin(ang)


def _apply_rope(x, cos, sin):
    half = x.shape[-1] // 2
    x1, x2 = x[..., :half], x[..., half:]
    cos = cos.astype(x.dtype)
    sin = sin.astype(x.dtype)
    return jnp.concatenate([x1 * cos - x2 * sin, x1 * sin + x2 * cos], axis=-1)


def _dwconv_centred(x, w, b):
    k = w.shape[-1]
    kern = jnp.transpose(w)[:, None, :].astype(x.dtype)
    y = lax.conv_general_dilated(x, kern, window_strides=(1,), padding=[(k // 2, k // 2)],
                                 dimension_numbers=('NWC', 'WIO', 'NWC'),
                                 feature_group_count=x.shape[-1])
    return y + b


def _ssd_scan(xs, dt, a, bm, cm, h0):
    b, l = xs.shape[:2]
    nc, q = l // SSD_CHUNK, SSD_CHUNK
    g, e = SSD_GROUPS, SSD_HEADS // SSD_GROUPS
    dt = dt.astype(jnp.float32)
    da = (dt * a).reshape(b, nc, q, g, e)
    xd = (xs * dt[..., None]).reshape(b, nc, q, g, e, SSD_HEAD_DIM)
    bm = bm.reshape(b, nc, q, g, SSD_STATE)
    cm = cm.reshape(b, nc, q, g, SSD_STATE)
    cum = jnp.cumsum(da, axis=2)
    scan_order = jnp.tril(jnp.ones((q, q), bool))[None, None, :, :, None, None]
    seg = cum[:, :, :, None] - cum[:, :, None, :]
    decay_in = jnp.exp(jnp.where(scan_order, seg, -jnp.inf))
    cb = jnp.einsum('bctgn,bcsgn->bctsg', cm, bm)
    y_diag = jnp.einsum('bctsge,bcsgep->bctgep', cb[..., None] * decay_in, xd)
    to_end = jnp.exp(cum[:, :, -1:] - cum)
    states = jnp.einsum('bcsgn,bcsgep->bcgepn', bm, xd * to_end[..., None])
    chunk_decay = jnp.exp(cum[:, :, -1])

    def step(h, inp):
        st, dec = inp
        return h * dec[..., None, None] + st, h

    h_fin, h_in = lax.scan(step, h0, (jnp.moveaxis(states, 1, 0), jnp.moveaxis(chunk_decay, 1, 0)))
    h_in = jnp.moveaxis(h_in, 0, 1)
    y_off = jnp.einsum('bctgn,bcgepn->bctgep', cm, h_in) * jnp.exp(cum)[..., None]
    return (y_diag + y_off).reshape(b, l, SSD_HEADS, SSD_HEAD_DIM), h_fin


def _ssd_inputs(xbc_raw, conv_w, conv_b):
    xbc = jax.nn.silu(_dwconv_centred(xbc_raw, conv_w, conv_b))
    b, l, _ = xbc.shape
    xs = xbc[..., :SSD_INNER].reshape(b, l, SSD_HEADS, SSD_HEAD_DIM)
    bm = xbc[..., SSD_INNER:SSD_INNER + SSD_BC].reshape(b, l, SSD_GROUPS, SSD_STATE)
    cm = xbc[..., SSD_INNER + SSD_BC:].reshape(b, l, SSD_GROUPS, SSD_STATE)
    return xs, bm, cm


def _ssd_bidir(ctx_in, lat_in, a_log, dt_bias):
    b = lat_in[0].shape[0]
    ys_c, ys_l = [], []
    for d in range(2):
        a = -jnp.exp(a_log[d].astype(jnp.float32))
        c_in, l_in = ctx_in, lat_in
        if d == 1:
            c_in = tuple(t[:, ::-1] for t in ctx_in)
            l_in = tuple(t[:, ::-1] for t in lat_in)
        h0 = jnp.zeros((b, SSD_GROUPS, SSD_HEADS // SSD_GROUPS, SSD_HEAD_DIM, SSD_STATE), jnp.float32)
        yc, hc = _ssd_scan(c_in[0], jax.nn.softplus(c_in[3] + dt_bias[d]), a, c_in[1], c_in[2], h0)
        yl, _ = _ssd_scan(l_in[0], jax.nn.softplus(l_in[3] + dt_bias[d]), a, l_in[1], l_in[2], hc)
        if d == 1:
            yc, yl = yc[:, ::-1], yl[:, ::-1]
        ys_c.append(yc)
        ys_l.append(yl)
    return ys_c[0] + ys_c[1], ys_l[0] + ys_l[1]


def _ssd_out(y, xs, z, d_skip, norm_g):
    b, l = z.shape[:2]
    y = (y + d_skip[:, None] * xs).reshape(b, l, SSD_INNER).astype(z.dtype) * jax.nn.silu(z)
    y = _rmsnorm(y.reshape(b, l, SSD_GROUPS, SSD_INNER // SSD_GROUPS), norm_g.reshape(SSD_GROUPS, -1))
    return y.reshape(b, l, SSD_INNER)


def _mla_project(cq, ckv, kr, q_norm_g, w_qb, kv_norm_g, w_kvb, rope):
    b, l, _ = cq.shape
    q = (_rmsnorm(cq, q_norm_g) @ w_qb).reshape(b, l, MLA_HEADS, MLA_NOPE + MLA_ROPE)
    kv = (_rmsnorm(ckv, kv_norm_g) @ w_kvb).reshape(b, l, MLA_HEADS, MLA_NOPE + MLA_V)
    q_nope, q_rope = q[..., :MLA_NOPE], q[..., MLA_NOPE:]
    k_nope, v = kv[..., :MLA_NOPE], kv[..., MLA_NOPE:]
    if rope is not None:
        cos, sin = rope
        q_rope = _apply_rope(q_rope, cos[:, None, :], sin[:, None, :])
        kr = _apply_rope(kr, cos, sin)
    return q_nope, q_rope, k_nope, kr, v


def _attend(qn, qr, kn, kr, v):
    s = jnp.einsum('bqhd,bkhd->bhqk', qn, kn) + jnp.einsum('bqhr,bkr->bhqk', qr, kr)
    p = jax.nn.softmax(s.astype(jnp.float32) * MLA_SCALE, axis=-1).astype(v.dtype)
    return jnp.einsum('bhqk,bkhd->bqhd', p, v)


def _latent_attention(qn, qr, kn, kr, v):
    b, l = qn.shape[:2]
    nb = l // ATTN_BLOCK
    blocks = lambda t: jnp.moveaxis(t.reshape(b, nb, ATTN_BLOCK, *t.shape[2:]), 1, 0)
    out = lax.map(lambda qs: _attend(qs[0], qs[1], kn, kr, v), (blocks(qn), blocks(qr)))
    return jnp.moveaxis(out, 0, 1).reshape(b, l, MLA_HEADS * MLA_V)


def _hybrid_mix(h_c, h_l, rope, w_in, conv_w, conv_b, a_log, dt_bias, d_skip, ssd_norm_g,
                q_norm_g, w_qb, kv_norm_g, w_kvb, w_out, ctx_out):
    z_c, xbc_c, dt_c, cq_c, ckv_c, kr_c = _split_cols(h_c @ w_in, HYB_SPLITS)
    z_l, xbc_l, dt_l, cq_l, ckv_l, kr_l = _split_cols(h_l @ w_in, HYB_SPLITS)
    xs_c, b_c, c_c = _ssd_inputs(xbc_c, conv_w, conv_b)
    xs_l, b_l, c_l = _ssd_inputs(xbc_l, conv_w, conv_b)
    ys_c, ys_l = _ssd_bidir((xs_c, b_c, c_c, dt_c), (xs_l, b_l, c_l, dt_l), a_log, dt_bias)
    qn_c, qr_c, kn_c, krc, v_c = _mla_project(cq_c, ckv_c, kr_c, q_norm_g, w_qb, kv_norm_g, w_kvb, None)
    qn_l, qr_l, kn_l, krl, v_l = _mla_project(cq_l, ckv_l, kr_l, q_norm_g, w_qb, kv_norm_g, w_kvb, rope)
    att_l = _latent_attention(qn_l, qr_l, jnp.concatenate([kn_c, kn_l], axis=1),
                              jnp.concatenate([krc, krl], axis=1), jnp.concatenate([v_c, v_l], axis=1))
    y_l = jnp.concatenate([_ssd_out(ys_l, xs_l, z_l, d_skip, ssd_norm_g), att_l], axis=-1) @ w_out
    if not ctx_out:
        return None, y_l
    b, lc = h_c.shape[:2]
    att_c = _attend(qn_c, qr_c, kn_c, krc, v_c).reshape(b, lc, MLA_HEADS * MLA_V)
    y_c = jnp.concatenate([_ssd_out(ys_c, xs_c, z_c, d_skip, ssd_norm_g), att_c], axis=-1) @ w_out
    return y_c, y_l


def _chunk_gmlp(h, w_in, ln_g, ln_b, ws, bs, w_out):
    b, l, _ = h.shape
    u, v = jnp.split(jax.nn.gelu(h @ w_in, approximate=False), 2, axis=-1)
    v = _layernorm(v, ln_g, ln_b).reshape(b, l // GM_CHUNK, GM_CHUNK, GM_GROUPS, GM_INNER // GM_GROUPS)
    sv = jnp.einsum('gts,bcsgd->bctgd', ws, v) + jnp.transpose(bs)[:, :, None]
    return (u * sv.reshape(b, l, GM_INNER)) @ w_out


def _peer(h, wq, k1, k2, u_tab, v_tab):
    b, l, d = h.shape
    n_tok = b * l
    t = h.reshape(n_tok, d)
    half = PEER_DKEY // 2
    q = (t @ wq).reshape(n_tok, PEER_HEADS, PEER_DKEY)
    s1 = jnp.einsum('thd,hkd->thk', q[..., :half], k1)
    s2 = jnp.einsum('thd,hkd->thk', q[..., half:], k2)
    v1, i1 = lax.top_k(s1, PEER_TOPK)
    v2, i2 = lax.top_k(s2, PEER_TOPK)
    cand_s = (v1[..., :, None] + v2[..., None, :]).reshape(n_tok, PEER_HEADS, PEER_TOPK * PEER_TOPK)
    cand_i = (i1[..., :, None] * PEER_NKEYS + i2[..., None, :]).reshape(n_tok, PEER_HEADS, PEER_TOPK * PEER_TOPK)
    top_s, j = lax.top_k(cand_s, PEER_TOPK)
    idx = jnp.take_along_axis(cand_i, j, axis=-1)
    gate = jax.nn.softmax(top_s.astype(jnp.float32), axis=-1).astype(h.dtype)
    nb = n_tok // PEER_BLOCK

    def expert_block(args):
        tb, ib, gb = args
        act = jax.nn.gelu(jnp.einsum('phkd,pd->phk', jnp.take(u_tab, ib, axis=0), tb), approximate=False)
        return jnp.einsum('phk,phkd->pd', gb * act, jnp.take(v_tab, ib, axis=0))

    out = lax.map(expert_block, (t.reshape(nb, PEER_BLOCK, d),
                                 idx.reshape(nb, PEER_BLOCK, PEER_HEADS, PEER_TOPK),
                                 gate.reshape(nb, PEER_BLOCK, PEER_HEADS, PEER_TOPK)))
    return out.reshape(b, l, d)


def setup_inputs(seed: int = 0) -> dict:
    key = jax.random.key(seed)
    ks = iter(jax.random.split(key, 40))
    nrm = lambda shape, std: std * jax.random.normal(next(ks), shape, jnp.float32)
    ne, no = (DEPTH + 1) // 2, DEPTH // 2
    D = D_MODEL
    x = nrm((BATCH, SEQ, D), 1.0)
    c = nrm((BATCH, D), 1.0)
    ctx = nrm((BATCH, CTX_LEN, D), 1.0)
    c_ctx = nrm((D,), 1.0)
    ada_w = nrm((DEPTH, D, N_MOD * D), 0.02)
    ada_b = nrm((DEPTH, N_MOD * D), 0.02)
    norm1_g = 1.0 + nrm((DEPTH, D), 0.02)
    norm2_g = 1.0 + nrm((DEPTH, D), 0.02)
    hyb_w_in = nrm((ne, D, HYB_IN), D ** -0.5)
    ssd_conv_w = nrm((ne, SSD_CONV_DIM, SSD_CONV), SSD_CONV ** -0.5)
    ssd_conv_b = nrm((ne, SSD_CONV_DIM), 0.02)
    ssd_a_log = jnp.log(jax.random.uniform(next(ks), (ne, 2, SSD_HEADS), jnp.float32, 1.0, 16.0))
    dt0 = jnp.exp(jax.random.uniform(next(ks), (ne, 2, SSD_HEADS), jnp.float32, math.log(1e-3), math.log(1e-1)))
    ssd_dt_bias = dt0 + jnp.log(-jnp.expm1(-dt0))
    ssd_d = 1.0 + nrm((ne, SSD_HEADS), 0.02)
    ssd_norm_g = 1.0 + nrm((ne, SSD_INNER), 0.02)
    mla_q_norm_g = 1.0 + nrm((ne, MLA_Q_RANK), 0.02)
    mla_w_qb = nrm((ne, MLA_Q_RANK, MLA_HEADS * (MLA_NOPE + MLA_ROPE)), MLA_Q_RANK ** -0.5)
    mla_kv_norm_g = 1.0 + nrm((ne, MLA_KV_RANK), 0.02)
    mla_w_kvb = nrm((ne, MLA_KV_RANK, MLA_HEADS * (MLA_NOPE + MLA_V)), MLA_KV_RANK ** -0.5)
    hyb_w_out = nrm((ne, HYB_CAT, D), HYB_CAT ** -0.5)
    gm_w_in = nrm((no, D, 2 * GM_INNER), D ** -0.5)
    gm_ln_g = 1.0 + nrm((no, GM_INNER), 0.02)
    gm_ln_b = nrm((no, GM_INNER), 0.02)
    gm_ws = nrm((no, GM_GROUPS, GM_CHUNK, GM_CHUNK), GM_CHUNK ** -0.5)
    gm_bs = 1.0 + nrm((no, GM_GROUPS, GM_CHUNK), 0.02)
    gm_w_out = nrm((no, GM_INNER, D), GM_INNER ** -0.5)
    peer_wq = nrm((DEPTH, D, PEER_HEADS * PEER_DKEY), D ** -0.5)
    peer_k1 = nrm((DEPTH, PEER_HEADS, PEER_NKEYS, PEER_DKEY // 2), (PEER_DKEY // 2) ** -0.5)
    peer_k2 = nrm((DEPTH, PEER_HEADS, PEER_NKEYS, PEER_DKEY // 2), (PEER_DKEY // 2) ** -0.5)
    peer_u = nrm((DEPTH, PEER_EXPERTS, D), D ** -0.5)
    peer_v = nrm((DEPTH, PEER_EXPERTS, D), PEER_HEADS ** -0.5)
    final_norm_g = 1.0 + nrm((D,), 0.02)
    return {'x': x, 'c': c, 'ctx': ctx, 'c_ctx': c_ctx, 'ada_w': ada_w, 'ada_b': ada_b,
            'norm1_g': norm1_g, 'norm2_g': norm2_g, 'hyb_w_in': hyb_w_in, 'ssd_conv_w': ssd_conv_w,
            'ssd_conv_b': ssd_conv_b, 'ssd_a_log': ssd_a_log, 'ssd_dt_bias': ssd_dt_bias, 'ssd_d': ssd_d,
            'ssd_norm_g': ssd_norm_g, 'mla_q_norm_g': mla_q_norm_g, 'mla_w_qb': mla_w_qb,
            'mla_kv_norm_g': mla_kv_norm_g, 'mla_w_kvb': mla_w_kvb, 'hyb_w_out': hyb_w_out,
            'gm_w_in': gm_w_in, 'gm_ln_g': gm_ln_g, 'gm_ln_b': gm_ln_b, 'gm_ws': gm_ws, 'gm_bs': gm_bs,
            'gm_w_out': gm_w_out, 'peer_wq': peer_wq, 'peer_k1': peer_k1, 'peer_k2': peer_k2,
            'peer_u': peer_u, 'peer_v': peer_v, 'final_norm_g': final_norm_g}


def reference(x, c, ctx, c_ctx, ada_w, ada_b, norm1_g, norm2_g, hyb_w_in, ssd_conv_w, ssd_conv_b,
              ssd_a_log, ssd_dt_bias, ssd_d, ssd_norm_g, mla_q_norm_g, mla_w_qb, mla_kv_norm_g, mla_w_kvb,
              hyb_w_out, gm_w_in, gm_ln_g, gm_ln_b, gm_ws, gm_bs, gm_w_out, peer_wq, peer_k1, peer_k2,
              peer_u, peer_v, final_norm_g):
    rope = _axial_rope(x.shape[1])
    cond_lat = jax.nn.silu(c)
    cond_ctx = jax.nn.silu(c_ctx)
    xc = ctx
    for layer in range(DEPTH):
        i = layer // 2
        even = layer % 2 == 0
        keep_ctx = any(j % 2 == 0 for j in range(layer + 1, DEPTH))
        need_ctx = even or keep_ctx
        sh1, sc1, g1, sh2, sc2, g2 = jnp.split((cond_lat @ ada_w[layer] + ada_b[layer])[:, None, :], N_MOD, axis=-1)
        h_l = _modulate(_rmsnorm(x, norm1_g[layer]), sh1, sc1)
        if need_ctx:
            csh1, csc1, cg1, csh2, csc2, cg2 = jnp.split(cond_ctx @ ada_w[layer] + ada_b[layer], N_MOD, axis=-1)
            h_c = _modulate(_rmsnorm(xc, norm1_g[layer]), csh1, csc1)
        if even:
            y_c, y_l = _hybrid_mix(h_c, h_l, rope, hyb_w_in[i], ssd_conv_w[i], ssd_conv_b[i], ssd_a_log[i],
                                   ssd_dt_bias[i], ssd_d[i], ssd_norm_g[i], mla_q_norm_g[i], mla_w_qb[i],
                                   mla_kv_norm_g[i], mla_w_kvb[i], hyb_w_out[i], keep_ctx)
        else:
            y_l = _chunk_gmlp(h_l, gm_w_in[i], gm_ln_g[i], gm_ln_b[i], gm_ws[i], gm_bs[i], gm_w_out[i])
            y_c = _chunk_gmlp(h_c, gm_w_in[i], gm_ln_g[i], gm_ln_b[i], gm_ws[i], gm_bs[i], gm_w_out[i]) if keep_ctx else None
        x = x + g1 * y_l
        x = x + g2 * _peer(_modulate(_rmsnorm(x, norm2_g[layer]), sh2, sc2),
                           peer_wq[layer], peer_k1[layer], peer_k2[layer], peer_u[layer], peer_v[layer])
        if keep_ctx:
            xc = xc + cg1 * y_c
            xc = xc + cg2 * _peer(_modulate(_rmsnorm(xc, norm2_g[layer]), csh2, csc2),
                                  peer_wq[layer], peer_k1[layer], peer_k2[layer], peer_u[layer], peer_v[layer])
    return _rmsnorm(x, final_norm_g)
```

```python
import functools

import jax
import jax.numpy as jnp
from jax import lax
from jax.experimental import pallas as pl
from jax.experimental.pallas import tpu as pltpu

F32, BF16 = jnp.float32, jnp.bfloat16
EPS = 1e-6
LANE = 128
SUBLANE = 8
VMEM_LIMIT = 56 * 1024 * 1024

D_MODEL = 1024
DEPTH = 4
GRID_W = 64
N_MOD = 6
SSD_HEADS = 16
SSD_HEAD_DIM = 64
SSD_INNER = SSD_HEADS * SSD_HEAD_DIM
SSD_GROUPS = 2
SSD_STATE = 128
SSD_BC = SSD_GROUPS * SSD_STATE
SSD_CONV = 5
SSD_CONV_DIM = SSD_INNER + 2 * SSD_BC
SSD_CHUNK = 128
MLA_HEADS = 8
MLA_Q_RANK = 384
MLA_KV_RANK = 256
MLA_NOPE = 128
MLA_ROPE = 64
MLA_V = 128
MLA_SCALE = (MLA_NOPE + MLA_ROPE) ** -0.5
ROPE_BASE = 10000.0
GM_CHUNK = 128
GM_INNER = 2 * D_MODEL
GM_GROUPS = 8
PEER_HEADS = 8
PEER_NKEYS = 128
PEER_EXPERTS = PEER_NKEYS * PEER_NKEYS
PEER_DKEY = 256
PEER_TOPK = 16

HYB_Z0 = 0
HYB_MISC0 = 1024
HYB_CQ0 = 1152
HYB_XBC0 = 1536
HYB_CKV0 = 3072
HYB_MISC2 = 3328
HYB_COLS = 3456
DT_LANE0 = MLA_ROPE
MLA_QK_PAD = 256

PEER_PAIRS = tuple((a, b) for a in range(PEER_TOPK) for b in range(PEER_TOPK) if (a + 1) * (b + 1) <= PEER_TOPK)


def _params(*sem):
    return pltpu.CompilerParams(dimension_semantics=sem, vmem_limit_bytes=VMEM_LIMIT)


def _rms(x, g):
    return x * lax.rsqrt(jnp.mean(x * x, axis=-1, keepdims=True) + EPS) * g


def _silu(x):
    return x * jax.nn.sigmoid(x)


def _gelu(x):
    return 0.5 * x * (1.0 + lax.erf(x * (2.0 ** -0.5)))


def _nt_dot(a, b):
    return lax.dot_general(a, b, (((1,), (1,)), ((), ())), preferred_element_type=F32)


def _split_dot(x, w, terms, left=False):
    acc = None
    rem = x
    for _ in range(terms):
        hi = rem.astype(BF16)
        part = jnp.dot(w, hi, preferred_element_type=F32) if left else jnp.dot(hi, w, preferred_element_type=F32)
        acc = part if acc is None else acc + part
        rem = rem - hi.astype(F32)
    return acc


def _mods_kernel(c_ref, w_ref, b_ref, o_ref):
    h = _silu(c_ref[...]).astype(BF16)
    o_ref[0] = jnp.dot(h, w_ref[0].astype(BF16), preferred_element_type=F32) + b_ref[0]


def _mods(conds, ada_w, ada_b):
    depth, d, n = ada_w.shape
    tn = 1536
    return pl.pallas_call(
        _mods_kernel,
        out_shape=jax.ShapeDtypeStruct((depth, conds.shape[0], n), F32),
        grid=(depth, n // tn),
        in_specs=[
            pl.BlockSpec(conds.shape, lambda l, j: (0, 0)),
            pl.BlockSpec((1, d, tn), lambda l, j: (l, 0, j)),
            pl.BlockSpec((1, 1, tn), lambda l, j: (l, 0, j)),
        ],
        out_specs=pl.BlockSpec((1, conds.shape[0], tn), lambda l, j: (l, 0, j)),
        compiler_params=_params("parallel", "parallel"),
        name="ada_mods",
    )(conds, ada_w, ada_b.reshape(depth, 1, n))


def _nmm_kernel(x_ref, g_ref, sh_ref, sc_ref, w_ref, o_ref, h_ref):
    @pl.when(pl.program_id(1) == 0)
    def _():
        y = _rms(x_ref[...], g_ref[...])
        h_ref[...] = (y * (1.0 + sc_ref[0]) + sh_ref[0]).astype(BF16)

    o_ref[...] = jnp.dot(h_ref[...], w_ref[...], preferred_element_type=F32).astype(o_ref.dtype)


def _nmm(x, g, mod, k_shift, k_scale, rows_per_batch, w, tm, tn):
    m, k = x.shape
    n = w.shape[1]
    tm = min(tm, m)
    bidx = lambda i: (i * tm) // rows_per_batch
    return pl.pallas_call(
        _nmm_kernel,
        out_shape=jax.ShapeDtypeStruct((m, n), F32),
        grid=(m // tm, n // tn),
        in_specs=[
            pl.BlockSpec((tm, k), lambda i, j: (i, 0)),
            pl.BlockSpec((1, k), lambda i, j: (0, 0)),
            pl.BlockSpec((1, 1, k), lambda i, j: (bidx(i) * N_MOD + k_shift, 0, 0)),
            pl.BlockSpec((1, 1, k), lambda i, j: (bidx(i) * N_MOD + k_scale, 0, 0)),
            pl.BlockSpec((k, tn), lambda i, j: (0, j)),
        ],
        out_specs=pl.BlockSpec((tm, tn), lambda i, j: (i, j)),
        scratch_shapes=[pltpu.VMEM((tm, k), BF16)],
        compiler_params=_params("parallel", "arbitrary"),
        name="norm_mod_matmul",
    )(x, g, mod, mod, w)


def _conv_kernel(cur_ref, prev_ref, next_ref, w_ref, b_ref, o_ref, buf_ref, *, tl):
    i = pl.program_id(1)
    halo = SUBLANE
    buf_ref[0:halo] = jnp.where(i > 0, prev_ref[0], 0.0)
    buf_ref[halo:halo + tl] = cur_ref[0]
    buf_ref[halo + tl:2 * halo + tl] = jnp.where(i < pl.num_programs(1) - 1, next_ref[0], 0.0)
    acc = jnp.broadcast_to(b_ref[...], (tl, b_ref.shape[1]))
    for k in range(SSD_CONV):
        start = halo + k - SSD_CONV // 2
        acc = acc + w_ref[k:k + 1, :] * buf_ref[start:start + tl, :]
    o_ref[0] = _silu(acc)


def _conv_silu(p3, conv_w8, conv_b, tl):
    b, l, _ = p3.shape
    c = SSD_CONV_DIM
    tl = min(tl, l)
    cb = HYB_XBC0 // c
    nh = l // SUBLANE
    r = tl // SUBLANE
    return pl.pallas_call(
        functools.partial(_conv_kernel, tl=tl),
        out_shape=jax.ShapeDtypeStruct((b, l, c), F32),
        grid=(b, l // tl),
        in_specs=[
            pl.BlockSpec((1, tl, c), lambda bb, i: (bb, i, cb)),
            pl.BlockSpec((1, SUBLANE, c), lambda bb, i: (bb, jnp.maximum(i * r - 1, 0), cb)),
            pl.BlockSpec((1, SUBLANE, c), lambda bb, i: (bb, jnp.minimum((i + 1) * r, nh - 1), cb)),
            pl.BlockSpec((SUBLANE, c), lambda bb, i: (0, 0)),
            pl.BlockSpec((1, c), lambda bb, i: (0, 0)),
        ],
        out_specs=pl.BlockSpec((1, tl, c), lambda bb, i: (bb, i, 0)),
        scratch_shapes=[pltpu.VMEM((tl + 2 * SUBLANE, c), F32)],
        compiler_params=_params("parallel", "parallel"),
        name="dwconv_silu",
    )(p3, p3, p3, conv_w8, conv_b)


def _ssd_kernel(xs_ref, bm_ref, cm_ref, misc_ref, h0_ref, tri_ref, e_ref, bias_ref, a_ref, *rest,
                reverse, finalize):
    if finalize:
        z_ref, y0_ref, dskip_ref, ng_ref, y_ref, hT_ref, st_ref, yb_ref = rest
    else:
        y_ref, hT_ref, st_ref, yb_ref = rest
    q = SSD_CHUNK
    gw = SSD_INNER // SSD_GROUPS
    hpg = SSD_HEADS // SSD_GROUPS

    @pl.when(pl.program_id(1) == 0)
    def _():
        st_ref[...] = h0_ref[0]

    xs = xs_ref[0]
    tri = tri_ref[...]
    visible = tri > 0.5
    lane = lax.broadcasted_iota(jnp.int32, (q, LANE), 1)
    dt_lanes = (lane >= DT_LANE0) & (lane < DT_LANE0 + SSD_HEADS)
    dt = jnp.where(dt_lanes, jax.nn.softplus(misc_ref[0] + bias_ref[...]), 0.0)
    da = dt * a_ref[...]
    cum = _split_dot(da, tri.astype(BF16), 3, left=True)
    cum_t = cum.T
    dt_t = dt.T
    edge = cum[0:1, :] if reverse else cum[q - 1:q, :]
    expcum = jnp.exp(cum)
    w_end = jnp.exp(edge - cum) * dt
    e = e_ref[...]
    expcum_x = _split_dot(expcum, e, 2)
    w_end_x = _split_dot(w_end, e, 2)
    xs_b = xs.astype(BF16)
    xw_b = (xs * w_end_x).astype(BF16)
    st = st_ref[...]
    st_b = st.astype(BF16)
    first_head = lane < SSD_HEAD_DIM
    decay_row = expcum_x[0:1, :] if reverse else expcum_x[q - 1:q, :]
    for g in range(SSD_GROUPS):
        bm = bm_ref[0][:, g * SSD_STATE:(g + 1) * SSD_STATE]
        cm_b = cm_ref[0][:, g * SSD_STATE:(g + 1) * SSD_STATE].astype(BF16)
        cb = _nt_dot(cm_b, bm.astype(BF16))
        gs = slice(g * gw, (g + 1) * gw)
        y_off = jnp.dot(cm_b, st_b[:, gs], preferred_element_type=F32) * expcum_x[:, gs]
        for jp in range(hpg // 2):
            c0 = (g * hpg + 2 * jp) * SSD_HEAD_DIM
            tiles = []
            for sub in range(2):
                col = DT_LANE0 + g * hpg + 2 * jp + sub
                seg = cum[:, col:col + 1] - cum_t[col:col + 1, :]
                lmat = jnp.exp(jnp.where(visible, seg, -jnp.inf))
                mm = (cb * lmat * dt_t[col:col + 1, :]).astype(BF16)
                tiles.append(jnp.dot(mm, xs_b[:, c0:c0 + LANE], preferred_element_type=F32))
            yb_ref[:, c0:c0 + LANE] = jnp.where(first_head, tiles[0], tiles[1]) + y_off[:, c0 - g * gw:c0 - g * gw + LANE]
        new = jnp.dot(bm.T.astype(BF16), xw_b[:, gs], preferred_element_type=F32)
        st_ref[:, gs] = st[:, gs] * decay_row[:, gs] + new

    if finalize:
        v = (yb_ref[...] + y0_ref[0] + dskip_ref[...] * xs) * _silu(z_ref[0])
        for g in range(SSD_GROUPS):
            gs = slice(g * gw, (g + 1) * gw)
            y_ref[0, :, gs] = _rms(v[:, gs], ng_ref[:, gs]).astype(y_ref.dtype)
    else:
        y_ref[0] = yb_ref[...]

    @pl.when(pl.program_id(1) == pl.num_programs(1) - 1)
    def _():
        hT_ref[0] = st_ref[...]


def _ssd_scan(xbc3, p3, h0, tri, expand, bias_row, a_row, reverse, fin=None):
    b, l, _ = xbc3.shape
    nc = l // SSD_CHUNK
    cidx = (lambda s: nc - 1 - s) if reverse else (lambda s: s)
    q = SSD_CHUNK
    in_specs = [
        pl.BlockSpec((1, q, SSD_INNER), lambda bb, s: (bb, cidx(s), 0)),
        pl.BlockSpec((1, q, SSD_BC), lambda bb, s: (bb, cidx(s), SSD_INNER // SSD_BC)),
        pl.BlockSpec((1, q, SSD_BC), lambda bb, s: (bb, cidx(s), SSD_INNER // SSD_BC + 1)),
        pl.BlockSpec((1, q, LANE), lambda bb, s: (bb, cidx(s), HYB_MISC0 // LANE)),
        pl.BlockSpec((1, SSD_STATE, SSD_INNER), lambda bb, s: (bb, 0, 0)),
        pl.BlockSpec((q, q), lambda bb, s: (0, 0)),
        pl.BlockSpec((LANE, SSD_INNER), lambda bb, s: (0, 0)),
        pl.BlockSpec((1, LANE), lambda bb, s: (0, 0)),
        pl.BlockSpec((1, LANE), lambda bb, s: (0, 0)),
    ]
    args = [xbc3, xbc3, xbc3, p3, h0, tri, expand, bias_row, a_row]
    if fin is not None:
        y0, dskip_row, normg_row = fin
        in_specs += [
            pl.BlockSpec((1, q, SSD_INNER), lambda bb, s: (bb, cidx(s), HYB_Z0 // SSD_INNER)),
            pl.BlockSpec((1, q, SSD_INNER), lambda bb, s: (bb, cidx(s), 0)),
            pl.BlockSpec((1, SSD_INNER), lambda bb, s: (0, 0)),
            pl.BlockSpec((1, SSD_INNER), lambda bb, s: (0, 0)),
        ]
        args += [p3, y0, dskip_row, normg_row]
    return pl.pallas_call(
        functools.partial(_ssd_kernel, reverse=reverse, finalize=fin is not None),
        out_shape=(jax.ShapeDtypeStruct((b, l, SSD_INNER), BF16 if fin is not None else F32),
                   jax.ShapeDtypeStruct((b, SSD_STATE, SSD_INNER), F32)),
        grid=(b, nc),
        in_specs=in_specs,
        out_specs=(pl.BlockSpec((1, q, SSD_INNER), lambda bb, s: (bb, cidx(s), 0)),
                   pl.BlockSpec((1, SSD_STATE, SSD_INNER), lambda bb, s: (bb, 0, 0))),
        scratch_shapes=[pltpu.VMEM((SSD_STATE, SSD_INNER), F32), pltpu.VMEM((q, SSD_INNER), F32)],
        compiler_params=_params("parallel", "arbitrary"),
        name="ssd_scan_bwd" if reverse else "ssd_scan_fwd",
    )(*args)


def _qproj_kernel(x_ref, g_ref, w_ref, cos_ref, sin_ref, o_ref):
    h = _rms(x_ref[...], g_ref[...]).astype(BF16)
    r = jnp.dot(h, w_ref[...], preferred_element_type=F32)
    main = MLA_HEADS * MLA_QK_PAD
    cos, sin = cos_ref[...], sin_ref[...]
    for hd in range(MLA_HEADS):
        c0 = hd * MLA_QK_PAD
        o_ref[:, c0:c0 + LANE] = r[:, c0:c0 + LANE].astype(o_ref.dtype)
        swapped = r[:, main + hd * LANE:main + (hd + 1) * LANE]
        o_ref[:, c0 + LANE:c0 + 2 * LANE] = (r[:, c0 + LANE:c0 + 2 * LANE] * cos + swapped * sin).astype(o_ref.dtype)


def _qproj(p, g, w, cos, sin, rows_per_batch, tm):
    m = p.shape[0]
    tm = min(tm, m, rows_per_batch)
    nb = rows_per_batch // tm
    n_out = MLA_HEADS * MLA_QK_PAD
    return pl.pallas_call(
        _qproj_kernel,
        out_shape=jax.ShapeDtypeStruct((m, n_out), BF16),
        grid=(m // tm,),
        in_specs=[
            pl.BlockSpec((tm, MLA_Q_RANK), lambda i: (i, HYB_CQ0 // MLA_Q_RANK)),
            pl.BlockSpec((1, MLA_Q_RANK), lambda i: (0, 0)),
            pl.BlockSpec(w.shape, lambda i: (0, 0)),
            pl.BlockSpec((tm, LANE), lambda i: (i % nb, 0)),
            pl.BlockSpec((tm, LANE), lambda i: (i % nb, 0)),
        ],
        out_specs=pl.BlockSpec((tm, n_out), lambda i: (i, 0)),
        compiler_params=_params("parallel"),
        name="mla_q_proj",
    )(p, g, w, cos, sin)


def _kvproj_kernel(x_ref, misc_ref, misc2_ref, g_ref, w_ref, cos_ref, sin_ref, k_ref, v_ref):
    h = _rms(x_ref[...], g_ref[...]).astype(BF16)
    r = jnp.dot(h, w_ref[...], preferred_element_type=F32)
    kr = (misc_ref[...] * cos_ref[...] + misc2_ref[...] * sin_ref[...]).astype(k_ref.dtype)
    for hd in range(MLA_HEADS):
        c0 = hd * MLA_QK_PAD
        k_ref[:, c0:c0 + LANE] = r[:, hd * MLA_NOPE:(hd + 1) * MLA_NOPE].astype(k_ref.dtype)
        k_ref[:, c0 + LANE:c0 + 2 * LANE] = kr
    v_ref[...] = r[:, MLA_HEADS * MLA_NOPE:].astype(v_ref.dtype)


def _kvproj(p, g, w, cos, sin, rows_per_batch, tm):
    m = p.shape[0]
    tm = min(tm, m, rows_per_batch)
    nb = rows_per_batch // tm
    nk, nv = MLA_HEADS * MLA_QK_PAD, MLA_HEADS * MLA_V
    return pl.pallas_call(
        _kvproj_kernel,
        out_shape=(jax.ShapeDtypeStruct((m, nk), BF16), jax.ShapeDtypeStruct((m, nv), BF16)),
        grid=(m // tm,),
        in_specs=[
            pl.BlockSpec((tm, MLA_KV_RANK), lambda i: (i, HYB_CKV0 // MLA_KV_RANK)),
            pl.BlockSpec((tm, LANE), lambda i: (i, HYB_MISC0 // LANE)),
            pl.BlockSpec((tm, LANE), lambda i: (i, HYB_MISC2 // LANE)),
            pl.BlockSpec((1, MLA_KV_RANK), lambda i: (0, 0)),
            pl.BlockSpec(w.shape, lambda i: (0, 0)),
            pl.BlockSpec((tm, LANE), lambda i: (i % nb, 0)),
            pl.BlockSpec((tm, LANE), lambda i: (i % nb, 0)),
        ],
        out_specs=(pl.BlockSpec((tm, nk), lambda i: (i, 0)), pl.BlockSpec((tm, nv), lambda i: (i, 0))),
        compiler_params=_params("parallel"),
        name="mla_kv_proj",
    )(p, p, p, g, w, cos, sin)


def _attn_kernel(q_ref, kc_ref, vc_ref, *rest, with_latent):
    if with_latent:
        k_ref, v_ref, o_ref, m_ref, l_ref, acc_ref = rest
    else:
        o_ref, m_ref, l_ref, acc_ref = rest
    ki = pl.program_id(2)

    def update(hd, kblk, vblk):
        s = _nt_dot(q_ref[:, hd * MLA_QK_PAD:(hd + 1) * MLA_QK_PAD], kblk) * MLA_SCALE
        m_prev = m_ref[hd]
        m_new = jnp.maximum(m_prev, jnp.max(s, axis=-1, keepdims=True))
        alpha = jnp.exp(m_prev - m_new)
        p = jnp.exp(s - m_new)
        l_ref[hd] = alpha * l_ref[hd] + jnp.sum(p, axis=-1, keepdims=True)
        vs = slice(hd * MLA_V, (hd + 1) * MLA_V)
        acc_ref[:, vs] = alpha * acc_ref[:, vs] + jnp.dot(p.astype(BF16), vblk, preferred_element_type=F32)
        m_ref[hd] = m_new

    @pl.when(ki == 0)
    def _():
        m_ref[...] = jnp.full(m_ref.shape, -jnp.inf, F32)
        l_ref[...] = jnp.zeros(l_ref.shape, F32)
        acc_ref[...] = jnp.zeros(acc_ref.shape, F32)
        for hd in range(MLA_HEADS):
            update(hd, kc_ref[:, hd * MLA_QK_PAD:(hd + 1) * MLA_QK_PAD], vc_ref[:, hd * MLA_V:(hd + 1) * MLA_V])

    if with_latent:
        for hd in range(MLA_HEADS):
            update(hd, k_ref[:, hd * MLA_QK_PAD:(hd + 1) * MLA_QK_PAD], v_ref[:, hd * MLA_V:(hd + 1) * MLA_V])

    @pl.when(ki == pl.num_programs(2) - 1)
    def _():
        for hd in range(MLA_HEADS):
            vs = slice(hd * MLA_V, (hd + 1) * MLA_V)
            o_ref[:, vs] = (acc_ref[:, vs] / l_ref[hd]).astype(o_ref.dtype)


def _attention(q, kc, vc, k, v, batch, tq, tk):
    m = q.shape[0]
    lq = m // batch
    lc = kc.shape[0] // batch
    tq = min(tq, lq)
    nq = lq // tq
    nq_pad, nv = MLA_HEADS * MLA_QK_PAD, MLA_HEADS * MLA_V
    in_specs = [
        pl.BlockSpec((tq, nq_pad), lambda b, i, j: (b * nq + i, 0)),
        pl.BlockSpec((lc, nq_pad), lambda b, i, j: (b, 0)),
        pl.BlockSpec((lc, nv), lambda b, i, j: (b, 0)),
    ]
    args = [q, kc, vc]
    nk = 1
    if k is not None:
        lk = k.shape[0] // batch
        tk = min(tk, lk)
        nk = lk // tk
        in_specs += [
            pl.BlockSpec((tk, nq_pad), lambda b, i, j: (b * nk + j, 0)),
            pl.BlockSpec((tk, nv), lambda b, i, j: (b * nk + j, 0)),
        ]
        args += [k, v]
    return pl.pallas_call(
        functools.partial(_attn_kernel, with_latent=k is not None),
        out_shape=jax.ShapeDtypeStruct((m, nv), BF16),
        grid=(batch, nq, nk),
        in_specs=in_specs,
        out_specs=pl.BlockSpec((tq, nv), lambda b, i, j: (b * nq + i, 0)),
        scratch_shapes=[pltpu.VMEM((MLA_HEADS, tq, 1), F32), pltpu.VMEM((MLA_HEADS, tq, 1), F32),
                        pltpu.VMEM((tq, nv), F32)],
        compiler_params=_params("parallel", "parallel", "arbitrary"),
        name="mla_attention",
    )(*args)


def _outproj_kernel(a1_ref, a2_ref, w1_ref, w2_ref, x_ref, gate_ref, o_ref):
    acc = jnp.dot(a1_ref[...], w1_ref[...], preferred_element_type=F32)
    acc = acc + jnp.dot(a2_ref[...], w2_ref[...], preferred_element_type=F32)
    o_ref[...] = x_ref[...] + gate_ref[0] * acc


def _outproj(a1, a2, w1, w2, x, mod, k_gate, rows_per_batch, tm):
    m, d = x.shape
    tm = min(tm, m)
    bidx = lambda i: (i * tm) // rows_per_batch
    return pl.pallas_call(
        _outproj_kernel,
        out_shape=jax.ShapeDtypeStruct((m, d), F32),
        grid=(m // tm,),
        in_specs=[
            pl.BlockSpec((tm, a1.shape[1]), lambda i: (i, 0)),
            pl.BlockSpec((tm, a2.shape[1]), lambda i: (i, 0)),
            pl.BlockSpec(w1.shape, lambda i: (0, 0)),
            pl.BlockSpec(w2.shape, lambda i: (0, 0)),
            pl.BlockSpec((tm, d), lambda i: (i, 0)),
            pl.BlockSpec((1, 1, d), lambda i: (bidx(i) * N_MOD + k_gate, 0, 0)),
        ],
        out_specs=pl.BlockSpec((tm, d), lambda i: (i, 0)),
        compiler_params=_params("parallel"),
        name="hyb_out_proj",
    )(a1, a2, w1, w2, x, mod)


def _gmlp_kernel(x_ref, g_ref, sh_ref, sc_ref, gate_ref, win_ref, lng_ref, lnb_ref, ws_ref, bs_ref, wout_ref,
                 o_ref, uv_ref):
    x = x_ref[...]
    tm = x.shape[0]
    h = (_rms(x, g_ref[...]) * (1.0 + sc_ref[0]) + sh_ref[0]).astype(BF16)
    uv = _gelu(jnp.dot(h, win_ref[...], preferred_element_type=F32))
    u, v = uv[:, :GM_INNER], uv[:, GM_INNER:]
    mu = jnp.mean(v, axis=-1, keepdims=True)
    var = jnp.mean(jnp.square(v - mu), axis=-1, keepdims=True)
    vn = ((v - mu) * lax.rsqrt(var + EPS) * lng_ref[...] + lnb_ref[...]).astype(BF16)
    gw = GM_INNER // GM_GROUPS
    for c in range(tm // GM_CHUNK):
        rs = slice(c * GM_CHUNK, (c + 1) * GM_CHUNK)
        for g in range(GM_GROUPS):
            cs = slice(g * gw, (g + 1) * gw)
            sv = jnp.dot(ws_ref[g], vn[rs, cs], preferred_element_type=F32) + bs_ref[:, cs]
            uv_ref[rs, cs] = (u[rs, cs] * sv).astype(BF16)
    o_ref[...] = x + gate_ref[0] * jnp.dot(uv_ref[...], wout_ref[...], preferred_element_type=F32)


def _gmlp(x, g, mod, rows_per_batch, w_in, ln_g, ln_b, ws, bs_full, w_out, tm):
    m, d = x.shape
    tm = min(tm, m)
    bidx = lambda i: (i * tm) // rows_per_batch
    const = lambda shape: pl.BlockSpec(shape, lambda i: (0,) * len(shape))
    modspec = lambda k: pl.BlockSpec((1, 1, d), lambda i: (bidx(i) * N_MOD + k, 0, 0))
    return pl.pallas_call(
        _gmlp_kernel,
        out_shape=jax.ShapeDtypeStruct((m, d), F32),
        grid=(m // tm,),
        in_specs=[
            pl.BlockSpec((tm, d), lambda i: (i, 0)), const((1, d)), modspec(0), modspec(1), modspec(2),
            const(w_in.shape), const((1, GM_INNER)), const((1, GM_INNER)), const(ws.shape), const(bs_full.shape),
            const(w_out.shape),
        ],
        out_specs=pl.BlockSpec((tm, d), lambda i: (i, 0)),
        scratch_shapes=[pltpu.VMEM((tm, GM_INNER), BF16)],
        compiler_params=_params("parallel"),
        name="chunk_gmlp",
    )(x, g, mod, mod, mod, w_in, ln_g, ln_b, ws, bs_full, w_out)


def _top_values(x, k):
    out = []
    for r in range(k):
        mx = jnp.max(x, axis=0, keepdims=True)
        out.append(mx)
        if r + 1 < k:
            x = jnp.where(x == mx, -jnp.inf, x)
    return out


def _peer_stats_kernel(x_ref, g_ref, sh_ref, sc_ref, wq_ref, k1_ref, k2_ref, t_ref, a_ref, b_ref, th_ref, q_ref,
                       cand_ref, prod_ref):
    h = (_rms(x_ref[...], g_ref[...]) * (1.0 + sc_ref[0]) + sh_ref[0]).astype(BF16)
    t_ref[...] = h
    q_ref[...] = jnp.dot(h, wq_ref[...], preferred_element_type=F32).astype(BF16)
    half = PEER_DKEY // 2
    tm = h.shape[0]
    n_pairs = len(PEER_PAIRS)
    cand_ref[n_pairs:, :] = jnp.full((cand_ref.shape[0] - n_pairs, tm), -jnp.inf, F32)
    prod_ref[n_pairs:, :] = jnp.zeros((cand_ref.shape[0] - n_pairs, tm), F32)

    def head(hd, carry):
        c0 = pl.multiple_of(hd * PEER_DKEY, PEER_DKEY)
        s1 = _nt_dot(k1_ref[hd], q_ref[:, pl.ds(c0, half)])
        s2 = _nt_dot(k2_ref[hd], q_ref[:, pl.ds(c0 + half, half)])
        v1 = _top_values(s1, PEER_TOPK)
        v2 = _top_values(s2, PEER_TOPK)
        for idx, (a, b) in enumerate(PEER_PAIRS):
            cand_ref[idx:idx + 1, :] = v1[a] + v2[b]
        cand = cand_ref[...]
        tops = _top_values(cand, PEER_TOPK)
        z = tops[0] * 0.0
        for tv in tops:
            z = z + jnp.exp(tv - tops[0])
        inv_z = 1.0 / z
        a_ref[hd] = jnp.exp(s1 - v1[0]) * inv_z
        b_ref[hd] = jnp.exp(s2 - v2[0])
        pa = [jnp.exp(v - v1[0]) * inv_z for v in v1]
        pb = [jnp.exp(v - v2[0]) for v in v2]
        for idx, (a, b) in enumerate(PEER_PAIRS):
            prod_ref[idx:idx + 1, :] = pa[a] * pb[b]
        theta = jnp.min(jnp.where(cand >= tops[PEER_TOPK - 1], prod_ref[...], jnp.inf), axis=0, keepdims=True)
        th_ref[hd] = jnp.broadcast_to(theta, (SUBLANE, tm))
        return carry

    lax.fori_loop(0, PEER_HEADS, head, 0)


def _peer_stats(x, g, mod, rows_per_batch, wq, k1, k2, tm):
    m, d = x.shape
    tm = min(tm, m)
    bidx = lambda i: (i * tm) // rows_per_batch
    const = lambda shape: pl.BlockSpec(shape, lambda i: (0,) * len(shape))
    modspec = lambda k: pl.BlockSpec((1, 1, d), lambda i: (bidx(i) * N_MOD + k, 0, 0))
    return pl.pallas_call(
        _peer_stats_kernel,
        out_shape=(jax.ShapeDtypeStruct((m, d), BF16),
                   jax.ShapeDtypeStruct((PEER_HEADS, PEER_NKEYS, m), F32),
                   jax.ShapeDtypeStruct((PEER_HEADS, PEER_NKEYS, m), F32),
                   jax.ShapeDtypeStruct((PEER_HEADS, SUBLANE, m), F32)),
        grid=(m // tm,),
        in_specs=[pl.BlockSpec((tm, d), lambda i: (i, 0)), const((1, d)), modspec(3), modspec(4),
                  const(wq.shape), const(k1.shape), const(k2.shape)],
        out_specs=(pl.BlockSpec((tm, d), lambda i: (i, 0)),
                   pl.BlockSpec((PEER_HEADS, PEER_NKEYS, tm), lambda i: (0, 0, i)),
                   pl.BlockSpec((PEER_HEADS, PEER_NKEYS, tm), lambda i: (0, 0, i)),
                   pl.BlockSpec((PEER_HEADS, SUBLANE, tm), lambda i: (0, 0, i))),
        scratch_shapes=[pltpu.VMEM((tm, PEER_HEADS * PEER_DKEY), BF16),
                        pltpu.VMEM((-(-len(PEER_PAIRS) // SUBLANE) * SUBLANE, tm), F32),
                        pltpu.VMEM((-(-len(PEER_PAIRS) // SUBLANE) * SUBLANE, tm), F32)],
        compiler_params=_params("parallel"),
        name="peer_topk",
    )(x, g, mod, mod, wq, k1, k2)


def _peer_expert_kernel(t_ref, u_ref, vt_ref, a_ref, b_ref, th_ref, x_ref, gate_ref, o_ref, acc_ref, gs_ref, *, te):
    j = pl.program_id(1)

    @pl.when(j == 0)
    def _():
        acc_ref[...] = jnp.zeros(acc_ref.shape, F32)

    st = _nt_dot(u_ref[...], t_ref[...])
    for r in range(te // PEER_NKEYS):
        rs = slice(r * PEER_NKEYS, (r + 1) * PEER_NKEYS)
        w = None
        for hd in range(PEER_HEADS):
            p = a_ref[hd, r:r + 1, :] * b_ref[hd]
            sel = jnp.where(p >= th_ref[hd, 0:1, :], p, 0.0)
            w = sel if w is None else w + sel
        gs_ref[rs, :] = (_gelu(st[rs, :]) * w).astype(BF16)
    acc_ref[...] += jnp.dot(vt_ref[...], gs_ref[...], preferred_element_type=F32)

    @pl.when(j == pl.num_programs(1) - 1)
    def _():
        o_ref[...] = x_ref[...] + gate_ref[0] * acc_ref[...].T


def _peer_experts(t, u, vt, a, b, th, x, mod, k_gate, rows_per_batch, tm, te):
    m, d = x.shape
    tm = min(tm, m)
    ne = u.shape[0]
    bidx = lambda i: (i * tm) // rows_per_batch
    rows = te // PEER_NKEYS
    return pl.pallas_call(
        functools.partial(_peer_expert_kernel, te=te),
        out_shape=jax.ShapeDtypeStruct((m, d), F32),
        grid=(m // tm, ne // te),
        in_specs=[
            pl.BlockSpec((tm, d), lambda i, j: (i, 0)),
            pl.BlockSpec((te, d), lambda i, j: (j, 0)),
            pl.BlockSpec((d, te), lambda i, j: (0, j)),
            pl.BlockSpec((PEER_HEADS, rows, tm), lambda i, j: (0, j, i)),
            pl.BlockSpec((PEER_HEADS, PEER_NKEYS, tm), lambda i, j: (0, 0, i)),
            pl.BlockSpec((PEER_HEADS, SUBLANE, tm), lambda i, j: (0, 0, i)),
            pl.BlockSpec((tm, d), lambda i, j: (i, 0)),
            pl.BlockSpec((1, 1, d), lambda i, j: (bidx(i) * N_MOD + k_gate, 0, 0)),
        ],
        out_specs=pl.BlockSpec((tm, d), lambda i, j: (i, 0)),
        scratch_shapes=[pltpu.VMEM((d, tm), F32), pltpu.VMEM((te, tm), BF16)],
        compiler_params=_params("parallel", "arbitrary"),
        name="peer_experts",
    )(t, u, vt, a, b, th, x, mod)


def _final_norm_kernel(x_ref, g_ref, o_ref):
    o_ref[...] = _rms(x_ref[...], g_ref[...])


def _final_norm(x, g, tm):
    m, d = x.shape
    tm = min(tm, m)
    return pl.pallas_call(
        _final_norm_kernel,
        out_shape=jax.ShapeDtypeStruct((m, d), F32),
        grid=(m // tm,),
        in_specs=[pl.BlockSpec((tm, d), lambda i: (i, 0)), pl.BlockSpec((1, d), lambda i: (0, 0))],
        out_specs=pl.BlockSpec((tm, d), lambda i: (i, 0)),
        compiler_params=_params("parallel"),
        name="final_rmsnorm",
    )(x, g)


def _rope_tables(n_lat, n_ctx):
    rows = n_lat // GRID_W
    row = jnp.broadcast_to(jnp.arange(rows, dtype=F32)[:, None], (rows, GRID_W)).reshape(-1)
    col = jnp.broadcast_to(jnp.arange(GRID_W, dtype=F32)[None, :], (rows, GRID_W)).reshape(-1)
    n_freq = MLA_ROPE // 4
    inv = ROPE_BASE ** (-jnp.arange(n_freq, dtype=F32) / n_freq)
    ang = jnp.concatenate([row[:, None] * inv, col[:, None] * inv], axis=-1)
    cos, sin = jnp.cos(ang), jnp.sin(ang)
    pad = jnp.zeros((n_lat, LANE - MLA_ROPE), F32)
    cos_l = jnp.concatenate([cos, cos, pad], axis=-1)
    sin_l = jnp.concatenate([-sin, sin, pad], axis=-1)
    cos_c = jnp.concatenate([jnp.ones((n_ctx, MLA_ROPE), F32), jnp.zeros((n_ctx, LANE - MLA_ROPE), F32)], axis=-1)
    return (cos_l, sin_l), (cos_c, jnp.zeros((n_ctx, LANE), F32))


def _swap_halves(w):
    half = w.shape[-1] // 2
    return jnp.concatenate([w[..., half:], w[..., :half]], axis=-1)


def _prep_hybrid(w_in, conv_w, conv_b, a_log, dt_bias, d_skip, norm_g, q_norm_g, w_qb, kv_norm_g, w_kvb, w_out):
    d = w_in.shape[0]
    o = 0
    pieces = {}
    for name, width in (("z", SSD_INNER), ("xbc", SSD_CONV_DIM), ("dt", SSD_HEADS), ("cq", MLA_Q_RANK),
                        ("ckv", MLA_KV_RANK), ("kr", MLA_ROPE)):
        pieces[name] = w_in[:, o:o + width]
        o += width
    zeros = lambda n: jnp.zeros((d, n), w_in.dtype)
    w_hyb = jnp.concatenate([
        pieces["z"], pieces["kr"], pieces["dt"], zeros(LANE - MLA_ROPE - SSD_HEADS), pieces["cq"], pieces["xbc"],
        pieces["ckv"], _swap_halves(pieces["kr"]), zeros(LANE - MLA_ROPE)], axis=1).astype(BF16)
    conv_w8 = jnp.concatenate([conv_w.T, jnp.zeros((SUBLANE - SSD_CONV, SSD_CONV_DIM), F32)], axis=0)
    lane_row = lambda v: jnp.zeros((1, LANE), F32).at[0, DT_LANE0:DT_LANE0 + SSD_HEADS].set(v)
    dirs = [(lane_row(dt_bias[k]), lane_row(-jnp.exp(a_log[k]))) for k in range(2)]
    expand = jnp.zeros((LANE, SSD_INNER), F32).at[DT_LANE0:DT_LANE0 + SSD_HEADS].set(
        jnp.repeat(jnp.eye(SSD_HEADS, dtype=F32), SSD_HEAD_DIM, axis=1)).astype(BF16)
    wq3 = w_qb.reshape(MLA_Q_RANK, MLA_HEADS, MLA_NOPE + MLA_ROPE)
    zq = jnp.zeros((MLA_Q_RANK, MLA_HEADS, LANE - MLA_ROPE), w_qb.dtype)
    q_main = jnp.concatenate([wq3, zq], axis=-1).reshape(MLA_Q_RANK, MLA_HEADS * MLA_QK_PAD)
    q_swap = jnp.concatenate([_swap_halves(wq3[..., MLA_NOPE:]), zq], axis=-1).reshape(MLA_Q_RANK, MLA_HEADS * LANE)
    wq = jnp.concatenate([q_main, q_swap], axis=1).astype(BF16)
    wkv3 = w_kvb.reshape(MLA_KV_RANK, MLA_HEADS, MLA_NOPE + MLA_V)
    wkv = jnp.concatenate([wkv3[..., :MLA_NOPE].reshape(MLA_KV_RANK, -1), wkv3[..., MLA_NOPE:].reshape(MLA_KV_RANK, -1)],
                          axis=1).astype(BF16)
    return dict(
        w_hyb=w_hyb, conv_w8=conv_w8, conv_b=conv_b[None, :], dirs=dirs, expand=expand,
        dskip=jnp.repeat(d_skip, SSD_HEAD_DIM)[None, :], normg=norm_g[None, :],
        q_norm_g=q_norm_g[None, :], wq=wq, kv_norm_g=kv_norm_g[None, :], wkv=wkv,
        w_out1=w_out[:SSD_INNER].astype(BF16), w_out2=w_out[SSD_INNER:].astype(BF16))


def _hybrid_layer(xl, xc, batch, g1, mod_l, mod_c, hp, ropes, update_ctx):
    n_lat, n_ctx = xl.shape[0] // batch, xc.shape[0] // batch
    (cos_l, sin_l), (cos_c, sin_c) = ropes
    q = SSD_CHUNK
    tri_f = jnp.tril(jnp.ones((q, q), F32))
    tri_b = jnp.triu(jnp.ones((q, q), F32))
    pl_ = _nmm(xl, g1, mod_l, 0, 1, n_lat, hp["w_hyb"], 512, HYB_COLS // 3)
    pc_ = _nmm(xc, g1, mod_c, 0, 1, xc.shape[0], hp["w_hyb"], 256, HYB_COLS // 3)
    pl3, pc3 = pl_.reshape(batch, n_lat, HYB_COLS), pc_.reshape(batch, n_ctx, HYB_COLS)
    xbc_l = _conv_silu(pl3, hp["conv_w8"], hp["conv_b"], 512)
    xbc_c = _conv_silu(pc3, hp["conv_w8"], hp["conv_b"], 512)
    h0 = jnp.zeros((batch, SSD_STATE, SSD_INNER), F32)
    (bias0, a0), (bias1, a1) = hp["dirs"]
    yc0, hc0 = _ssd_scan(xbc_c, pc3, h0, tri_f, hp["expand"], bias0, a0, False)
    yl0, _ = _ssd_scan(xbc_l, pl3, hc0, tri_f, hp["expand"], bias0, a0, False)
    fin = lambda y0: (y0, hp["dskip"], hp["normg"])
    sc, hc1 = _ssd_scan(xbc_c, pc3, h0, tri_b, hp["expand"], bias1, a1, True, fin(yc0))
    sl, _ = _ssd_scan(xbc_l, pl3, hc1, tri_b, hp["expand"], bias1, a1, True, fin(yl0))
    ql = _qproj(pl_, hp["q_norm_g"], hp["wq"], cos_l, sin_l, n_lat, 512)
    kl, vl = _kvproj(pl_, hp["kv_norm_g"], hp["wkv"], cos_l, sin_l, n_lat, 512)
    kc, vc = _kvproj(pc_, hp["kv_norm_g"], hp["wkv"], cos_c, sin_c, n_ctx, 256)
    att_l = _attention(ql, kc, vc, kl, vl, batch, 512, 512)
    xl_new = _outproj(sl.reshape(-1, SSD_INNER), att_l, hp["w_out1"], hp["w_out2"], xl, mod_l, 2, n_lat, 512)
    xc_new = xc
    if update_ctx:
        qc = _qproj(pc_, hp["q_norm_g"], hp["wq"], cos_c, sin_c, n_ctx, 256)
        att_c = _attention(qc, kc, vc, None, None, batch, 256, 256)
        xc_new = _outproj(sc.reshape(-1, SSD_INNER), att_c, hp["w_out1"], hp["w_out2"], xc, mod_c, 2, xc.shape[0], 256)
    return xl_new, xc_new


def _peer_layer(x, g2, mod, rows_per_batch, pp, tm):
    t, a, b, th = _peer_stats(x, g2, mod, rows_per_batch, pp["wq"], pp["k1"], pp["k2"], tm)
    return _peer_experts(t, pp["u"], pp["vt"], a, b, th, x, mod, 5, rows_per_batch, tm, 1024)


def kernel(x, c, ctx, c_ctx, ada_w, ada_b, norm1_g, norm2_g, hyb_w_in, ssd_conv_w, ssd_conv_b, ssd_a_log,
           ssd_dt_bias, ssd_d, ssd_norm_g, mla_q_norm_g, mla_w_qb, mla_kv_norm_g, mla_w_kvb, hyb_w_out, gm_w_in,
           gm_ln_g, gm_ln_b, gm_ws, gm_bs, gm_w_out, peer_wq, peer_k1, peer_k2, peer_u, peer_v, final_norm_g):
    batch, n_lat, d = x.shape
    n_ctx = ctx.shape[1]
    depth = ada_w.shape[0]
    xl = x.reshape(batch * n_lat, d)
    xc = ctx.reshape(batch * n_ctx, d)
    n_cond = -(-(batch + 1) // SUBLANE) * SUBLANE
    conds = jnp.zeros((n_cond, d), F32).at[:batch].set(c).at[batch].set(c_ctx)
    mods = _mods(conds, ada_w, ada_b)
    ropes = _rope_tables(n_lat, n_ctx)
    for layer in range(depth):
        i = layer // 2
        even = layer % 2 == 0
        keep_ctx = any(j % 2 == 0 for j in range(layer + 1, depth))
        mod_l = mods[layer, :batch].reshape(batch * N_MOD, 1, d)
        mod_c = mods[layer, batch].reshape(N_MOD, 1, d)
        g1, g2 = norm1_g[layer][None, :], norm2_g[layer][None, :]
        if even:
            hp = _prep_hybrid(hyb_w_in[i], ssd_conv_w[i], ssd_conv_b[i], ssd_a_log[i], ssd_dt_bias[i], ssd_d[i],
                              ssd_norm_g[i], mla_q_norm_g[i], mla_w_qb[i], mla_kv_norm_g[i], mla_w_kvb[i], hyb_w_out[i])
            xl, xc = _hybrid_layer(xl, xc, batch, g1, mod_l, mod_c, hp, ropes, keep_ctx)
        else:
            gp = (gm_w_in[i].astype(BF16), gm_ln_g[i][None, :], gm_ln_b[i][None, :], gm_ws[i].astype(BF16),
                  jnp.repeat(gm_bs[i].T, GM_INNER // GM_GROUPS, axis=1), gm_w_out[i].astype(BF16))
            xl = _gmlp(xl, g1, mod_l, n_lat, *gp, 256)
            if keep_ctx:
                xc = _gmlp(xc, g1, mod_c, xc.shape[0], *gp, 256)
        pp = dict(wq=peer_wq[layer].astype(BF16), k1=peer_k1[layer].astype(BF16), k2=peer_k2[layer].astype(BF16),
                  u=peer_u[layer].astype(BF16), vt=peer_v[layer].T.astype(BF16))
        xl = _peer_layer(xl, g2, mod_l, n_lat, pp, 256)
        if keep_ctx:
            xc = _peer_layer(xc, g2, mod_c, xc.shape[0], pp, 256)
    return _final_norm(xl, final_norm_g[None, :], 512).reshape(batch, n_lat, d)
```

```python
import functools

import jax
import jax.numpy as jnp
from jax import lax
from jax.experimental import pallas as pl
from jax.experimental.pallas import tpu as pltpu

F32, BF16 = jnp.float32, jnp.bfloat16
EPS = 1e-6
LANE = 128
SUBLANE = 8
VMEM_LIMIT = 56 * 1024 * 1024

D_MODEL = 1024
DEPTH = 4
GRID_W = 64
N_MOD = 6
SSD_HEADS = 16
SSD_HEAD_DIM = 64
SSD_INNER = SSD_HEADS * SSD_HEAD_DIM
SSD_GROUPS = 2
SSD_STATE = 128
SSD_BC = SSD_GROUPS * SSD_STATE
SSD_CONV = 5
SSD_CONV_DIM = SSD_INNER + 2 * SSD_BC
SSD_CHUNK = 128
MLA_HEADS = 8
MLA_Q_RANK = 384
MLA_KV_RANK = 256
MLA_NOPE = 128
MLA_ROPE = 64
MLA_V = 128
MLA_SCALE = (MLA_NOPE + MLA_ROPE) ** -0.5
EXP2_SCALE = MLA_SCALE * 1.4426950408889634
ROPE_BASE = 10000.0
GM_CHUNK = 128
GM_INNER = 2 * D_MODEL
GM_GROUPS = 8
PEER_HEADS = 8
PEER_NKEYS = 128
PEER_EXPERTS = PEER_NKEYS * PEER_NKEYS
PEER_DKEY = 256
PEER_TOPK = 16

HYB_Z0 = 0
HYB_MISC0 = 1024
HYB_CQ0 = 1152
HYB_XBC0 = 1536
HYB_CKV0 = 3072
HYB_MISC2 = 3328
HYB_COLS = 3456
DT_LANE0 = MLA_ROPE
MLA_QK_PAD = 256

PEER_PAIRS = tuple((a, b) for a in range(PEER_TOPK) for b in range(PEER_TOPK) if (a + 1) * (b + 1) <= PEER_TOPK)


def _params(*sem, flags=None):
    return pltpu.CompilerParams(dimension_semantics=sem, vmem_limit_bytes=VMEM_LIMIT, flags=flags)


def _rms(x, g):
    return x * lax.rsqrt(jnp.mean(x * x, axis=-1, keepdims=True) + EPS) * g


def _silu(x):
    return x * jax.nn.sigmoid(x)


def _gelu(x):
    return 0.5 * x * (1.0 + lax.erf(x * (2.0 ** -0.5)))


def _nt_dot(a, b):
    return lax.dot_general(a, b, (((1,), (1,)), ((), ())), preferred_element_type=F32)


def _split_dot(x, w, terms, left=False):
    acc = None
    rem = x
    for _ in range(terms):
        hi = rem.astype(BF16)
        part = jnp.dot(w, hi, preferred_element_type=F32) if left else jnp.dot(hi, w, preferred_element_type=F32)
        acc = part if acc is None else acc + part
        rem = rem - hi.astype(F32)
    return acc


def _mods_kernel(c_ref, w_ref, b_ref, o_ref):
    h = _silu(c_ref[...]).astype(BF16)
    o_ref[0] = jnp.dot(h, w_ref[0].astype(BF16), preferred_element_type=F32) + b_ref[0]


def _mods(conds, ada_w, ada_b):
    depth, d, n = ada_w.shape
    tn = 1536
    return pl.pallas_call(
        _mods_kernel,
        out_shape=jax.ShapeDtypeStruct((depth, conds.shape[0], n), F32),
        grid=(depth, n // tn),
        in_specs=[
            pl.BlockSpec(conds.shape, lambda l, j: (0, 0)),
            pl.BlockSpec((1, d, tn), lambda l, j: (l, 0, j)),
            pl.BlockSpec((1, 1, tn), lambda l, j: (l, 0, j)),
        ],
        out_specs=pl.BlockSpec((1, conds.shape[0], tn), lambda l, j: (l, 0, j)),
        compiler_params=_params("parallel", "parallel"),
        name="ada_mods",
    )(conds, ada_w, ada_b.reshape(depth, 1, n))


def _nmm_kernel(x_ref, g_ref, sh_ref, sc_ref, w_ref, o_ref, h_ref):
    @pl.when(pl.program_id(1) == 0)
    def _():
        y = _rms(x_ref[...], g_ref[...])
        h_ref[...] = (y * (1.0 + sc_ref[0]) + sh_ref[0]).astype(BF16)

    o_ref[...] = jnp.dot(h_ref[...], w_ref[...], preferred_element_type=F32).astype(o_ref.dtype)


def _nmm(x, g, mod, k_shift, k_scale, rows_per_batch, w, tm, tn):
    m, k = x.shape
    n = w.shape[1]
    tm = min(tm, m)
    bidx = lambda i: (i * tm) // rows_per_batch
    return pl.pallas_call(
        _nmm_kernel,
        out_shape=jax.ShapeDtypeStruct((m, n), F32),
        grid=(m // tm, n // tn),
        in_specs=[
            pl.BlockSpec((tm, k), lambda i, j: (i, 0)),
            pl.BlockSpec((1, k), lambda i, j: (0, 0)),
            pl.BlockSpec((1, 1, k), lambda i, j: (bidx(i) * N_MOD + k_shift, 0, 0)),
            pl.BlockSpec((1, 1, k), lambda i, j: (bidx(i) * N_MOD + k_scale, 0, 0)),
            pl.BlockSpec((k, tn), lambda i, j: (0, j)),
        ],
        out_specs=pl.BlockSpec((tm, tn), lambda i, j: (i, j)),
        scratch_shapes=[pltpu.VMEM((tm, k), BF16)],
        compiler_params=_params("parallel", "arbitrary"),
        name="norm_mod_matmul",
    )(x, g, mod, mod, w)


def _conv_kernel(cur_ref, prev_ref, next_ref, w_ref, b_ref, o_ref, buf_ref, *, tl):
    i = pl.program_id(1)
    halo = SUBLANE
    buf_ref[0:halo] = jnp.where(i > 0, prev_ref[0], 0.0)
    buf_ref[halo:halo + tl] = cur_ref[0]
    buf_ref[halo + tl:2 * halo + tl] = jnp.where(i < pl.num_programs(1) - 1, next_ref[0], 0.0)
    acc = jnp.broadcast_to(b_ref[...], (tl, b_ref.shape[1]))
    for k in range(SSD_CONV):
        start = halo + k - SSD_CONV // 2
        acc = acc + w_ref[k:k + 1, :] * buf_ref[start:start + tl, :]
    o_ref[0] = _silu(acc)


def _conv_silu(p3, conv_w8, conv_b, tl):
    b, l, _ = p3.shape
    c = SSD_CONV_DIM
    tl = min(tl, l)
    cb = HYB_XBC0 // c
    nh = l // SUBLANE
    r = tl // SUBLANE
    return pl.pallas_call(
        functools.partial(_conv_kernel, tl=tl),
        out_shape=jax.ShapeDtypeStruct((b, l, c), F32),
        grid=(b, l // tl),
        in_specs=[
            pl.BlockSpec((1, tl, c), lambda bb, i: (bb, i, cb)),
            pl.BlockSpec((1, SUBLANE, c), lambda bb, i: (bb, jnp.maximum(i * r - 1, 0), cb)),
            pl.BlockSpec((1, SUBLANE, c), lambda bb, i: (bb, jnp.minimum((i + 1) * r, nh - 1), cb)),
            pl.BlockSpec((SUBLANE, c), lambda bb, i: (0, 0)),
            pl.BlockSpec((1, c), lambda bb, i: (0, 0)),
        ],
        out_specs=pl.BlockSpec((1, tl, c), lambda bb, i: (bb, i, 0)),
        scratch_shapes=[pltpu.VMEM((tl + 2 * SUBLANE, c), F32)],
        compiler_params=_params("parallel", "parallel"),
        name="dwconv_silu",
    )(p3, p3, p3, conv_w8, conv_b)


def _ssd_kernel(xs_ref, bm_ref, cm_ref, misc_ref, h0_ref, tri_ref, e_ref, bias_ref, a_ref, *rest,
                reverse, finalize):
    if finalize:
        z_ref, y0_ref, dskip_ref, ng_ref, y_ref, hT_ref, st_ref, yb_ref = rest
    else:
        y_ref, hT_ref, st_ref, yb_ref = rest
    q = SSD_CHUNK
    gw = SSD_INNER // SSD_GROUPS
    hpg = SSD_HEADS // SSD_GROUPS

    @pl.when(pl.program_id(1) == 0)
    def _():
        st_ref[...] = h0_ref[0]

    xs = xs_ref[0]
    tri = tri_ref[...]
    visible = tri > 0.5
    lane = lax.broadcasted_iota(jnp.int32, (q, LANE), 1)
    dt_lanes = (lane >= DT_LANE0) & (lane < DT_LANE0 + SSD_HEADS)
    dt = jnp.where(dt_lanes, jax.nn.softplus(misc_ref[0] + bias_ref[...]), 0.0)
    da = dt * a_ref[...]
    cum = _split_dot(da, tri.astype(BF16), 3, left=True)
    cum_t = cum.T
    dt_t = dt.T
    edge = cum[0:1, :] if reverse else cum[q - 1:q, :]
    expcum = jnp.exp(cum)
    w_end = jnp.exp(edge - cum) * dt
    e = e_ref[...]
    expcum_x = _split_dot(expcum, e, 2)
    w_end_x = _split_dot(w_end, e, 2)
    xs_b = xs.astype(BF16)
    xw_b = (xs * w_end_x).astype(BF16)
    st = st_ref[...]
    st_b = st.astype(BF16)
    first_head = lane < SSD_HEAD_DIM
    decay_row = expcum_x[0:1, :] if reverse else expcum_x[q - 1:q, :]
    for g in range(SSD_GROUPS):
        bm = bm_ref[0][:, g * SSD_STATE:(g + 1) * SSD_STATE]
        cm_b = cm_ref[0][:, g * SSD_STATE:(g + 1) * SSD_STATE].astype(BF16)
        cb = _nt_dot(cm_b, bm.astype(BF16))
        gs = slice(g * gw, (g + 1) * gw)
        y_off = jnp.dot(cm_b, st_b[:, gs], preferred_element_type=F32) * expcum_x[:, gs]
        for jp in range(hpg // 2):
            c0 = (g * hpg + 2 * jp) * SSD_HEAD_DIM
            tiles = []
            for sub in range(2):
                col = DT_LANE0 + g * hpg + 2 * jp + sub
                seg = cum[:, col:col + 1] - cum_t[col:col + 1, :]
                lmat = jnp.exp(jnp.where(visible, seg, -jnp.inf))
                mm = (cb * lmat * dt_t[col:col + 1, :]).astype(BF16)
                tiles.append(jnp.dot(mm, xs_b[:, c0:c0 + LANE], preferred_element_type=F32))
            yb_ref[:, c0:c0 + LANE] = jnp.where(first_head, tiles[0], tiles[1]) + y_off[:, c0 - g * gw:c0 - g * gw + LANE]
        new = jnp.dot(bm.T.astype(BF16), xw_b[:, gs], preferred_element_type=F32)
        st_ref[:, gs] = st[:, gs] * decay_row[:, gs] + new

    if finalize:
        v = (yb_ref[...] + y0_ref[0] + dskip_ref[...] * xs) * _silu(z_ref[0])
        for g in range(SSD_GROUPS):
            gs = slice(g * gw, (g + 1) * gw)
            y_ref[0, :, gs] = _rms(v[:, gs], ng_ref[:, gs]).astype(y_ref.dtype)
    else:
        y_ref[0] = yb_ref[...]

    @pl.when(pl.program_id(1) == pl.num_programs(1) - 1)
    def _():
        hT_ref[0] = st_ref[...]


def _ssd_scan(xbc3, p3, h0, tri, expand, bias_row, a_row, reverse, fin=None):
    b, l, _ = xbc3.shape
    nc = l // SSD_CHUNK
    cidx = (lambda s: nc - 1 - s) if reverse else (lambda s: s)
    q = SSD_CHUNK
    in_specs = [
        pl.BlockSpec((1, q, SSD_INNER), lambda bb, s: (bb, cidx(s), 0)),
        pl.BlockSpec((1, q, SSD_BC), lambda bb, s: (bb, cidx(s), SSD_INNER // SSD_BC)),
        pl.BlockSpec((1, q, SSD_BC), lambda bb, s: (bb, cidx(s), SSD_INNER // SSD_BC + 1)),
        pl.BlockSpec((1, q, LANE), lambda bb, s: (bb, cidx(s), HYB_MISC0 // LANE)),
        pl.BlockSpec((1, SSD_STATE, SSD_INNER), lambda bb, s: (bb, 0, 0)),
        pl.BlockSpec((q, q), lambda bb, s: (0, 0)),
        pl.BlockSpec((LANE, SSD_INNER), lambda bb, s: (0, 0)),
        pl.BlockSpec((1, LANE), lambda bb, s: (0, 0)),
        pl.BlockSpec((1, LANE), lambda bb, s: (0, 0)),
    ]
    args = [xbc3, xbc3, xbc3, p3, h0, tri, expand, bias_row, a_row]
    if fin is not None:
        y0, dskip_row, normg_row = fin
        in_specs += [
            pl.BlockSpec((1, q, SSD_INNER), lambda bb, s: (bb, cidx(s), HYB_Z0 // SSD_INNER)),
            pl.BlockSpec((1, q, SSD_INNER), lambda bb, s: (bb, cidx(s), 0)),
            pl.BlockSpec((1, SSD_INNER), lambda bb, s: (0, 0)),
            pl.BlockSpec((1, SSD_INNER), lambda bb, s: (0, 0)),
        ]
        args += [p3, y0, dskip_row, normg_row]
    return pl.pallas_call(
        functools.partial(_ssd_kernel, reverse=reverse, finalize=fin is not None),
        out_shape=(jax.ShapeDtypeStruct((b, l, SSD_INNER), BF16 if fin is not None else F32),
                   jax.ShapeDtypeStruct((b, SSD_STATE, SSD_INNER), F32)),
        grid=(b, nc),
        in_specs=in_specs,
        out_specs=(pl.BlockSpec((1, q, SSD_INNER), lambda bb, s: (bb, cidx(s), 0)),
                   pl.BlockSpec((1, SSD_STATE, SSD_INNER), lambda bb, s: (bb, 0, 0))),
        scratch_shapes=[pltpu.VMEM((SSD_STATE, SSD_INNER), F32), pltpu.VMEM((q, SSD_INNER), F32)],
        compiler_params=_params("parallel", "arbitrary"),
        name="ssd_scan_bwd" if reverse else "ssd_scan_fwd",
    )(*args)


def _qproj_kernel(x_ref, g_ref, w_ref, cos_ref, sin_ref, o_ref):
    h = _rms(x_ref[...], g_ref[...]).astype(BF16)
    r = jnp.dot(h, w_ref[...], preferred_element_type=F32)
    main = MLA_HEADS * MLA_QK_PAD
    cos, sin = cos_ref[...], sin_ref[...]
    for hd in range(MLA_HEADS):
        c0 = hd * MLA_QK_PAD
        o_ref[:, c0:c0 + LANE] = r[:, c0:c0 + LANE].astype(o_ref.dtype)
        swapped = r[:, main + hd * LANE:main + (hd + 1) * LANE]
        o_ref[:, c0 + LANE:c0 + 2 * LANE] = (r[:, c0 + LANE:c0 + 2 * LANE] * cos + swapped * sin).astype(o_ref.dtype)


def _qproj(p, g, w, cos, sin, rows_per_batch, tm):
    m = p.shape[0]
    tm = min(tm, m, rows_per_batch)
    nb = rows_per_batch // tm
    n_out = MLA_HEADS * MLA_QK_PAD
    return pl.pallas_call(
        _qproj_kernel,
        out_shape=jax.ShapeDtypeStruct((m, n_out), BF16),
        grid=(m // tm,),
        in_specs=[
            pl.BlockSpec((tm, MLA_Q_RANK), lambda i: (i, HYB_CQ0 // MLA_Q_RANK)),
            pl.BlockSpec((1, MLA_Q_RANK), lambda i: (0, 0)),
            pl.BlockSpec(w.shape, lambda i: (0, 0)),
            pl.BlockSpec((tm, LANE), lambda i: (i % nb, 0)),
            pl.BlockSpec((tm, LANE), lambda i: (i % nb, 0)),
        ],
        out_specs=pl.BlockSpec((tm, n_out), lambda i: (i, 0)),
        compiler_params=_params("parallel"),
        name="mla_q_proj",
    )(p, g, w, cos, sin)


def _kvproj_kernel(x_ref, misc_ref, misc2_ref, g_ref, w_ref, cos_ref, sin_ref, k_ref, v_ref):
    h = _rms(x_ref[...], g_ref[...]).astype(BF16)
    r = jnp.dot(h, w_ref[...], preferred_element_type=F32)
    kr = (misc_ref[...] * cos_ref[...] + misc2_ref[...] * sin_ref[...]).astype(k_ref.dtype)
    for hd in range(MLA_HEADS):
        c0 = hd * MLA_QK_PAD
        k_ref[:, c0:c0 + LANE] = r[:, hd * MLA_NOPE:(hd + 1) * MLA_NOPE].astype(k_ref.dtype)
        k_ref[:, c0 + LANE:c0 + 2 * LANE] = kr
    v_ref[...] = r[:, MLA_HEADS * MLA_NOPE:].astype(v_ref.dtype)


def _kvproj(p, g, w, cos, sin, rows_per_batch, tm):
    m = p.shape[0]
    tm = min(tm, m, rows_per_batch)
    nb = rows_per_batch // tm
    nk, nv = MLA_HEADS * MLA_QK_PAD, MLA_HEADS * MLA_V
    return pl.pallas_call(
        _kvproj_kernel,
        out_shape=(jax.ShapeDtypeStruct((m, nk), BF16), jax.ShapeDtypeStruct((m, nv), BF16)),
        grid=(m // tm,),
        in_specs=[
            pl.BlockSpec((tm, MLA_KV_RANK), lambda i: (i, HYB_CKV0 // MLA_KV_RANK)),
            pl.BlockSpec((tm, LANE), lambda i: (i, HYB_MISC0 // LANE)),
            pl.BlockSpec((tm, LANE), lambda i: (i, HYB_MISC2 // LANE)),
            pl.BlockSpec((1, MLA_KV_RANK), lambda i: (0, 0)),
            pl.BlockSpec(w.shape, lambda i: (0, 0)),
            pl.BlockSpec((tm, LANE), lambda i: (i % nb, 0)),
            pl.BlockSpec((tm, LANE), lambda i: (i % nb, 0)),
        ],
        out_specs=(pl.BlockSpec((tm, nk), lambda i: (i, 0)), pl.BlockSpec((tm, nv), lambda i: (i, 0))),
        compiler_params=_params("parallel"),
        name="mla_kv_proj",
    )(p, p, p, g, w, cos, sin)


def _attn_kernel(q_ref, kc_ref, vc_ref, *rest, with_latent):
    if with_latent:
        k_ref, v_ref, o_ref, m_ref, l_ref, acc_ref = rest
    else:
        o_ref, m_ref, l_ref, acc_ref = rest
    ki = pl.program_id(2)

    def update(hd, kblk, vblk):
        s = _nt_dot(q_ref[:, hd * MLA_QK_PAD:(hd + 1) * MLA_QK_PAD], kblk)
        tiles = s.shape[1] // LANE
        m_prev = m_ref[hd]
        m_new = jnp.maximum(m_prev, jnp.max(s, axis=-1, keepdims=True))
        alpha = jnp.exp2((m_prev - m_new) * EXP2_SCALE)
        p = jnp.exp2((s - jnp.tile(m_new, (1, tiles))) * EXP2_SCALE)
        psum = p[:, 0:LANE]
        for c in range(1, tiles):
            psum = psum + p[:, c * LANE:(c + 1) * LANE]
        l_ref[hd] = alpha * l_ref[hd] + psum
        vs = slice(hd * MLA_V, (hd + 1) * MLA_V)
        acc_ref[:, vs] = alpha * acc_ref[:, vs] + jnp.dot(p.astype(BF16), vblk, preferred_element_type=F32)
        m_ref[hd] = m_new

    @pl.when(ki == 0)
    def _():
        m_ref[...] = jnp.full(m_ref.shape, -jnp.inf, F32)
        l_ref[...] = jnp.zeros(l_ref.shape, F32)
        acc_ref[...] = jnp.zeros(acc_ref.shape, F32)
        for hd in range(MLA_HEADS):
            update(hd, kc_ref[:, hd * MLA_QK_PAD:(hd + 1) * MLA_QK_PAD], vc_ref[:, hd * MLA_V:(hd + 1) * MLA_V])

    if with_latent:
        for hd in range(MLA_HEADS):
            update(hd, k_ref[:, hd * MLA_QK_PAD:(hd + 1) * MLA_QK_PAD], v_ref[:, hd * MLA_V:(hd + 1) * MLA_V])

    @pl.when(ki == pl.num_programs(2) - 1)
    def _():
        for hd in range(MLA_HEADS):
            vs = slice(hd * MLA_V, (hd + 1) * MLA_V)
            o_ref[:, vs] = (acc_ref[:, vs] / jnp.sum(l_ref[hd], axis=-1, keepdims=True)).astype(o_ref.dtype)


def _attention(q, kc, vc, k, v, batch, tq, tk):
    m = q.shape[0]
    lq = m // batch
    lc = kc.shape[0] // batch
    tq = min(tq, lq)
    nq = lq // tq
    nq_pad, nv = MLA_HEADS * MLA_QK_PAD, MLA_HEADS * MLA_V
    in_specs = [
        pl.BlockSpec((tq, nq_pad), lambda b, i, j: (b * nq + i, 0)),
        pl.BlockSpec((lc, nq_pad), lambda b, i, j: (b, 0)),
        pl.BlockSpec((lc, nv), lambda b, i, j: (b, 0)),
    ]
    args = [q, kc, vc]
    nk = 1
    if k is not None:
        lk = k.shape[0] // batch
        tk = min(tk, lk)
        nk = lk // tk
        in_specs += [
            pl.BlockSpec((tk, nq_pad), lambda b, i, j: (b * nk + j, 0)),
            pl.BlockSpec((tk, nv), lambda b, i, j: (b * nk + j, 0)),
        ]
        args += [k, v]
    return pl.pallas_call(
        functools.partial(_attn_kernel, with_latent=k is not None),
        out_shape=jax.ShapeDtypeStruct((m, nv), BF16),
        grid=(batch, nq, nk),
        in_specs=in_specs,
        out_specs=pl.BlockSpec((tq, nv), lambda b, i, j: (b * nq + i, 0)),
        scratch_shapes=[pltpu.VMEM((MLA_HEADS, tq, LANE), F32), pltpu.VMEM((MLA_HEADS, tq, LANE), F32),
                        pltpu.VMEM((tq, nv), F32)],
        compiler_params=_params("parallel", "parallel", "arbitrary"),
        name="mla_attention",
    )(*args)


def _outproj_kernel(a1_ref, a2_ref, w1_ref, w2_ref, x_ref, gate_ref, o_ref):
    acc = jnp.dot(a1_ref[...], w1_ref[...], preferred_element_type=F32)
    acc = acc + jnp.dot(a2_ref[...], w2_ref[...], preferred_element_type=F32)
    o_ref[...] = x_ref[...] + gate_ref[0] * acc


def _outproj(a1, a2, w1, w2, x, mod, k_gate, rows_per_batch, tm):
    m, d = x.shape
    tm = min(tm, m)
    bidx = lambda i: (i * tm) // rows_per_batch
    return pl.pallas_call(
        _outproj_kernel,
        out_shape=jax.ShapeDtypeStruct((m, d), F32),
        grid=(m // tm,),
        in_specs=[
            pl.BlockSpec((tm, a1.shape[1]), lambda i: (i, 0)),
            pl.BlockSpec((tm, a2.shape[1]), lambda i: (i, 0)),
            pl.BlockSpec(w1.shape, lambda i: (0, 0)),
            pl.BlockSpec(w2.shape, lambda i: (0, 0)),
            pl.BlockSpec((tm, d), lambda i: (i, 0)),
            pl.BlockSpec((1, 1, d), lambda i: (bidx(i) * N_MOD + k_gate, 0, 0)),
        ],
        out_specs=pl.BlockSpec((tm, d), lambda i: (i, 0)),
        compiler_params=_params("parallel"),
        name="hyb_out_proj",
    )(a1, a2, w1, w2, x, mod)


def _gmlp_kernel(x_ref, g_ref, sh_ref, sc_ref, gate_ref, win_ref, lng_ref, lnb_ref, ws_ref, bs_ref, wout_ref,
                 o_ref, uv_ref):
    x = x_ref[...]
    tm = x.shape[0]
    h = (_rms(x, g_ref[...]) * (1.0 + sc_ref[0]) + sh_ref[0]).astype(BF16)
    uv = _gelu(jnp.dot(h, win_ref[...], preferred_element_type=F32))
    u, v = uv[:, :GM_INNER], uv[:, GM_INNER:]
    mu = jnp.mean(v, axis=-1, keepdims=True)
    var = jnp.mean(jnp.square(v - mu), axis=-1, keepdims=True)
    vn = ((v - mu) * lax.rsqrt(var + EPS) * lng_ref[...] + lnb_ref[...]).astype(BF16)
    gw = GM_INNER // GM_GROUPS
    for c in range(tm // GM_CHUNK):
        rs = slice(c * GM_CHUNK, (c + 1) * GM_CHUNK)
        for g in range(GM_GROUPS):
            cs = slice(g * gw, (g + 1) * gw)
            sv = jnp.dot(ws_ref[g], vn[rs, cs], preferred_element_type=F32) + bs_ref[:, cs]
            uv_ref[rs, cs] = (u[rs, cs] * sv).astype(BF16)
    o_ref[...] = x + gate_ref[0] * jnp.dot(uv_ref[...], wout_ref[...], preferred_element_type=F32)


def _gmlp(x, g, mod, rows_per_batch, w_in, ln_g, ln_b, ws, bs_full, w_out, tm):
    m, d = x.shape
    tm = min(tm, m)
    bidx = lambda i: (i * tm) // rows_per_batch
    const = lambda shape: pl.BlockSpec(shape, lambda i: (0,) * len(shape))
    modspec = lambda k: pl.BlockSpec((1, 1, d), lambda i: (bidx(i) * N_MOD + k, 0, 0))
    return pl.pallas_call(
        _gmlp_kernel,
        out_shape=jax.ShapeDtypeStruct((m, d), F32),
        grid=(m // tm,),
        in_specs=[
            pl.BlockSpec((tm, d), lambda i: (i, 0)), const((1, d)), modspec(0), modspec(1), modspec(2),
            const(w_in.shape), const((1, GM_INNER)), const((1, GM_INNER)), const(ws.shape), const(bs_full.shape),
            const(w_out.shape),
        ],
        out_specs=pl.BlockSpec((tm, d), lambda i: (i, 0)),
        scratch_shapes=[pltpu.VMEM((tm, GM_INNER), BF16)],
        compiler_params=_params("parallel"),
        name="chunk_gmlp",
    )(x, g, mod, mod, mod, w_in, ln_g, ln_b, ws, bs_full, w_out)


def _top_values(x, k, want_rank=False):
    vals = []
    rank = jnp.full(x.shape, float(k), F32) if want_rank else None
    for r in range(k):
        mx = jnp.max(x, axis=0, keepdims=True)
        vals.append(mx)
        hit = x == mx
        if want_rank:
            rank = jnp.where(hit, float(r), rank)
        if r + 1 < k:
            x = jnp.where(hit, -jnp.inf, x)
    return (vals, rank) if want_rank else vals


def _peer_stats_kernel(x_ref, g_ref, sh_ref, sc_ref, wq_ref, k1_ref, k2_ref, t_ref, a_ref, n_ref, b_ref, r2_ref,
                       q_ref, cand_ref):
    h = (_rms(x_ref[...], g_ref[...]) * (1.0 + sc_ref[0]) + sh_ref[0]).astype(BF16)
    t_ref[...] = h
    q_ref[...] = jnp.dot(h, wq_ref[...], preferred_element_type=F32).astype(BF16)
    half = PEER_DKEY // 2
    tm = h.shape[0]
    n_pairs = len(PEER_PAIRS)
    cand_ref[n_pairs:, :] = jnp.full((cand_ref.shape[0] - n_pairs, tm), -jnp.inf, F32)

    def head(hd, carry):
        c0 = pl.multiple_of(hd * PEER_DKEY, PEER_DKEY)
        s1 = _nt_dot(k1_ref[hd], q_ref[:, pl.ds(c0, half)])
        s2 = _nt_dot(k2_ref[hd], q_ref[:, pl.ds(c0 + half, half)])
        v1 = _top_values(s1, PEER_TOPK)
        v2, rank2 = _top_values(s2, PEER_TOPK, want_rank=True)
        for idx, (a, b) in enumerate(PEER_PAIRS):
            cand_ref[idx:idx + 1, :] = v1[a] + v2[b]
        tops = _top_values(cand_ref[...], PEER_TOPK)
        thr = tops[PEER_TOPK - 1]
        z = tops[0] * 0.0
        for tv in tops:
            z = z + jnp.exp(tv - tops[0])
        n = jnp.zeros(s1.shape, F32)
        for vb in v2:
            n = n + jnp.where(s1 + vb >= thr, 1.0, 0.0)
        a_ref[hd] = jnp.exp(s1 - v1[0]) * (0.5 / z)
        n_ref[hd] = n
        bvals = jnp.exp(s2 - v2[0])
        for c in range(tm // LANE):
            b_ref[hd, c] = bvals[:, c * LANE:(c + 1) * LANE]
            r2_ref[hd, c] = rank2[:, c * LANE:(c + 1) * LANE]
        return carry

    lax.fori_loop(0, PEER_HEADS, head, 0)


def _peer_stats(x, g, mod, rows_per_batch, wq, k1, k2, tm):
    m, d = x.shape
    tm = min(tm, m)
    bidx = lambda i: (i * tm) // rows_per_batch
    const = lambda shape: pl.BlockSpec(shape, lambda i: (0,) * len(shape))
    modspec = lambda k: pl.BlockSpec((1, 1, d), lambda i: (bidx(i) * N_MOD + k, 0, 0))
    return pl.pallas_call(
        _peer_stats_kernel,
        out_shape=(jax.ShapeDtypeStruct((m, d), BF16),
                   jax.ShapeDtypeStruct((PEER_HEADS, PEER_NKEYS, m), F32),
                   jax.ShapeDtypeStruct((PEER_HEADS, PEER_NKEYS, m), F32),
                   jax.ShapeDtypeStruct((PEER_HEADS, m // LANE, PEER_NKEYS, LANE), F32),
                   jax.ShapeDtypeStruct((PEER_HEADS, m // LANE, PEER_NKEYS, LANE), F32)),
        grid=(m // tm,),
        in_specs=[pl.BlockSpec((tm, d), lambda i: (i, 0)), const((1, d)), modspec(3), modspec(4),
                  const(wq.shape), const(k1.shape), const(k2.shape)],
        out_specs=(pl.BlockSpec((tm, d), lambda i: (i, 0)),) + tuple(
            pl.BlockSpec((PEER_HEADS, PEER_NKEYS, tm), lambda i: (0, 0, i)) for _ in range(2)) + tuple(
            pl.BlockSpec((PEER_HEADS, tm // LANE, PEER_NKEYS, LANE), lambda i: (0, i, 0, 0)) for _ in range(2)),
        scratch_shapes=[pltpu.VMEM((tm, PEER_HEADS * PEER_DKEY), BF16),
                        pltpu.VMEM((-(-len(PEER_PAIRS) // SUBLANE) * SUBLANE, tm), F32)],
        compiler_params=_params("parallel"),
        name="peer_topk",
    )(x, g, mod, mod, wq, k1, k2)


def _peer_expert_kernel(t_ref, u_ref, vt_ref, a_ref, n_ref, b_ref, r2_ref, x_ref, gate_ref, o_ref,
                        acc_ref, s0_ref, s1_ref, g0_ref, g1_ref, *, te):
    j = pl.program_id(1)
    d, tm = acc_ref.shape
    rows = te // PEER_NKEYS
    dr = d // rows
    sub = PEER_NKEYS // 4

    @pl.when(j == 0)
    def _():
        for ref in (acc_ref, s0_ref, s1_ref, g0_ref, g1_ref):
            ref[...] = jnp.zeros(ref.shape, ref.dtype)

    def stages(s_new, s_old, g_new, g_old):
        for r in range(rows):
            rs = slice(r * PEER_NKEYS, (r + 1) * PEER_NKEYS)
            s_new[rs, :] = _nt_dot(u_ref[rs, :], t_ref[...])
            for c in range(tm // LANE):
                cs = slice(c * LANE, (c + 1) * LANE)
                for k0 in range(0, PEER_NKEYS, sub):
                    ks = slice(k0, k0 + sub)
                    w = None
                    for hd in range(PEER_HEADS):
                        sel = jnp.where(r2_ref[hd, c, ks, :] < n_ref[hd, r:r + 1, cs], b_ref[hd, c, ks, :], 0.0)
                        term = a_ref[hd, r:r + 1, cs] * sel
                        w = term if w is None else w + term
                    es = slice(r * PEER_NKEYS + k0, r * PEER_NKEYS + k0 + sub)
                    sc = s_old[es, cs]
                    g_new[es, cs] = (sc * (1.0 + lax.erf(sc * (2.0 ** -0.5))) * w).astype(BF16)
            ds = slice(r * dr, (r + 1) * dr)
            acc_ref[ds, :] += jnp.dot(vt_ref[ds, :], g_old[...], preferred_element_type=F32)

    @pl.when(lax.rem(j, 2) == 0)
    def _():
        stages(s0_ref, s1_ref, g1_ref, g0_ref)

    @pl.when(lax.rem(j, 2) == 1)
    def _():
        stages(s1_ref, s0_ref, g0_ref, g1_ref)

    @pl.when(j == pl.num_programs(1) - 1)
    def _():
        o_ref[...] = x_ref[...] + gate_ref[0] * acc_ref[...].T


def _peer_experts(t, u, vt, a, cnt, b, r2, x, mod, k_gate, rows_per_batch, tm, te):
    m, d = x.shape
    tm = min(tm, m)
    ne = u.shape[0]
    bidx = lambda i: (i * tm) // rows_per_batch
    rows = te // PEER_NKEYS
    n = ne // te
    tile = lambda j, lag: jnp.clip(j - lag, 0, n - 1)
    return pl.pallas_call(
        functools.partial(_peer_expert_kernel, te=te),
        out_shape=jax.ShapeDtypeStruct((m, d), F32),
        grid=(m // tm, n + 2),
        in_specs=[
            pl.BlockSpec((tm, d), lambda i, j: (i, 0)),
            pl.BlockSpec((te, d), lambda i, j: (tile(j, 0), 0)),
            pl.BlockSpec((d, te), lambda i, j: (0, tile(j, 2))),
            pl.BlockSpec((PEER_HEADS, rows, tm), lambda i, j: (0, tile(j, 1), i)),
            pl.BlockSpec((PEER_HEADS, rows, tm), lambda i, j: (0, tile(j, 1), i)),
            pl.BlockSpec((PEER_HEADS, tm // LANE, PEER_NKEYS, LANE), lambda i, j: (0, i, 0, 0)),
            pl.BlockSpec((PEER_HEADS, tm // LANE, PEER_NKEYS, LANE), lambda i, j: (0, i, 0, 0)),
            pl.BlockSpec((tm, d), lambda i, j: (i, 0)),
            pl.BlockSpec((1, 1, d), lambda i, j: (bidx(i) * N_MOD + k_gate, 0, 0)),
        ],
        out_specs=pl.BlockSpec((tm, d), lambda i, j: (i, 0)),
        scratch_shapes=[pltpu.VMEM((d, tm), F32), pltpu.VMEM((te, tm), F32), pltpu.VMEM((te, tm), F32),
                        pltpu.VMEM((te, tm), BF16), pltpu.VMEM((te, tm), BF16)],
        compiler_params=_params("parallel", "arbitrary"),
        name="peer_experts",
    )(t, u, vt, a, cnt, b, r2, x, mod)


def _final_norm_kernel(x_ref, g_ref, o_ref):
    o_ref[...] = _rms(x_ref[...], g_ref[...])


def _final_norm(x, g, tm):
    m, d = x.shape
    tm = min(tm, m)
    return pl.pallas_call(
        _final_norm_kernel,
        out_shape=jax.ShapeDtypeStruct((m, d), F32),
        grid=(m // tm,),
        in_specs=[pl.BlockSpec((tm, d), lambda i: (i, 0)), pl.BlockSpec((1, d), lambda i: (0, 0))],
        out_specs=pl.BlockSpec((tm, d), lambda i: (i, 0)),
        compiler_params=_params("parallel"),
        name="final_rmsnorm",
    )(x, g)


def _rope_tables(n_lat, n_ctx):
    rows = n_lat // GRID_W
    row = jnp.broadcast_to(jnp.arange(rows, dtype=F32)[:, None], (rows, GRID_W)).reshape(-1)
    col = jnp.broadcast_to(jnp.arange(GRID_W, dtype=F32)[None, :], (rows, GRID_W)).reshape(-1)
    n_freq = MLA_ROPE // 4
    inv = ROPE_BASE ** (-jnp.arange(n_freq, dtype=F32) / n_freq)
    ang = jnp.concatenate([row[:, None] * inv, col[:, None] * inv], axis=-1)
    cos, sin = jnp.cos(ang), jnp.sin(ang)
    pad = jnp.zeros((n_lat, LANE - MLA_ROPE), F32)
    cos_l = jnp.concatenate([cos, cos, pad], axis=-1)
    sin_l = jnp.concatenate([-sin, sin, pad], axis=-1)
    cos_c = jnp.concatenate([jnp.ones((n_ctx, MLA_ROPE), F32), jnp.zeros((n_ctx, LANE - MLA_ROPE), F32)], axis=-1)
    return (cos_l, sin_l), (cos_c, jnp.zeros((n_ctx, LANE), F32))


def _swap_halves(w):
    half = w.shape[-1] // 2
    return jnp.concatenate([w[..., half:], w[..., :half]], axis=-1)


def _prep_hybrid(w_in, conv_w, conv_b, a_log, dt_bias, d_skip, norm_g, q_norm_g, w_qb, kv_norm_g, w_kvb, w_out):
    d = w_in.shape[0]
    o = 0
    pieces = {}
    for name, width in (("z", SSD_INNER), ("xbc", SSD_CONV_DIM), ("dt", SSD_HEADS), ("cq", MLA_Q_RANK),
                        ("ckv", MLA_KV_RANK), ("kr", MLA_ROPE)):
        pieces[name] = w_in[:, o:o + width]
        o += width
    zeros = lambda n: jnp.zeros((d, n), w_in.dtype)
    w_hyb = jnp.concatenate([
        pieces["z"], pieces["kr"], pieces["dt"], zeros(LANE - MLA_ROPE - SSD_HEADS), pieces["cq"], pieces["xbc"],
        pieces["ckv"], _swap_halves(pieces["kr"]), zeros(LANE - MLA_ROPE)], axis=1).astype(BF16)
    conv_w8 = jnp.concatenate([conv_w.T, jnp.zeros((SUBLANE - SSD_CONV, SSD_CONV_DIM), F32)], axis=0)
    lane_row = lambda v: jnp.zeros((1, LANE), F32).at[0, DT_LANE0:DT_LANE0 + SSD_HEADS].set(v)
    dirs = [(lane_row(dt_bias[k]), lane_row(-jnp.exp(a_log[k]))) for k in range(2)]
    expand = jnp.zeros((LANE, SSD_INNER), F32).at[DT_LANE0:DT_LANE0 + SSD_HEADS].set(
        jnp.repeat(jnp.eye(SSD_HEADS, dtype=F32), SSD_HEAD_DIM, axis=1)).astype(BF16)
    wq3 = w_qb.reshape(MLA_Q_RANK, MLA_HEADS, MLA_NOPE + MLA_ROPE)
    zq = jnp.zeros((MLA_Q_RANK, MLA_HEADS, LANE - MLA_ROPE), w_qb.dtype)
    q_main = jnp.concatenate([wq3, zq], axis=-1).reshape(MLA_Q_RANK, MLA_HEADS * MLA_QK_PAD)
    q_swap = jnp.concatenate([_swap_halves(wq3[..., MLA_NOPE:]), zq], axis=-1).reshape(MLA_Q_RANK, MLA_HEADS * LANE)
    wq = jnp.concatenate([q_main, q_swap], axis=1).astype(BF16)
    wkv3 = w_kvb.reshape(MLA_KV_RANK, MLA_HEADS, MLA_NOPE + MLA_V)
    wkv = jnp.concatenate([wkv3[..., :MLA_NOPE].reshape(MLA_KV_RANK, -1), wkv3[..., MLA_NOPE:].reshape(MLA_KV_RANK, -1)],
                          axis=1).astype(BF16)
    return dict(
        w_hyb=w_hyb, conv_w8=conv_w8, conv_b=conv_b[None, :], dirs=dirs, expand=expand,
        dskip=jnp.repeat(d_skip, SSD_HEAD_DIM)[None, :], normg=norm_g[None, :],
        q_norm_g=q_norm_g[None, :], wq=wq, kv_norm_g=kv_norm_g[None, :], wkv=wkv,
        w_out1=w_out[:SSD_INNER].astype(BF16), w_out2=w_out[SSD_INNER:].astype(BF16))


def _hybrid_layer(xl, xc, batch, g1, mod_l, mod_c, hp, ropes, update_ctx):
    n_lat, n_ctx = xl.shape[0] // batch, xc.shape[0] // batch
    (cos_l, sin_l), (cos_c, sin_c) = ropes
    q = SSD_CHUNK
    tri_f = jnp.tril(jnp.ones((q, q), F32))
    tri_b = jnp.triu(jnp.ones((q, q), F32))
    pl_ = _nmm(xl, g1, mod_l, 0, 1, n_lat, hp["w_hyb"], 512, HYB_COLS // 3)
    pc_ = _nmm(xc, g1, mod_c, 0, 1, xc.shape[0], hp["w_hyb"], 256, HYB_COLS // 3)
    pl3, pc3 = pl_.reshape(batch, n_lat, HYB_COLS), pc_.reshape(batch, n_ctx, HYB_COLS)
    xbc_l = _conv_silu(pl3, hp["conv_w8"], hp["conv_b"], 512)
    xbc_c = _conv_silu(pc3, hp["conv_w8"], hp["conv_b"], 512)
    h0 = jnp.zeros((batch, SSD_STATE, SSD_INNER), F32)
    (bias0, a0), (bias1, a1) = hp["dirs"]
    yc0, hc0 = _ssd_scan(xbc_c, pc3, h0, tri_f, hp["expand"], bias0, a0, False)
    yl0, _ = _ssd_scan(xbc_l, pl3, hc0, tri_f, hp["expand"], bias0, a0, False)
    fin = lambda y0: (y0, hp["dskip"], hp["normg"])
    sc, hc1 = _ssd_scan(xbc_c, pc3, h0, tri_b, hp["expand"], bias1, a1, True, fin(yc0))
    sl, _ = _ssd_scan(xbc_l, pl3, hc1, tri_b, hp["expand"], bias1, a1, True, fin(yl0))
    ql = _qproj(pl_, hp["q_norm_g"], hp["wq"], cos_l, sin_l, n_lat, 512)
    kl, vl = _kvproj(pl_, hp["kv_norm_g"], hp["wkv"], cos_l, sin_l, n_lat, 512)
    kc, vc = _kvproj(pc_, hp["kv_norm_g"], hp["wkv"], cos_c, sin_c, n_ctx, 256)
    att_l = _attention(ql, kc, vc, kl, vl, batch, 512, 1024)
    xl_new = _outproj(sl.reshape(-1, SSD_INNER), att_l, hp["w_out1"], hp["w_out2"], xl, mod_l, 2, n_lat, 512)
    xc_new = xc
    if update_ctx:
        qc = _qproj(pc_, hp["q_norm_g"], hp["wq"], cos_c, sin_c, n_ctx, 256)
        att_c = _attention(qc, kc, vc, None, None, batch, 256, 256)
        xc_new = _outproj(sc.reshape(-1, SSD_INNER), att_c, hp["w_out1"], hp["w_out2"], xc, mod_c, 2, xc.shape[0], 256)
    return xl_new, xc_new


def _peer_layer(x, g2, mod, rows_per_batch, pp):
    t, a, cnt, b, r2 = _peer_stats(x, g2, mod, rows_per_batch, pp["wq"], pp["k1"], pp["k2"], 256)
    return _peer_experts(t, pp["u"], pp["vt"], a, cnt, b, r2, x, mod, 5, rows_per_batch, 512, 1024)


def kernel(x, c, ctx, c_ctx, ada_w, ada_b, norm1_g, norm2_g, hyb_w_in, ssd_conv_w, ssd_conv_b, ssd_a_log,
           ssd_dt_bias, ssd_d, ssd_norm_g, mla_q_norm_g, mla_w_qb, mla_kv_norm_g, mla_w_kvb, hyb_w_out, gm_w_in,
           gm_ln_g, gm_ln_b, gm_ws, gm_bs, gm_w_out, peer_wq, peer_k1, peer_k2, peer_u, peer_v, final_norm_g):
    batch, n_lat, d = x.shape
    n_ctx = ctx.shape[1]
    depth = ada_w.shape[0]
    xl = x.reshape(batch * n_lat, d)
    xc = ctx.reshape(batch * n_ctx, d)
    n_cond = -(-(batch + 1) // SUBLANE) * SUBLANE
    conds = jnp.zeros((n_cond, d), F32).at[:batch].set(c).at[batch].set(c_ctx)
    mods = _mods(conds, ada_w, ada_b)
    ropes = _rope_tables(n_lat, n_ctx)
    for layer in range(depth):
        i = layer // 2
        even = layer % 2 == 0
        keep_ctx = any(j % 2 == 0 for j in range(layer + 1, depth))
        mod_l = mods[layer, :batch].reshape(batch * N_MOD, 1, d)
        mod_c = mods[layer, batch].reshape(N_MOD, 1, d)
        g1, g2 = norm1_g[layer][None, :], norm2_g[layer][None, :]
        if even:
            hp = _prep_hybrid(hyb_w_in[i], ssd_conv_w[i], ssd_conv_b[i], ssd_a_log[i], ssd_dt_bias[i], ssd_d[i],
                              ssd_norm_g[i], mla_q_norm_g[i], mla_w_qb[i], mla_kv_norm_g[i], mla_w_kvb[i], hyb_w_out[i])
            xl, xc = _hybrid_layer(xl, xc, batch, g1, mod_l, mod_c, hp, ropes, keep_ctx)
        else:
            gp = (gm_w_in[i].astype(BF16), gm_ln_g[i][None, :], gm_ln_b[i][None, :], gm_ws[i].astype(BF16),
                  jnp.repeat(gm_bs[i].T, GM_INNER // GM_GROUPS, axis=1), gm_w_out[i].astype(BF16))
            xl = _gmlp(xl, g1, mod_l, n_lat, *gp, 256)
            if keep_ctx:
                xc = _gmlp(xc, g1, mod_c, xc.shape[0], *gp, 256)
        pp = dict(wq=peer_wq[layer].astype(BF16), k1=peer_k1[layer].astype(BF16), k2=peer_k2[layer].astype(BF16),
                  u=peer_u[layer].astype(BF16), vt=peer_v[layer].T.astype(BF16))
        xl = _peer_layer(xl, g2, mod_l, n_lat, pp)
        if keep_ctx:
            xc = _peer_layer(xc, g2, mod_c, xc.shape[0], pp)
    return _final_norm(xl, final_norm_g[None, :], 512).reshape(batch, n_lat, d)
```

```python
import functools

import jax
import jax.numpy as jnp
from jax import lax
from jax.experimental import pallas as pl
from jax.experimental.pallas import tpu as pltpu

F32, BF16 = jnp.float32, jnp.bfloat16
EPS = 1e-6
LANE = 128
SUBLANE = 8
VMEM_LIMIT = 56 * 1024 * 1024

D_MODEL = 1024
DEPTH = 4
GRID_W = 64
N_MOD = 6
SSD_HEADS = 16
SSD_HEAD_DIM = 64
SSD_INNER = SSD_HEADS * SSD_HEAD_DIM
SSD_GROUPS = 2
SSD_STATE = 128
SSD_BC = SSD_GROUPS * SSD_STATE
SSD_CONV = 5
SSD_CONV_DIM = SSD_INNER + 2 * SSD_BC
SSD_CHUNK = 128
MLA_HEADS = 8
MLA_Q_RANK = 384
MLA_KV_RANK = 256
MLA_NOPE = 128
MLA_ROPE = 64
MLA_V = 128
MLA_SCALE = (MLA_NOPE + MLA_ROPE) ** -0.5
EXP2_SCALE = MLA_SCALE * 1.4426950408889634
ROPE_BASE = 10000.0
GM_CHUNK = 128
GM_INNER = 2 * D_MODEL
GM_GROUPS = 8
PEER_HEADS = 8
PEER_NKEYS = 128
PEER_EXPERTS = PEER_NKEYS * PEER_NKEYS
PEER_DKEY = 256
PEER_TOPK = 16

HYB_Z0 = 0
HYB_MISC0 = 1024
HYB_CQ0 = 1152
HYB_XBC0 = 1536
HYB_CKV0 = 3072
HYB_MISC2 = 3328
HYB_COLS = 3456
DT_LANE0 = MLA_ROPE
MLA_QK_PAD = 256

PEER_PAIRS = tuple((a, b) for a in range(PEER_TOPK) for b in range(PEER_TOPK) if (a + 1) * (b + 1) <= PEER_TOPK)


def _params(*sem, flags=None):
    return pltpu.CompilerParams(dimension_semantics=sem, vmem_limit_bytes=VMEM_LIMIT, flags=flags)


def _rms(x, g):
    return x * lax.rsqrt(jnp.mean(x * x, axis=-1, keepdims=True) + EPS) * g


def _silu(x):
    return x * jax.nn.sigmoid(x)


def _gelu(x):
    return 0.5 * x * (1.0 + lax.erf(x * (2.0 ** -0.5)))


def _nt_dot(a, b):
    return lax.dot_general(a, b, (((1,), (1,)), ((), ())), preferred_element_type=F32)


def _split_dot(x, w, terms, left=False):
    acc = None
    rem = x
    for _ in range(terms):
        hi = rem.astype(BF16)
        part = jnp.dot(w, hi, preferred_element_type=F32) if left else jnp.dot(hi, w, preferred_element_type=F32)
        acc = part if acc is None else acc + part
        rem = rem - hi.astype(F32)
    return acc


def _mods_kernel(c_ref, w_ref, b_ref, o_ref):
    h = _silu(c_ref[...]).astype(BF16)
    o_ref[0] = jnp.dot(h, w_ref[0].astype(BF16), preferred_element_type=F32) + b_ref[0]


def _mods(conds, ada_w, ada_b):
    depth, d, n = ada_w.shape
    tn = 1536
    return pl.pallas_call(
        _mods_kernel,
        out_shape=jax.ShapeDtypeStruct((depth, conds.shape[0], n), F32),
        grid=(depth, n // tn),
        in_specs=[
            pl.BlockSpec(conds.shape, lambda l, j: (0, 0)),
            pl.BlockSpec((1, d, tn), lambda l, j: (l, 0, j)),
            pl.BlockSpec((1, 1, tn), lambda l, j: (l, 0, j)),
        ],
        out_specs=pl.BlockSpec((1, conds.shape[0], tn), lambda l, j: (l, 0, j)),
        compiler_params=_params("parallel", "parallel"),
        name="ada_mods",
    )(conds, ada_w, ada_b.reshape(depth, 1, n))


def _nmm_kernel(x_ref, g_ref, sh_ref, sc_ref, w_ref, o_ref, h_ref):
    @pl.when(pl.program_id(1) == 0)
    def _():
        y = _rms(x_ref[...], g_ref[...])
        h_ref[...] = (y * (1.0 + sc_ref[0]) + sh_ref[0]).astype(BF16)

    o_ref[...] = jnp.dot(h_ref[...], w_ref[...], preferred_element_type=F32).astype(o_ref.dtype)


def _nmm(x, g, mod, k_shift, k_scale, rows_per_batch, w, tm, tn):
    m, k = x.shape
    n = w.shape[1]
    tm = min(tm, m)
    bidx = lambda i: (i * tm) // rows_per_batch
    return pl.pallas_call(
        _nmm_kernel,
        out_shape=jax.ShapeDtypeStruct((m, n), F32),
        grid=(m // tm, n // tn),
        in_specs=[
            pl.BlockSpec((tm, k), lambda i, j: (i, 0)),
            pl.BlockSpec((1, k), lambda i, j: (0, 0)),
            pl.BlockSpec((1, 1, k), lambda i, j: (bidx(i) * N_MOD + k_shift, 0, 0)),
            pl.BlockSpec((1, 1, k), lambda i, j: (bidx(i) * N_MOD + k_scale, 0, 0)),
            pl.BlockSpec((k, tn), lambda i, j: (0, j)),
        ],
        out_specs=pl.BlockSpec((tm, tn), lambda i, j: (i, j)),
        scratch_shapes=[pltpu.VMEM((tm, k), BF16)],
        compiler_params=_params("parallel", "arbitrary"),
        name="norm_mod_matmul",
    )(x, g, mod, mod, w)


def _conv_kernel(cur_ref, prev_ref, next_ref, w_ref, b_ref, o_ref, buf_ref, *, tl):
    i = pl.program_id(1)
    halo = SUBLANE
    buf_ref[0:halo] = jnp.where(i > 0, prev_ref[0], 0.0)
    buf_ref[halo:halo + tl] = cur_ref[0]
    buf_ref[halo + tl:2 * halo + tl] = jnp.where(i < pl.num_programs(1) - 1, next_ref[0], 0.0)
    acc = jnp.broadcast_to(b_ref[...], (tl, b_ref.shape[1]))
    for k in range(SSD_CONV):
        start = halo + k - SSD_CONV // 2
        acc = acc + w_ref[k:k + 1, :] * buf_ref[start:start + tl, :]
    o_ref[0] = _silu(acc)


def _conv_silu(p3, conv_w8, conv_b, tl):
    b, l, _ = p3.shape
    c = SSD_CONV_DIM
    tl = min(tl, l)
    cb = HYB_XBC0 // c
    nh = l // SUBLANE
    r = tl // SUBLANE
    return pl.pallas_call(
        functools.partial(_conv_kernel, tl=tl),
        out_shape=jax.ShapeDtypeStruct((b, l, c), F32),
        grid=(b, l // tl),
        in_specs=[
            pl.BlockSpec((1, tl, c), lambda bb, i: (bb, i, cb)),
            pl.BlockSpec((1, SUBLANE, c), lambda bb, i: (bb, jnp.maximum(i * r - 1, 0), cb)),
            pl.BlockSpec((1, SUBLANE, c), lambda bb, i: (bb, jnp.minimum((i + 1) * r, nh - 1), cb)),
            pl.BlockSpec((SUBLANE, c), lambda bb, i: (0, 0)),
            pl.BlockSpec((1, c), lambda bb, i: (0, 0)),
        ],
        out_specs=pl.BlockSpec((1, tl, c), lambda bb, i: (bb, i, 0)),
        scratch_shapes=[pltpu.VMEM((tl + 2 * SUBLANE, c), F32)],
        compiler_params=_params("parallel", "parallel"),
        name="dwconv_silu",
    )(p3, p3, p3, conv_w8, conv_b)


def _ssd_kernel(xs_ref, bm_ref, cm_ref, misc_ref, h0_ref, tri_ref, e_ref, bias_ref, a_ref, *rest,
                reverse, finalize):
    if finalize:
        z_ref, y0_ref, dskip_ref, ng_ref, y_ref, hT_ref, st_ref, yb_ref = rest
    else:
        y_ref, hT_ref, st_ref, yb_ref = rest
    q = SSD_CHUNK
    gw = SSD_INNER // SSD_GROUPS
    hpg = SSD_HEADS // SSD_GROUPS

    @pl.when(pl.program_id(1) == 0)
    def _():
        st_ref[...] = h0_ref[0]

    xs = xs_ref[0]
    tri = tri_ref[...]
    visible = tri > 0.5
    lane = lax.broadcasted_iota(jnp.int32, (q, LANE), 1)
    dt_lanes = (lane >= DT_LANE0) & (lane < DT_LANE0 + SSD_HEADS)
    dt = jnp.where(dt_lanes, jax.nn.softplus(misc_ref[0] + bias_ref[...]), 0.0)
    da = dt * a_ref[...]
    cum = _split_dot(da, tri.astype(BF16), 3, left=True)
    cum_t = cum.T
    dt_t = dt.T
    edge = cum[0:1, :] if reverse else cum[q - 1:q, :]
    expcum = jnp.exp(cum)
    w_end = jnp.exp(edge - cum) * dt
    e = e_ref[...]
    expcum_x = _split_dot(expcum, e, 2)
    w_end_x = _split_dot(w_end, e, 2)
    xs_b = xs.astype(BF16)
    xw_b = (xs * w_end_x).astype(BF16)
    st = st_ref[...]
    st_b = st.astype(BF16)
    first_head = lane < SSD_HEAD_DIM
    decay_row = expcum_x[0:1, :] if reverse else expcum_x[q - 1:q, :]
    for g in range(SSD_GROUPS):
        bm = bm_ref[0][:, g * SSD_STATE:(g + 1) * SSD_STATE]
        cm_b = cm_ref[0][:, g * SSD_STATE:(g + 1) * SSD_STATE].astype(BF16)
        cb = _nt_dot(cm_b, bm.astype(BF16))
        gs = slice(g * gw, (g + 1) * gw)
        y_off = jnp.dot(cm_b, st_b[:, gs], preferred_element_type=F32) * expcum_x[:, gs]
        for jp in range(hpg // 2):
            c0 = (g * hpg + 2 * jp) * SSD_HEAD_DIM
            tiles = []
            for sub in range(2):
                col = DT_LANE0 + g * hpg + 2 * jp + sub
                seg = cum[:, col:col + 1] - cum_t[col:col + 1, :]
                lmat = jnp.exp(jnp.where(visible, seg, -jnp.inf))
                mm = (cb * lmat * dt_t[col:col + 1, :]).astype(BF16)
                tiles.append(jnp.dot(mm, xs_b[:, c0:c0 + LANE], preferred_element_type=F32))
            yb_ref[:, c0:c0 + LANE] = jnp.where(first_head, tiles[0], tiles[1]) + y_off[:, c0 - g * gw:c0 - g * gw + LANE]
        new = jnp.dot(bm.T.astype(BF16), xw_b[:, gs], preferred_element_type=F32)
        st_ref[:, gs] = st[:, gs] * decay_row[:, gs] + new

    if finalize:
        v = (yb_ref[...] + y0_ref[0] + dskip_ref[...] * xs) * _silu(z_ref[0])
        for g in range(SSD_GROUPS):
            gs = slice(g * gw, (g + 1) * gw)
            y_ref[0, :, gs] = _rms(v[:, gs], ng_ref[:, gs]).astype(y_ref.dtype)
    else:
        y_ref[0] = yb_ref[...]

    @pl.when(pl.program_id(1) == pl.num_programs(1) - 1)
    def _():
        hT_ref[0] = st_ref[...]


def _ssd_scan(xbc3, p3, h0, tri, expand, bias_row, a_row, reverse, fin=None):
    b, l, _ = xbc3.shape
    nc = l // SSD_CHUNK
    cidx = (lambda s: nc - 1 - s) if reverse else (lambda s: s)
    q = SSD_CHUNK
    in_specs = [
        pl.BlockSpec((1, q, SSD_INNER), lambda bb, s: (bb, cidx(s), 0)),
        pl.BlockSpec((1, q, SSD_BC), lambda bb, s: (bb, cidx(s), SSD_INNER // SSD_BC)),
        pl.BlockSpec((1, q, SSD_BC), lambda bb, s: (bb, cidx(s), SSD_INNER // SSD_BC + 1)),
        pl.BlockSpec((1, q, LANE), lambda bb, s: (bb, cidx(s), HYB_MISC0 // LANE)),
        pl.BlockSpec((1, SSD_STATE, SSD_INNER), lambda bb, s: (bb, 0, 0)),
        pl.BlockSpec((q, q), lambda bb, s: (0, 0)),
        pl.BlockSpec((LANE, SSD_INNER), lambda bb, s: (0, 0)),
        pl.BlockSpec((1, LANE), lambda bb, s: (0, 0)),
        pl.BlockSpec((1, LANE), lambda bb, s: (0, 0)),
    ]
    args = [xbc3, xbc3, xbc3, p3, h0, tri, expand, bias_row, a_row]
    if fin is not None:
        y0, dskip_row, normg_row = fin
        in_specs += [
            pl.BlockSpec((1, q, SSD_INNER), lambda bb, s: (bb, cidx(s), HYB_Z0 // SSD_INNER)),
            pl.BlockSpec((1, q, SSD_INNER), lambda bb, s: (bb, cidx(s), 0)),
            pl.BlockSpec((1, SSD_INNER), lambda bb, s: (0, 0)),
            pl.BlockSpec((1, SSD_INNER), lambda bb, s: (0, 0)),
        ]
        args += [p3, y0, dskip_row, normg_row]
    return pl.pallas_call(
        functools.partial(_ssd_kernel, reverse=reverse, finalize=fin is not None),
        out_shape=(jax.ShapeDtypeStruct((b, l, SSD_INNER), BF16 if fin is not None else F32),
                   jax.ShapeDtypeStruct((b, SSD_STATE, SSD_INNER), F32)),
        grid=(b, nc),
        in_specs=in_specs,
        out_specs=(pl.BlockSpec((1, q, SSD_INNER), lambda bb, s: (bb, cidx(s), 0)),
                   pl.BlockSpec((1, SSD_STATE, SSD_INNER), lambda bb, s: (bb, 0, 0))),
        scratch_shapes=[pltpu.VMEM((SSD_STATE, SSD_INNER), F32), pltpu.VMEM((q, SSD_INNER), F32)],
        compiler_params=_params("parallel", "arbitrary"),
        name="ssd_scan_bwd" if reverse else "ssd_scan_fwd",
    )(*args)


def _qproj_kernel(x_ref, g_ref, w_ref, cos_ref, sin_ref, o_ref):
    h = _rms(x_ref[...], g_ref[...]).astype(BF16)
    r = jnp.dot(h, w_ref[...], preferred_element_type=F32)
    main = MLA_HEADS * MLA_QK_PAD
    cos, sin = cos_ref[...], sin_ref[...]
    for hd in range(MLA_HEADS):
        c0 = hd * MLA_QK_PAD
        o_ref[:, c0:c0 + LANE] = r[:, c0:c0 + LANE].astype(o_ref.dtype)
        swapped = r[:, main + hd * LANE:main + (hd + 1) * LANE]
        o_ref[:, c0 + LANE:c0 + 2 * LANE] = (r[:, c0 + LANE:c0 + 2 * LANE] * cos + swapped * sin).astype(o_ref.dtype)


def _qproj(p, g, w, cos, sin, rows_per_batch, tm):
    m = p.shape[0]
    tm = min(tm, m, rows_per_batch)
    nb = rows_per_batch // tm
    n_out = MLA_HEADS * MLA_QK_PAD
    return pl.pallas_call(
        _qproj_kernel,
        out_shape=jax.ShapeDtypeStruct((m, n_out), BF16),
        grid=(m // tm,),
        in_specs=[
            pl.BlockSpec((tm, MLA_Q_RANK), lambda i: (i, HYB_CQ0 // MLA_Q_RANK)),
            pl.BlockSpec((1, MLA_Q_RANK), lambda i: (0, 0)),
            pl.BlockSpec(w.shape, lambda i: (0, 0)),
            pl.BlockSpec((tm, LANE), lambda i: (i % nb, 0)),
            pl.BlockSpec((tm, LANE), lambda i: (i % nb, 0)),
        ],
        out_specs=pl.BlockSpec((tm, n_out), lambda i: (i, 0)),
        compiler_params=_params("parallel"),
        name="mla_q_proj",
    )(p, g, w, cos, sin)


def _kvproj_kernel(x_ref, misc_ref, misc2_ref, g_ref, w_ref, cos_ref, sin_ref, k_ref, v_ref):
    h = _rms(x_ref[...], g_ref[...]).astype(BF16)
    r = jnp.dot(h, w_ref[...], preferred_element_type=F32)
    kr = (misc_ref[...] * cos_ref[...] + misc2_ref[...] * sin_ref[...]).astype(k_ref.dtype)
    for hd in range(MLA_HEADS):
        c0 = hd * MLA_QK_PAD
        k_ref[:, c0:c0 + LANE] = r[:, hd * MLA_NOPE:(hd + 1) * MLA_NOPE].astype(k_ref.dtype)
        k_ref[:, c0 + LANE:c0 + 2 * LANE] = kr
    v_ref[...] = r[:, MLA_HEADS * MLA_NOPE:].astype(v_ref.dtype)


def _kvproj(p, g, w, cos, sin, rows_per_batch, tm):
    m = p.shape[0]
    tm = min(tm, m, rows_per_batch)
    nb = rows_per_batch // tm
    nk, nv = MLA_HEADS * MLA_QK_PAD, MLA_HEADS * MLA_V
    return pl.pallas_call(
        _kvproj_kernel,
        out_shape=(jax.ShapeDtypeStruct((m, nk), BF16), jax.ShapeDtypeStruct((m, nv), BF16)),
        grid=(m // tm,),
        in_specs=[
            pl.BlockSpec((tm, MLA_KV_RANK), lambda i: (i, HYB_CKV0 // MLA_KV_RANK)),
            pl.BlockSpec((tm, LANE), lambda i: (i, HYB_MISC0 // LANE)),
            pl.BlockSpec((tm, LANE), lambda i: (i, HYB_MISC2 // LANE)),
            pl.BlockSpec((1, MLA_KV_RANK), lambda i: (0, 0)),
            pl.BlockSpec(w.shape, lambda i: (0, 0)),
            pl.BlockSpec((tm, LANE), lambda i: (i % nb, 0)),
            pl.BlockSpec((tm, LANE), lambda i: (i % nb, 0)),
        ],
        out_specs=(pl.BlockSpec((tm, nk), lambda i: (i, 0)), pl.BlockSpec((tm, nv), lambda i: (i, 0))),
        compiler_params=_params("parallel"),
        name="mla_kv_proj",
    )(p, p, p, g, w, cos, sin)


def _attn_kernel(q_ref, kc_ref, vc_ref, *rest, with_latent):
    if with_latent:
        k_ref, v_ref, o_ref, m_ref, l_ref, acc_ref = rest
    else:
        o_ref, m_ref, l_ref, acc_ref = rest
    ki = pl.program_id(2)

    def update(hd, kblk, vblk):
        s = _nt_dot(q_ref[:, hd * MLA_QK_PAD:(hd + 1) * MLA_QK_PAD], kblk)
        tiles = s.shape[1] // LANE
        m_prev = m_ref[hd]
        m_new = jnp.maximum(m_prev, jnp.max(s, axis=-1, keepdims=True))
        alpha = jnp.exp2((m_prev - m_new) * EXP2_SCALE)
        p = jnp.exp2((s - jnp.tile(m_new, (1, tiles))) * EXP2_SCALE)
        psum = p[:, 0:LANE]
        for c in range(1, tiles):
            psum = psum + p[:, c * LANE:(c + 1) * LANE]
        l_ref[hd] = alpha * l_ref[hd] + psum
        vs = slice(hd * MLA_V, (hd + 1) * MLA_V)
        acc_ref[:, vs] = alpha * acc_ref[:, vs] + jnp.dot(p.astype(BF16), vblk, preferred_element_type=F32)
        m_ref[hd] = m_new

    @pl.when(ki == 0)
    def _():
        m_ref[...] = jnp.full(m_ref.shape, -jnp.inf, F32)
        l_ref[...] = jnp.zeros(l_ref.shape, F32)
        acc_ref[...] = jnp.zeros(acc_ref.shape, F32)
        for hd in range(MLA_HEADS):
            update(hd, kc_ref[:, hd * MLA_QK_PAD:(hd + 1) * MLA_QK_PAD], vc_ref[:, hd * MLA_V:(hd + 1) * MLA_V])

    if with_latent:
        for hd in range(MLA_HEADS):
            update(hd, k_ref[:, hd * MLA_QK_PAD:(hd + 1) * MLA_QK_PAD], v_ref[:, hd * MLA_V:(hd + 1) * MLA_V])

    @pl.when(ki == pl.num_programs(2) - 1)
    def _():
        for hd in range(MLA_HEADS):
            vs = slice(hd * MLA_V, (hd + 1) * MLA_V)
            o_ref[:, vs] = (acc_ref[:, vs] / jnp.sum(l_ref[hd], axis=-1, keepdims=True)).astype(o_ref.dtype)


def _attention(q, kc, vc, k, v, batch, tq, tk):
    m = q.shape[0]
    lq = m // batch
    lc = kc.shape[0] // batch
    tq = min(tq, lq)
    nq = lq // tq
    nq_pad, nv = MLA_HEADS * MLA_QK_PAD, MLA_HEADS * MLA_V
    in_specs = [
        pl.BlockSpec((tq, nq_pad), lambda b, i, j: (b * nq + i, 0)),
        pl.BlockSpec((lc, nq_pad), lambda b, i, j: (b, 0)),
        pl.BlockSpec((lc, nv), lambda b, i, j: (b, 0)),
    ]
    args = [q, kc, vc]
    nk = 1
    if k is not None:
        lk = k.shape[0] // batch
        tk = min(tk, lk)
        nk = lk // tk
        in_specs += [
            pl.BlockSpec((tk, nq_pad), lambda b, i, j: (b * nk + j, 0)),
            pl.BlockSpec((tk, nv), lambda b, i, j: (b * nk + j, 0)),
        ]
        args += [k, v]
    return pl.pallas_call(
        functools.partial(_attn_kernel, with_latent=k is not None),
        out_shape=jax.ShapeDtypeStruct((m, nv), BF16),
        grid=(batch, nq, nk),
        in_specs=in_specs,
        out_specs=pl.BlockSpec((tq, nv), lambda b, i, j: (b * nq + i, 0)),
        scratch_shapes=[pltpu.VMEM((MLA_HEADS, tq, LANE), F32), pltpu.VMEM((MLA_HEADS, tq, LANE), F32),
                        pltpu.VMEM((tq, nv), F32)],
        compiler_params=_params("parallel", "parallel", "arbitrary"),
        name="mla_attention",
    )(*args)


def _outproj_kernel(a1_ref, a2_ref, w1_ref, w2_ref, x_ref, gate_ref, o_ref):
    acc = jnp.dot(a1_ref[...], w1_ref[...], preferred_element_type=F32)
    acc = acc + jnp.dot(a2_ref[...], w2_ref[...], preferred_element_type=F32)
    o_ref[...] = x_ref[...] + gate_ref[0] * acc


def _outproj(a1, a2, w1, w2, x, mod, k_gate, rows_per_batch, tm):
    m, d = x.shape
    tm = min(tm, m)
    bidx = lambda i: (i * tm) // rows_per_batch
    return pl.pallas_call(
        _outproj_kernel,
        out_shape=jax.ShapeDtypeStruct((m, d), F32),
        grid=(m // tm,),
        in_specs=[
            pl.BlockSpec((tm, a1.shape[1]), lambda i: (i, 0)),
            pl.BlockSpec((tm, a2.shape[1]), lambda i: (i, 0)),
            pl.BlockSpec(w1.shape, lambda i: (0, 0)),
            pl.BlockSpec(w2.shape, lambda i: (0, 0)),
            pl.BlockSpec((tm, d), lambda i: (i, 0)),
            pl.BlockSpec((1, 1, d), lambda i: (bidx(i) * N_MOD + k_gate, 0, 0)),
        ],
        out_specs=pl.BlockSpec((tm, d), lambda i: (i, 0)),
        compiler_params=_params("parallel"),
        name="hyb_out_proj",
    )(a1, a2, w1, w2, x, mod)


def _gmlp_kernel(x_ref, g_ref, sh_ref, sc_ref, gate_ref, win_ref, lng_ref, lnb_ref, ws_ref, bs_ref, wout_ref,
                 o_ref, uv_ref):
    x = x_ref[...]
    tm = x.shape[0]
    h = (_rms(x, g_ref[...]) * (1.0 + sc_ref[0]) + sh_ref[0]).astype(BF16)
    uv = _gelu(jnp.dot(h, win_ref[...], preferred_element_type=F32))
    u, v = uv[:, :GM_INNER], uv[:, GM_INNER:]
    mu = jnp.mean(v, axis=-1, keepdims=True)
    var = jnp.mean(jnp.square(v - mu), axis=-1, keepdims=True)
    vn = ((v - mu) * lax.rsqrt(var + EPS) * lng_ref[...] + lnb_ref[...]).astype(BF16)
    gw = GM_INNER // GM_GROUPS
    for c in range(tm // GM_CHUNK):
        rs = slice(c * GM_CHUNK, (c + 1) * GM_CHUNK)
        for g in range(GM_GROUPS):
            cs = slice(g * gw, (g + 1) * gw)
            sv = jnp.dot(ws_ref[g], vn[rs, cs], preferred_element_type=F32) + bs_ref[:, cs]
            uv_ref[rs, cs] = (u[rs, cs] * sv).astype(BF16)
    o_ref[...] = x + gate_ref[0] * jnp.dot(uv_ref[...], wout_ref[...], preferred_element_type=F32)


def _gmlp(x, g, mod, rows_per_batch, w_in, ln_g, ln_b, ws, bs_full, w_out, tm):
    m, d = x.shape
    tm = min(tm, m)
    bidx = lambda i: (i * tm) // rows_per_batch
    const = lambda shape: pl.BlockSpec(shape, lambda i: (0,) * len(shape))
    modspec = lambda k: pl.BlockSpec((1, 1, d), lambda i: (bidx(i) * N_MOD + k, 0, 0))
    return pl.pallas_call(
        _gmlp_kernel,
        out_shape=jax.ShapeDtypeStruct((m, d), F32),
        grid=(m // tm,),
        in_specs=[
            pl.BlockSpec((tm, d), lambda i: (i, 0)), const((1, d)), modspec(0), modspec(1), modspec(2),
            const(w_in.shape), const((1, GM_INNER)), const((1, GM_INNER)), const(ws.shape), const(bs_full.shape),
            const(w_out.shape),
        ],
        out_specs=pl.BlockSpec((tm, d), lambda i: (i, 0)),
        scratch_shapes=[pltpu.VMEM((tm, GM_INNER), BF16)],
        compiler_params=_params("parallel"),
        name="chunk_gmlp",
    )(x, g, mod, mod, mod, w_in, ln_g, ln_b, ws, bs_full, w_out)


def _top_values(xs, k, want_rank):
    xs = list(xs)
    vals = [[] for _ in xs]
    ranks = [jnp.full(x.shape, float(k), F32) if w else None for x, w in zip(xs, want_rank)]
    for r in range(k):
        for i, x in enumerate(xs):
            mx = jnp.max(x, axis=0, keepdims=True)
            vals[i].append(mx)
            hit = x == mx
            if want_rank[i]:
                ranks[i] = jnp.where(hit, float(r), ranks[i])
            if r + 1 < k:
                xs[i] = jnp.where(hit, -jnp.inf, x)
    return vals, ranks


def _peer_stats_kernel(x_ref, g_ref, sh_ref, sc_ref, wq_ref, k1_ref, k2_ref, t_ref, a_ref, n_ref, b_ref, r2_ref,
                       q_ref, cand_ref):
    h = (_rms(x_ref[...], g_ref[...]) * (1.0 + sc_ref[0]) + sh_ref[0]).astype(BF16)
    t_ref[...] = h
    q_ref[...] = jnp.dot(h, wq_ref[...], preferred_element_type=F32).astype(BF16)
    half = PEER_DKEY // 2
    tm = h.shape[0]
    n_pairs = len(PEER_PAIRS)
    cand_ref[n_pairs:, :] = jnp.full((cand_ref.shape[0] - n_pairs, tm), -jnp.inf, F32)

    def head(hd, carry):
        c0 = pl.multiple_of(hd * PEER_DKEY, PEER_DKEY)
        s1 = _nt_dot(k1_ref[hd], q_ref[:, pl.ds(c0, half)])
        s2 = _nt_dot(k2_ref[hd], q_ref[:, pl.ds(c0 + half, half)])
        (v1, v2), (_, rank2) = _top_values((s1, s2), PEER_TOPK, (False, True))
        for idx, (a, b) in enumerate(PEER_PAIRS):
            cand_ref[idx:idx + 1, :] = v1[a] + v2[b]
        (tops,), _ = _top_values((cand_ref[...],), PEER_TOPK, (False,))
        thr = tops[PEER_TOPK - 1]
        z = tops[0] * 0.0
        for tv in tops:
            z = z + jnp.exp(tv - tops[0])
        n = jnp.zeros(s1.shape, F32)
        for vb in v2:
            n = n + jnp.where(s1 + vb >= thr, 1.0, 0.0)
        a_ref[hd] = jnp.exp(s1 - v1[0]) * (0.5 / z)
        n_ref[hd] = n
        bvals = jnp.exp(s2 - v2[0])
        for c in range(tm // LANE):
            b_ref[hd, c] = bvals[:, c * LANE:(c + 1) * LANE]
            r2_ref[hd, c] = rank2[:, c * LANE:(c + 1) * LANE]
        return carry

    lax.fori_loop(0, PEER_HEADS, head, 0)


def _peer_stats(x, g, mod, rows_per_batch, wq, k1, k2, tm):
    m, d = x.shape
    tm = min(tm, m)
    bidx = lambda i: (i * tm) // rows_per_batch
    const = lambda shape: pl.BlockSpec(shape, lambda i: (0,) * len(shape))
    modspec = lambda k: pl.BlockSpec((1, 1, d), lambda i: (bidx(i) * N_MOD + k, 0, 0))
    return pl.pallas_call(
        _peer_stats_kernel,
        out_shape=(jax.ShapeDtypeStruct((m, d), BF16),
                   jax.ShapeDtypeStruct((PEER_HEADS, PEER_NKEYS, m), F32),
                   jax.ShapeDtypeStruct((PEER_HEADS, PEER_NKEYS, m), F32),
                   jax.ShapeDtypeStruct((PEER_HEADS, m // LANE, PEER_NKEYS, LANE), F32),
                   jax.ShapeDtypeStruct((PEER_HEADS, m // LANE, PEER_NKEYS, LANE), F32)),
        grid=(m // tm,),
        in_specs=[pl.BlockSpec((tm, d), lambda i: (i, 0)), const((1, d)), modspec(3), modspec(4),
                  const(wq.shape), const(k1.shape), const(k2.shape)],
        out_specs=(pl.BlockSpec((tm, d), lambda i: (i, 0)),) + tuple(
            pl.BlockSpec((PEER_HEADS, PEER_NKEYS, tm), lambda i: (0, 0, i)) for _ in range(2)) + tuple(
            pl.BlockSpec((PEER_HEADS, tm // LANE, PEER_NKEYS, LANE), lambda i: (0, i, 0, 0)) for _ in range(2)),
        scratch_shapes=[pltpu.VMEM((tm, PEER_HEADS * PEER_DKEY), BF16),
                        pltpu.VMEM((-(-len(PEER_PAIRS) // SUBLANE) * SUBLANE, tm), F32)],
        compiler_params=_params("parallel"),
        name="peer_topk",
    )(x, g, mod, mod, wq, k1, k2)


def _peer_expert_kernel(t_ref, u_ref, vt_ref, a_ref, n_ref, b_ref, r2_ref, x_ref, gate_ref, o_ref,
                        acc_ref, s_ref, g_ref, *, te):
    j = pl.program_id(1)
    d, tm = acc_ref.shape
    rows = te // PEER_NKEYS
    sub = PEER_NKEYS // 4

    @pl.when(j == 0)
    def _():
        acc_ref[...] = jnp.zeros(acc_ref.shape, F32)

    s_ref[...] = _nt_dot(u_ref[...], t_ref[...])
    for c in range(tm // LANE):
        cs = slice(c * LANE, (c + 1) * LANE)
        for k0 in range(0, PEER_NKEYS, sub):
            ks = slice(k0, k0 + sub)
            w = [None] * rows
            for hd in range(PEER_HEADS):
                bv, rk = b_ref[hd, c, ks, :], r2_ref[hd, c, ks, :]
                for r in range(rows):
                    term = a_ref[hd, r:r + 1, cs] * jnp.where(rk < n_ref[hd, r:r + 1, cs], bv, 0.0)
                    w[r] = term if w[r] is None else w[r] + term
            for r in range(rows):
                es = slice(r * PEER_NKEYS + k0, r * PEER_NKEYS + k0 + sub)
                sc = s_ref[es, cs]
                g_ref[es, cs] = (sc * (1.0 + lax.erf(sc * (2.0 ** -0.5))) * w[r]).astype(BF16)
    acc_ref[...] += jnp.dot(vt_ref[...], g_ref[...], preferred_element_type=F32)

    @pl.when(j == pl.num_programs(1) - 1)
    def _():
        o_ref[...] = x_ref[...] + gate_ref[0] * acc_ref[...].T


def _peer_experts(t, u, vt, a, cnt, b, r2, x, mod, k_gate, rows_per_batch, tm, te):
    m, d = x.shape
    tm = min(tm, m)
    ne = u.shape[0]
    bidx = lambda i: (i * tm) // rows_per_batch
    rows = te // PEER_NKEYS
    return pl.pallas_call(
        functools.partial(_peer_expert_kernel, te=te),
        out_shape=jax.ShapeDtypeStruct((m, d), F32),
        grid=(m // tm, ne // te),
        in_specs=[
            pl.BlockSpec((tm, d), lambda i, j: (i, 0)),
            pl.BlockSpec((te, d), lambda i, j: (j, 0)),
            pl.BlockSpec((d, te), lambda i, j: (0, j)),
            pl.BlockSpec((PEER_HEADS, rows, tm), lambda i, j: (0, j, i)),
            pl.BlockSpec((PEER_HEADS, rows, tm), lambda i, j: (0, j, i)),
            pl.BlockSpec((PEER_HEADS, tm // LANE, PEER_NKEYS, LANE), lambda i, j: (0, i, 0, 0)),
            pl.BlockSpec((PEER_HEADS, tm // LANE, PEER_NKEYS, LANE), lambda i, j: (0, i, 0, 0)),
            pl.BlockSpec((tm, d), lambda i, j: (i, 0)),
            pl.BlockSpec((1, 1, d), lambda i, j: (bidx(i) * N_MOD + k_gate, 0, 0)),
        ],
        out_specs=pl.BlockSpec((tm, d), lambda i, j: (i, 0)),
        scratch_shapes=[pltpu.VMEM((d, tm), F32), pltpu.VMEM((te, tm), F32), pltpu.VMEM((te, tm), BF16)],
        compiler_params=_params("parallel", "arbitrary"),
        name="peer_experts",
    )(t, u, vt, a, cnt, b, r2, x, mod)


def _final_norm_kernel(x_ref, g_ref, o_ref):
    o_ref[...] = _rms(x_ref[...], g_ref[...])


def _final_norm(x, g, tm):
    m, d = x.shape
    tm = min(tm, m)
    return pl.pallas_call(
        _final_norm_kernel,
        out_shape=jax.ShapeDtypeStruct((m, d), F32),
        grid=(m // tm,),
        in_specs=[pl.BlockSpec((tm, d), lambda i: (i, 0)), pl.BlockSpec((1, d), lambda i: (0, 0))],
        out_specs=pl.BlockSpec((tm, d), lambda i: (i, 0)),
        compiler_params=_params("parallel"),
        name="final_rmsnorm",
    )(x, g)


def _rope_tables(n_lat, n_ctx):
    rows = n_lat // GRID_W
    row = jnp.broadcast_to(jnp.arange(rows, dtype=F32)[:, None], (rows, GRID_W)).reshape(-1)
    col = jnp.broadcast_to(jnp.arange(GRID_W, dtype=F32)[None, :], (rows, GRID_W)).reshape(-1)
    n_freq = MLA_ROPE // 4
    inv = ROPE_BASE ** (-jnp.arange(n_freq, dtype=F32) / n_freq)
    ang = jnp.concatenate([row[:, None] * inv, col[:, None] * inv], axis=-1)
    cos, sin = jnp.cos(ang), jnp.sin(ang)
    pad = jnp.zeros((n_lat, LANE - MLA_ROPE), F32)
    cos_l = jnp.concatenate([cos, cos, pad], axis=-1)
    sin_l = jnp.concatenate([-sin, sin, pad], axis=-1)
    cos_c = jnp.concatenate([jnp.ones((n_ctx, MLA_ROPE), F32), jnp.zeros((n_ctx, LANE - MLA_ROPE), F32)], axis=-1)
    return (cos_l, sin_l), (cos_c, jnp.zeros((n_ctx, LANE), F32))


def _swap_halves(w):
    half = w.shape[-1] // 2
    return jnp.concatenate([w[..., half:], w[..., :half]], axis=-1)


def _prep_hybrid(w_in, conv_w, conv_b, a_log, dt_bias, d_skip, norm_g, q_norm_g, w_qb, kv_norm_g, w_kvb, w_out):
    d = w_in.shape[0]
    o = 0
    pieces = {}
    for name, width in (("z", SSD_INNER), ("xbc", SSD_CONV_DIM), ("dt", SSD_HEADS), ("cq", MLA_Q_RANK),
                        ("ckv", MLA_KV_RANK), ("kr", MLA_ROPE)):
        pieces[name] = w_in[:, o:o + width]
        o += width
    zeros = lambda n: jnp.zeros((d, n), w_in.dtype)
    w_hyb = jnp.concatenate([
        pieces["z"], pieces["kr"], pieces["dt"], zeros(LANE - MLA_ROPE - SSD_HEADS), pieces["cq"], pieces["xbc"],
        pieces["ckv"], _swap_halves(pieces["kr"]), zeros(LANE - MLA_ROPE)], axis=1).astype(BF16)
    conv_w8 = jnp.concatenate([conv_w.T, jnp.zeros((SUBLANE - SSD_CONV, SSD_CONV_DIM), F32)], axis=0)
    lane_row = lambda v: jnp.zeros((1, LANE), F32).at[0, DT_LANE0:DT_LANE0 + SSD_HEADS].set(v)
    dirs = [(lane_row(dt_bias[k]), lane_row(-jnp.exp(a_log[k]))) for k in range(2)]
    expand = jnp.zeros((LANE, SSD_INNER), F32).at[DT_LANE0:DT_LANE0 + SSD_HEADS].set(
        jnp.repeat(jnp.eye(SSD_HEADS, dtype=F32), SSD_HEAD_DIM, axis=1)).astype(BF16)
    wq3 = w_qb.reshape(MLA_Q_RANK, MLA_HEADS, MLA_NOPE + MLA_ROPE)
    zq = jnp.zeros((MLA_Q_RANK, MLA_HEADS, LANE - MLA_ROPE), w_qb.dtype)
    q_main = jnp.concatenate([wq3, zq], axis=-1).reshape(MLA_Q_RANK, MLA_HEADS * MLA_QK_PAD)
    q_swap = jnp.concatenate([_swap_halves(wq3[..., MLA_NOPE:]), zq], axis=-1).reshape(MLA_Q_RANK, MLA_HEADS * LANE)
    wq = jnp.concatenate([q_main, q_swap], axis=1).astype(BF16)
    wkv3 = w_kvb.reshape(MLA_KV_RANK, MLA_HEADS, MLA_NOPE + MLA_V)
    wkv = jnp.concatenate([wkv3[..., :MLA_NOPE].reshape(MLA_KV_RANK, -1), wkv3[..., MLA_NOPE:].reshape(MLA_KV_RANK, -1)],
                          axis=1).astype(BF16)
    return dict(
        w_hyb=w_hyb, conv_w8=conv_w8, conv_b=conv_b[None, :], dirs=dirs, expand=expand,
        dskip=jnp.repeat(d_skip, SSD_HEAD_DIM)[None, :], normg=norm_g[None, :],
        q_norm_g=q_norm_g[None, :], wq=wq, kv_norm_g=kv_norm_g[None, :], wkv=wkv,
        w_out1=w_out[:SSD_INNER].astype(BF16), w_out2=w_out[SSD_INNER:].astype(BF16))


def _hybrid_layer(xl, xc, batch, g1, mod_l, mod_c, hp, ropes, update_ctx):
    n_lat, n_ctx = xl.shape[0] // batch, xc.shape[0] // batch
    (cos_l, sin_l), (cos_c, sin_c) = ropes
    q = SSD_CHUNK
    tri_f = jnp.tril(jnp.ones((q, q), F32))
    tri_b = jnp.triu(jnp.ones((q, q), F32))
    pl_ = _nmm(xl, g1, mod_l, 0, 1, n_lat, hp["w_hyb"], 512, HYB_COLS // 3)
    pc_ = _nmm(xc, g1, mod_c, 0, 1, xc.shape[0], hp["w_hyb"], 256, HYB_COLS // 3)
    pl3, pc3 = pl_.reshape(batch, n_lat, HYB_COLS), pc_.reshape(batch, n_ctx, HYB_COLS)
    xbc_l = _conv_silu(pl3, hp["conv_w8"], hp["conv_b"], 512)
    xbc_c = _conv_silu(pc3, hp["conv_w8"], hp["conv_b"], 512)
    h0 = jnp.zeros((batch, SSD_STATE, SSD_INNER), F32)
    (bias0, a0), (bias1, a1) = hp["dirs"]
    yc0, hc0 = _ssd_scan(xbc_c, pc3, h0, tri_f, hp["expand"], bias0, a0, False)
    yl0, _ = _ssd_scan(xbc_l, pl3, hc0, tri_f, hp["expand"], bias0, a0, False)
    fin = lambda y0: (y0, hp["dskip"], hp["normg"])
    sc, hc1 = _ssd_scan(xbc_c, pc3, h0, tri_b, hp["expand"], bias1, a1, True, fin(yc0))
    sl, _ = _ssd_scan(xbc_l, pl3, hc1, tri_b, hp["expand"], bias1, a1, True, fin(yl0))
    ql = _qproj(pl_, hp["q_norm_g"], hp["wq"], cos_l, sin_l, n_lat, 512)
    kl, vl = _kvproj(pl_, hp["kv_norm_g"], hp["wkv"], cos_l, sin_l, n_lat, 512)
    kc, vc = _kvproj(pc_, hp["kv_norm_g"], hp["wkv"], cos_c, sin_c, n_ctx, 256)
    att_l = _attention(ql, kc, vc, kl, vl, batch, 512, 1024)
    xl_new = _outproj(sl.reshape(-1, SSD_INNER), att_l, hp["w_out1"], hp["w_out2"], xl, mod_l, 2, n_lat, 512)
    xc_new = xc
    if update_ctx:
        qc = _qproj(pc_, hp["q_norm_g"], hp["wq"], cos_c, sin_c, n_ctx, 256)
        att_c = _attention(qc, kc, vc, None, None, batch, 256, 256)
        xc_new = _outproj(sc.reshape(-1, SSD_INNER), att_c, hp["w_out1"], hp["w_out2"], xc, mod_c, 2, xc.shape[0], 256)
    return xl_new, xc_new


def _peer_layer(x, g2, mod, rows_per_batch, pp):
    t, a, cnt, b, r2 = _peer_stats(x, g2, mod, rows_per_batch, pp["wq"], pp["k1"], pp["k2"], 256)
    return _peer_experts(t, pp["u"], pp["vt"], a, cnt, b, r2, x, mod, 5, rows_per_batch, 512, 1024)


def kernel(x, c, ctx, c_ctx, ada_w, ada_b, norm1_g, norm2_g, hyb_w_in, ssd_conv_w, ssd_conv_b, ssd_a_log,
           ssd_dt_bias, ssd_d, ssd_norm_g, mla_q_norm_g, mla_w_qb, mla_kv_norm_g, mla_w_kvb, hyb_w_out, gm_w_in,
           gm_ln_g, gm_ln_b, gm_ws, gm_bs, gm_w_out, peer_wq, peer_k1, peer_k2, peer_u, peer_v, final_norm_g):
    batch, n_lat, d = x.shape
    n_ctx = ctx.shape[1]
    depth = ada_w.shape[0]
    xl = x.reshape(batch * n_lat, d)
    xc = ctx.reshape(batch * n_ctx, d)
    n_cond = -(-(batch + 1) // SUBLANE) * SUBLANE
    conds = jnp.zeros((n_cond, d), F32).at[:batch].set(c).at[batch].set(c_ctx)
    mods = _mods(conds, ada_w, ada_b)
    ropes = _rope_tables(n_lat, n_ctx)
    for layer in range(depth):
        i = layer // 2
        even = layer % 2 == 0
        keep_ctx = any(j % 2 == 0 for j in range(layer + 1, depth))
        mod_l = mods[layer, :batch].reshape(batch * N_MOD, 1, d)
        mod_c = mods[layer, batch].reshape(N_MOD, 1, d)
        g1, g2 = norm1_g[layer][None, :], norm2_g[layer][None, :]
        if even:
            hp = _prep_hybrid(hyb_w_in[i], ssd_conv_w[i], ssd_conv_b[i], ssd_a_log[i], ssd_dt_bias[i], ssd_d[i],
                              ssd_norm_g[i], mla_q_norm_g[i], mla_w_qb[i], mla_kv_norm_g[i], mla_w_kvb[i], hyb_w_out[i])
            xl, xc = _hybrid_layer(xl, xc, batch, g1, mod_l, mod_c, hp, ropes, keep_ctx)
        else:
            gp = (gm_w_in[i].astype(BF16), gm_ln_g[i][None, :], gm_ln_b[i][None, :], gm_ws[i].astype(BF16),
                  jnp.repeat(gm_bs[i].T, GM_INNER // GM_GROUPS, axis=1), gm_w_out[i].astype(BF16))
            xl = _gmlp(xl, g1, mod_l, n_lat, *gp, 256)
            if keep_ctx:
                xc = _gmlp(xc, g1, mod_c, xc.shape[0], *gp, 256)
        pp = dict(wq=peer_wq[layer].astype(BF16), k1=peer_k1[layer].astype(BF16), k2=peer_k2[layer].astype(BF16),
                  u=peer_u[layer].astype(BF16), vt=peer_v[layer].T.astype(BF16))
        xl = _peer_layer(xl, g2, mod_l, n_lat, pp)
        if keep_ctx:
            xc = _peer_layer(xc, g2, mod_c, xc.shape[0], pp)
    return _final_norm(xl, final_norm_g[None, :], 512).reshape(batch, n_lat, d)
```

```python
import functools

import jax
import jax.numpy as jnp
from jax import lax
from jax.experimental import pallas as pl
from jax.experimental.pallas import tpu as pltpu

F32, BF16 = jnp.float32, jnp.bfloat16
EPS = 1e-6
LANE = 128
SUBLANE = 8
MXU_COLS = 256
VMEM_LIMIT = 56 * 1024 * 1024

D_MODEL = 1024
DEPTH = 4
GRID_W = 64
N_MOD = 6
SSD_HEADS = 16
SSD_HEAD_DIM = 64
SSD_INNER = SSD_HEADS * SSD_HEAD_DIM
SSD_GROUPS = 2
SSD_STATE = 128
SSD_BC = SSD_GROUPS * SSD_STATE
SSD_CONV = 5
SSD_CONV_DIM = SSD_INNER + 2 * SSD_BC
SSD_CHUNK = 128
MLA_HEADS = 8
MLA_Q_RANK = 384
MLA_KV_RANK = 256
MLA_NOPE = 128
MLA_ROPE = 64
MLA_V = 128
MLA_SCALE = (MLA_NOPE + MLA_ROPE) ** -0.5
EXP2_SCALE = MLA_SCALE * 1.4426950408889634
ROPE_BASE = 10000.0
GM_CHUNK = 128
GM_INNER = 2 * D_MODEL
GM_GROUPS = 8
PEER_HEADS = 8
PEER_NKEYS = 128
PEER_EXPERTS = PEER_NKEYS * PEER_NKEYS
PEER_DKEY = 256
PEER_TOPK = 16
PEER_TE = 1024
PEER_TM = 512

HYB_Z0 = 0
HYB_MISC0 = 1024
HYB_CQ0 = 1152
HYB_XBC0 = 1536
HYB_CKV0 = 3072
HYB_MISC2 = 3328
HYB_COLS = 3456
DT_LANE0 = MLA_ROPE
MLA_QK_PAD = 256

PEER_PAIRS = tuple((a, b) for a in range(PEER_TOPK) for b in range(PEER_TOPK) if (a + 1) * (b + 1) <= PEER_TOPK)


def _params(*sem, flags=None):
    return pltpu.CompilerParams(dimension_semantics=sem, vmem_limit_bytes=VMEM_LIMIT, flags=flags)


def _rms(x, g):
    return x * lax.rsqrt(jnp.mean(x * x, axis=-1, keepdims=True) + EPS) * g


def _silu(x):
    return x * jax.nn.sigmoid(x)


def _gelu(x):
    return 0.5 * x * (1.0 + lax.erf(x * (2.0 ** -0.5)))


def _nt_dot(a, b):
    return lax.dot_general(a, b, (((1,), (1,)), ((), ())), preferred_element_type=F32)


def _split_dot(x, w, terms, left=False):
    acc = None
    rem = x
    for _ in range(terms):
        hi = rem.astype(BF16)
        part = jnp.dot(w, hi, preferred_element_type=F32) if left else jnp.dot(hi, w, preferred_element_type=F32)
        acc = part if acc is None else acc + part
        rem = rem - hi.astype(F32)
    return acc


def _mods_kernel(c_ref, w_ref, b_ref, o_ref):
    h = _silu(c_ref[...]).astype(BF16)
    o_ref[0] = jnp.dot(h, w_ref[0].astype(BF16), preferred_element_type=F32) + b_ref[0]


def _mods(conds, ada_w, ada_b):
    depth, d, n = ada_w.shape
    tn = 1536
    return pl.pallas_call(
        _mods_kernel,
        out_shape=jax.ShapeDtypeStruct((depth, conds.shape[0], n), F32),
        grid=(depth, n // tn),
        in_specs=[
            pl.BlockSpec(conds.shape, lambda l, j: (0, 0)),
            pl.BlockSpec((1, d, tn), lambda l, j: (l, 0, j)),
            pl.BlockSpec((1, 1, tn), lambda l, j: (l, 0, j)),
        ],
        out_specs=pl.BlockSpec((1, conds.shape[0], tn), lambda l, j: (l, 0, j)),
        compiler_params=_params("parallel", "parallel"),
        name="ada_mods",
    )(conds, ada_w, ada_b.reshape(depth, 1, n))


def _nmm_kernel(x_ref, g_ref, sh_ref, sc_ref, w_ref, o_ref, h_ref):
    @pl.when(pl.program_id(1) == 0)
    def _():
        y = _rms(x_ref[...], g_ref[...])
        h_ref[...] = (y * (1.0 + sc_ref[0]) + sh_ref[0]).astype(BF16)

    o_ref[...] = jnp.dot(h_ref[...], w_ref[...], preferred_element_type=F32).astype(o_ref.dtype)


def _nmm(x, g, mod, k_shift, k_scale, rows_per_batch, w, tm, tn):
    m, k = x.shape
    n = w.shape[1]
    tm = min(tm, m)
    bidx = lambda i: (i * tm) // rows_per_batch
    return pl.pallas_call(
        _nmm_kernel,
        out_shape=jax.ShapeDtypeStruct((m, n), F32),
        grid=(m // tm, n // tn),
        in_specs=[
            pl.BlockSpec((tm, k), lambda i, j: (i, 0)),
            pl.BlockSpec((1, k), lambda i, j: (0, 0)),
            pl.BlockSpec((1, 1, k), lambda i, j: (bidx(i) * N_MOD + k_shift, 0, 0)),
            pl.BlockSpec((1, 1, k), lambda i, j: (bidx(i) * N_MOD + k_scale, 0, 0)),
            pl.BlockSpec((k, tn), lambda i, j: (0, j)),
        ],
        out_specs=pl.BlockSpec((tm, tn), lambda i, j: (i, j)),
        scratch_shapes=[pltpu.VMEM((tm, k), BF16)],
        compiler_params=_params("parallel", "arbitrary"),
        name="norm_mod_matmul",
    )(x, g, mod, mod, w)


def _conv_kernel(cur_ref, prev_ref, next_ref, w_ref, b_ref, o_ref, buf_ref, *, tl):
    i = pl.program_id(1)
    halo = SUBLANE
    buf_ref[0:halo] = jnp.where(i > 0, prev_ref[0], 0.0)
    buf_ref[halo:halo + tl] = cur_ref[0]
    buf_ref[halo + tl:2 * halo + tl] = jnp.where(i < pl.num_programs(1) - 1, next_ref[0], 0.0)
    acc = jnp.broadcast_to(b_ref[...], (tl, b_ref.shape[1]))
    for k in range(SSD_CONV):
        start = halo + k - SSD_CONV // 2
        acc = acc + w_ref[k:k + 1, :] * buf_ref[start:start + tl, :]
    o_ref[0] = _silu(acc)


def _conv_silu(p3, conv_w8, conv_b, tl):
    b, l, _ = p3.shape
    c = SSD_CONV_DIM
    tl = min(tl, l)
    cb = HYB_XBC0 // c
    nh = l // SUBLANE
    r = tl // SUBLANE
    return pl.pallas_call(
        functools.partial(_conv_kernel, tl=tl),
        out_shape=jax.ShapeDtypeStruct((b, l, c), F32),
        grid=(b, l // tl),
        in_specs=[
            pl.BlockSpec((1, tl, c), lambda bb, i: (bb, i, cb)),
            pl.BlockSpec((1, SUBLANE, c), lambda bb, i: (bb, jnp.maximum(i * r - 1, 0), cb)),
            pl.BlockSpec((1, SUBLANE, c), lambda bb, i: (bb, jnp.minimum((i + 1) * r, nh - 1), cb)),
            pl.BlockSpec((SUBLANE, c), lambda bb, i: (0, 0)),
            pl.BlockSpec((1, c), lambda bb, i: (0, 0)),
        ],
        out_specs=pl.BlockSpec((1, tl, c), lambda bb, i: (bb, i, 0)),
        scratch_shapes=[pltpu.VMEM((tl + 2 * SUBLANE, c), F32)],
        compiler_params=_params("parallel", "parallel"),
        name="dwconv_silu",
    )(p3, p3, p3, conv_w8, conv_b)


def _ssd_kernel(xs_ref, bm_ref, cm_ref, misc_ref, h0_ref, tri_ref, e_ref, bias_ref, a_ref, *rest,
                reverse, finalize):
    if finalize:
        z_ref, y0_ref, dskip_ref, ng_ref, y_ref, hT_ref, st_ref, yb_ref = rest
    else:
        y_ref, hT_ref, st_ref, yb_ref = rest
    q = SSD_CHUNK
    gw = SSD_INNER // SSD_GROUPS
    hpg = SSD_HEADS // SSD_GROUPS

    @pl.when(pl.program_id(1) == 0)
    def _():
        st_ref[...] = h0_ref[0]

    xs = xs_ref[0]
    tri = tri_ref[...]
    visible = tri > 0.5
    lane = lax.broadcasted_iota(jnp.int32, (q, LANE), 1)
    dt_lanes = (lane >= DT_LANE0) & (lane < DT_LANE0 + SSD_HEADS)
    dt = jnp.where(dt_lanes, jax.nn.softplus(misc_ref[0] + bias_ref[...]), 0.0)
    da = dt * a_ref[...]
    cum = _split_dot(da, tri.astype(BF16), 3, left=True)
    cum_t = cum.T
    dt_t = dt.T
    edge = cum[0:1, :] if reverse else cum[q - 1:q, :]
    expcum = jnp.exp(cum)
    w_end = jnp.exp(edge - cum) * dt
    e = e_ref[...]
    expcum_x = _split_dot(expcum, e, 2)
    w_end_x = _split_dot(w_end, e, 2)
    xs_b = xs.astype(BF16)
    xw_b = (xs * w_end_x).astype(BF16)
    st = st_ref[...]
    st_b = st.astype(BF16)
    first_head = lane < SSD_HEAD_DIM
    decay_row = expcum_x[0:1, :] if reverse else expcum_x[q - 1:q, :]
    for g in range(SSD_GROUPS):
        bm = bm_ref[0][:, g * SSD_STATE:(g + 1) * SSD_STATE]
        cm_b = cm_ref[0][:, g * SSD_STATE:(g + 1) * SSD_STATE].astype(BF16)
        cb = _nt_dot(cm_b, bm.astype(BF16))
        gs = slice(g * gw, (g + 1) * gw)
        y_off = jnp.dot(cm_b, st_b[:, gs], preferred_element_type=F32) * expcum_x[:, gs]
        for jp in range(hpg // 2):
            c0 = (g * hpg + 2 * jp) * SSD_HEAD_DIM
            tiles = []
            for sub in range(2):
                col = DT_LANE0 + g * hpg + 2 * jp + sub
                seg = cum[:, col:col + 1] - cum_t[col:col + 1, :]
                lmat = jnp.exp(jnp.where(visible, seg, -jnp.inf))
                mm = (cb * lmat * dt_t[col:col + 1, :]).astype(BF16)
                tiles.append(jnp.dot(mm, xs_b[:, c0:c0 + LANE], preferred_element_type=F32))
            yb_ref[:, c0:c0 + LANE] = jnp.where(first_head, tiles[0], tiles[1]) + y_off[:, c0 - g * gw:c0 - g * gw + LANE]
        new = jnp.dot(bm.T.astype(BF16), xw_b[:, gs], preferred_element_type=F32)
        st_ref[:, gs] = st[:, gs] * decay_row[:, gs] + new

    if finalize:
        v = (yb_ref[...] + y0_ref[0] + dskip_ref[...] * xs) * _silu(z_ref[0])
        for g in range(SSD_GROUPS):
            gs = slice(g * gw, (g + 1) * gw)
            y_ref[0, :, gs] = _rms(v[:, gs], ng_ref[:, gs]).astype(y_ref.dtype)
    else:
        y_ref[0] = yb_ref[...]

    @pl.when(pl.program_id(1) == pl.num_programs(1) - 1)
    def _():
        hT_ref[0] = st_ref[...]


def _ssd_scan(xbc3, p3, h0, tri, expand, bias_row, a_row, reverse, fin=None):
    b, l, _ = xbc3.shape
    nc = l // SSD_CHUNK
    cidx = (lambda s: nc - 1 - s) if reverse else (lambda s: s)
    q = SSD_CHUNK
    in_specs = [
        pl.BlockSpec((1, q, SSD_INNER), lambda bb, s: (bb, cidx(s), 0)),
        pl.BlockSpec((1, q, SSD_BC), lambda bb, s: (bb, cidx(s), SSD_INNER // SSD_BC)),
        pl.BlockSpec((1, q, SSD_BC), lambda bb, s: (bb, cidx(s), SSD_INNER // SSD_BC + 1)),
        pl.BlockSpec((1, q, LANE), lambda bb, s: (bb, cidx(s), HYB_MISC0 // LANE)),
        pl.BlockSpec((1, SSD_STATE, SSD_INNER), lambda bb, s: (bb, 0, 0)),
        pl.BlockSpec((q, q), lambda bb, s: (0, 0)),
        pl.BlockSpec((LANE, SSD_INNER), lambda bb, s: (0, 0)),
        pl.BlockSpec((1, LANE), lambda bb, s: (0, 0)),
        pl.BlockSpec((1, LANE), lambda bb, s: (0, 0)),
    ]
    args = [xbc3, xbc3, xbc3, p3, h0, tri, expand, bias_row, a_row]
    if fin is not None:
        y0, dskip_row, normg_row = fin
        in_specs += [
            pl.BlockSpec((1, q, SSD_INNER), lambda bb, s: (bb, cidx(s), HYB_Z0 // SSD_INNER)),
            pl.BlockSpec((1, q, SSD_INNER), lambda bb, s: (bb, cidx(s), 0)),
            pl.BlockSpec((1, SSD_INNER), lambda bb, s: (0, 0)),
            pl.BlockSpec((1, SSD_INNER), lambda bb, s: (0, 0)),
        ]
        args += [p3, y0, dskip_row, normg_row]
    return pl.pallas_call(
        functools.partial(_ssd_kernel, reverse=reverse, finalize=fin is not None),
        out_shape=(jax.ShapeDtypeStruct((b, l, SSD_INNER), BF16 if fin is not None else F32),
                   jax.ShapeDtypeStruct((b, SSD_STATE, SSD_INNER), F32)),
        grid=(b, nc),
        in_specs=in_specs,
        out_specs=(pl.BlockSpec((1, q, SSD_INNER), lambda bb, s: (bb, cidx(s), 0)),
                   pl.BlockSpec((1, SSD_STATE, SSD_INNER), lambda bb, s: (bb, 0, 0))),
        scratch_shapes=[pltpu.VMEM((SSD_STATE, SSD_INNER), F32), pltpu.VMEM((q, SSD_INNER), F32)],
        compiler_params=_params("parallel", "arbitrary"),
        name="ssd_scan_bwd" if reverse else "ssd_scan_fwd",
    )(*args)


def _qproj_kernel(x_ref, g_ref, w_ref, cos_ref, sin_ref, o_ref):
    h = _rms(x_ref[...], g_ref[...]).astype(BF16)
    r = jnp.dot(h, w_ref[...], preferred_element_type=F32)
    main = MLA_HEADS * MLA_QK_PAD
    cos, sin = cos_ref[...], sin_ref[...]
    for hd in range(MLA_HEADS):
        c0 = hd * MLA_QK_PAD
        o_ref[:, c0:c0 + LANE] = r[:, c0:c0 + LANE].astype(o_ref.dtype)
        swapped = r[:, main + hd * LANE:main + (hd + 1) * LANE]
        o_ref[:, c0 + LANE:c0 + 2 * LANE] = (r[:, c0 + LANE:c0 + 2 * LANE] * cos + swapped * sin).astype(o_ref.dtype)


def _qproj(p, g, w, cos, sin, rows_per_batch, tm):
    m = p.shape[0]
    tm = min(tm, m, rows_per_batch)
    nb = rows_per_batch // tm
    n_out = MLA_HEADS * MLA_QK_PAD
    return pl.pallas_call(
        _qproj_kernel,
        out_shape=jax.ShapeDtypeStruct((m, n_out), BF16),
        grid=(m // tm,),
        in_specs=[
            pl.BlockSpec((tm, MLA_Q_RANK), lambda i: (i, HYB_CQ0 // MLA_Q_RANK)),
            pl.BlockSpec((1, MLA_Q_RANK), lambda i: (0, 0)),
            pl.BlockSpec(w.shape, lambda i: (0, 0)),
            pl.BlockSpec((tm, LANE), lambda i: (i % nb, 0)),
            pl.BlockSpec((tm, LANE), lambda i: (i % nb, 0)),
        ],
        out_specs=pl.BlockSpec((tm, n_out), lambda i: (i, 0)),
        compiler_params=_params("parallel"),
        name="mla_q_proj",
    )(p, g, w, cos, sin)


def _kvproj_kernel(x_ref, misc_ref, misc2_ref, g_ref, w_ref, cos_ref, sin_ref, k_ref, v_ref):
    h = _rms(x_ref[...], g_ref[...]).astype(BF16)
    r = jnp.dot(h, w_ref[...], preferred_element_type=F32)
    kr = (misc_ref[...] * cos_ref[...] + misc2_ref[...] * sin_ref[...]).astype(k_ref.dtype)
    for hd in range(MLA_HEADS):
        c0 = hd * MLA_QK_PAD
        k_ref[:, c0:c0 + LANE] = r[:, hd * MLA_NOPE:(hd + 1) * MLA_NOPE].astype(k_ref.dtype)
        k_ref[:, c0 + LANE:c0 + 2 * LANE] = kr
    v_ref[...] = r[:, MLA_HEADS * MLA_NOPE:].astype(v_ref.dtype)


def _kvproj(p, g, w, cos, sin, rows_per_batch, tm):
    m = p.shape[0]
    tm = min(tm, m, rows_per_batch)
    nb = rows_per_batch // tm
    nk, nv = MLA_HEADS * MLA_QK_PAD, MLA_HEADS * MLA_V
    return pl.pallas_call(
        _kvproj_kernel,
        out_shape=(jax.ShapeDtypeStruct((m, nk), BF16), jax.ShapeDtypeStruct((m, nv), BF16)),
        grid=(m // tm,),
        in_specs=[
            pl.BlockSpec((tm, MLA_KV_RANK), lambda i: (i, HYB_CKV0 // MLA_KV_RANK)),
            pl.BlockSpec((tm, LANE), lambda i: (i, HYB_MISC0 // LANE)),
            pl.BlockSpec((tm, LANE), lambda i: (i, HYB_MISC2 // LANE)),
            pl.BlockSpec((1, MLA_KV_RANK), lambda i: (0, 0)),
            pl.BlockSpec(w.shape, lambda i: (0, 0)),
            pl.BlockSpec((tm, LANE), lambda i: (i % nb, 0)),
            pl.BlockSpec((tm, LANE), lambda i: (i % nb, 0)),
        ],
        out_specs=(pl.BlockSpec((tm, nk), lambda i: (i, 0)), pl.BlockSpec((tm, nv), lambda i: (i, 0))),
        compiler_params=_params("parallel"),
        name="mla_kv_proj",
    )(p, p, p, g, w, cos, sin)


def _attn_kernel(q_ref, kc_ref, vc_ref, *rest, with_latent):
    if with_latent:
        k_ref, v_ref, o_ref, m_ref, l_ref, acc_ref = rest
    else:
        o_ref, m_ref, l_ref, acc_ref = rest
    ki = pl.program_id(2)

    def update(hd, kblk, vblk):
        s = _nt_dot(q_ref[:, hd * MLA_QK_PAD:(hd + 1) * MLA_QK_PAD], kblk)
        tiles = s.shape[1] // LANE
        m_prev = m_ref[hd]
        m_new = jnp.maximum(m_prev, jnp.max(s, axis=-1, keepdims=True))
        alpha = jnp.exp2((m_prev - m_new) * EXP2_SCALE)
        p = jnp.exp2((s - jnp.tile(m_new, (1, tiles))) * EXP2_SCALE)
        psum = p[:, 0:LANE]
        for c in range(1, tiles):
            psum = psum + p[:, c * LANE:(c + 1) * LANE]
        l_ref[hd] = alpha * l_ref[hd] + psum
        vs = slice(hd * MLA_V, (hd + 1) * MLA_V)
        acc_ref[:, vs] = alpha * acc_ref[:, vs] + jnp.dot(p.astype(BF16), vblk, preferred_element_type=F32)
        m_ref[hd] = m_new

    @pl.when(ki == 0)
    def _():
        m_ref[...] = jnp.full(m_ref.shape, -jnp.inf, F32)
        l_ref[...] = jnp.zeros(l_ref.shape, F32)
        acc_ref[...] = jnp.zeros(acc_ref.shape, F32)
        for hd in range(MLA_HEADS):
            update(hd, kc_ref[:, hd * MLA_QK_PAD:(hd + 1) * MLA_QK_PAD], vc_ref[:, hd * MLA_V:(hd + 1) * MLA_V])

    if with_latent:
        for hd in range(MLA_HEADS):
            update(hd, k_ref[:, hd * MLA_QK_PAD:(hd + 1) * MLA_QK_PAD], v_ref[:, hd * MLA_V:(hd + 1) * MLA_V])

    @pl.when(ki == pl.num_programs(2) - 1)
    def _():
        for hd in range(MLA_HEADS):
            vs = slice(hd * MLA_V, (hd + 1) * MLA_V)
            o_ref[:, vs] = (acc_ref[:, vs] / jnp.sum(l_ref[hd], axis=-1, keepdims=True)).astype(o_ref.dtype)


def _attention(q, kc, vc, k, v, batch, tq, tk):
    m = q.shape[0]
    lq = m // batch
    lc = kc.shape[0] // batch
    tq = min(tq, lq)
    nq = lq // tq
    nq_pad, nv = MLA_HEADS * MLA_QK_PAD, MLA_HEADS * MLA_V
    in_specs = [
        pl.BlockSpec((tq, nq_pad), lambda b, i, j: (b * nq + i, 0)),
        pl.BlockSpec((lc, nq_pad), lambda b, i, j: (b, 0)),
        pl.BlockSpec((lc, nv), lambda b, i, j: (b, 0)),
    ]
    args = [q, kc, vc]
    nk = 1
    if k is not None:
        lk = k.shape[0] // batch
        tk = min(tk, lk)
        nk = lk // tk
        in_specs += [
            pl.BlockSpec((tk, nq_pad), lambda b, i, j: (b * nk + j, 0)),
            pl.BlockSpec((tk, nv), lambda b, i, j: (b * nk + j, 0)),
        ]
        args += [k, v]
    return pl.pallas_call(
        functools.partial(_attn_kernel, with_latent=k is not None),
        out_shape=jax.ShapeDtypeStruct((m, nv), BF16),
        grid=(batch, nq, nk),
        in_specs=in_specs,
        out_specs=pl.BlockSpec((tq, nv), lambda b, i, j: (b * nq + i, 0)),
        scratch_shapes=[pltpu.VMEM((MLA_HEADS, tq, LANE), F32), pltpu.VMEM((MLA_HEADS, tq, LANE), F32),
                        pltpu.VMEM((tq, nv), F32)],
        compiler_params=_params("parallel", "parallel", "arbitrary"),
        name="mla_attention",
    )(*args)


def _outproj_kernel(a1_ref, a2_ref, w1_ref, w2_ref, x_ref, gate_ref, o_ref):
    acc = jnp.dot(a1_ref[...], w1_ref[...], preferred_element_type=F32)
    acc = acc + jnp.dot(a2_ref[...], w2_ref[...], preferred_element_type=F32)
    o_ref[...] = x_ref[...] + gate_ref[0] * acc


def _outproj(a1, a2, w1, w2, x, mod, k_gate, rows_per_batch, tm):
    m, d = x.shape
    tm = min(tm, m)
    bidx = lambda i: (i * tm) // rows_per_batch
    return pl.pallas_call(
        _outproj_kernel,
        out_shape=jax.ShapeDtypeStruct((m, d), F32),
        grid=(m // tm,),
        in_specs=[
            pl.BlockSpec((tm, a1.shape[1]), lambda i: (i, 0)),
            pl.BlockSpec((tm, a2.shape[1]), lambda i: (i, 0)),
            pl.BlockSpec(w1.shape, lambda i: (0, 0)),
            pl.BlockSpec(w2.shape, lambda i: (0, 0)),
            pl.BlockSpec((tm, d), lambda i: (i, 0)),
            pl.BlockSpec((1, 1, d), lambda i: (bidx(i) * N_MOD + k_gate, 0, 0)),
        ],
        out_specs=pl.BlockSpec((tm, d), lambda i: (i, 0)),
        compiler_params=_params("parallel"),
        name="hyb_out_proj",
    )(a1, a2, w1, w2, x, mod)


def _gmlp_kernel(x_ref, g_ref, sh_ref, sc_ref, gate_ref, win_ref, lng_ref, lnb_ref, ws_ref, bs_ref, wout_ref,
                 o_ref, uv_ref):
    x = x_ref[...]
    tm = x.shape[0]
    h = (_rms(x, g_ref[...]) * (1.0 + sc_ref[0]) + sh_ref[0]).astype(BF16)
    uv = _gelu(jnp.dot(h, win_ref[...], preferred_element_type=F32))
    u, v = uv[:, :GM_INNER], uv[:, GM_INNER:]
    mu = jnp.mean(v, axis=-1, keepdims=True)
    var = jnp.mean(jnp.square(v - mu), axis=-1, keepdims=True)
    vn = ((v - mu) * lax.rsqrt(var + EPS) * lng_ref[...] + lnb_ref[...]).astype(BF16)
    gw = GM_INNER // GM_GROUPS
    for c in range(tm // GM_CHUNK):
        rs = slice(c * GM_CHUNK, (c + 1) * GM_CHUNK)
        for g in range(GM_GROUPS):
            cs = slice(g * gw, (g + 1) * gw)
            sv = jnp.dot(ws_ref[g], vn[rs, cs], preferred_element_type=F32) + bs_ref[:, cs]
            uv_ref[rs, cs] = (u[rs, cs] * sv).astype(BF16)
    o_ref[...] = x + gate_ref[0] * jnp.dot(uv_ref[...], wout_ref[...], preferred_element_type=F32)


def _gmlp(x, g, mod, rows_per_batch, w_in, ln_g, ln_b, ws, bs_full, w_out, tm):
    m, d = x.shape
    tm = min(tm, m)
    bidx = lambda i: (i * tm) // rows_per_batch
    const = lambda shape: pl.BlockSpec(shape, lambda i: (0,) * len(shape))
    modspec = lambda k: pl.BlockSpec((1, 1, d), lambda i: (bidx(i) * N_MOD + k, 0, 0))
    return pl.pallas_call(
        _gmlp_kernel,
        out_shape=jax.ShapeDtypeStruct((m, d), F32),
        grid=(m // tm,),
        in_specs=[
            pl.BlockSpec((tm, d), lambda i: (i, 0)), const((1, d)), modspec(0), modspec(1), modspec(2),
            const(w_in.shape), const((1, GM_INNER)), const((1, GM_INNER)), const(ws.shape), const(bs_full.shape),
            const(w_out.shape),
        ],
        out_specs=pl.BlockSpec((tm, d), lambda i: (i, 0)),
        scratch_shapes=[pltpu.VMEM((tm, GM_INNER), BF16)],
        compiler_params=_params("parallel"),
        name="chunk_gmlp",
    )(x, g, mod, mod, mod, w_in, ln_g, ln_b, ws, bs_full, w_out)


def _top_values(xs, k, want_rank):
    xs = list(xs)
    vals = [[] for _ in xs]
    ranks = [jnp.full(x.shape, float(k), F32) if w else None for x, w in zip(xs, want_rank)]
    for r in range(k):
        for i, x in enumerate(xs):
            mx = jnp.max(x, axis=0, keepdims=True)
            vals[i].append(mx)
            hit = x == mx
            if want_rank[i]:
                ranks[i] = jnp.where(hit, float(r), ranks[i])
            if r + 1 < k:
                xs[i] = jnp.where(hit, -jnp.inf, x)
    return vals, ranks


def _peer_stats_kernel(x_ref, g_ref, sh_ref, sc_ref, wq_ref, k1_ref, k2_ref, t_ref, a_ref, n_ref, b_ref, r2_ref,
                       q_ref, cand_ref):
    h = (_rms(x_ref[...], g_ref[...]) * (1.0 + sc_ref[0]) + sh_ref[0]).astype(BF16)
    t_ref[...] = h
    q_ref[...] = jnp.dot(h, wq_ref[...], preferred_element_type=F32).astype(BF16)
    half = PEER_DKEY // 2
    tm = h.shape[0]
    n_pairs = len(PEER_PAIRS)
    cand_ref[n_pairs:, :] = jnp.full((cand_ref.shape[0] - n_pairs, tm), -jnp.inf, F32)

    def head(hd, carry):
        c0 = pl.multiple_of(hd * PEER_DKEY, PEER_DKEY)
        s1 = _nt_dot(k1_ref[hd], q_ref[:, pl.ds(c0, half)])
        s2 = _nt_dot(k2_ref[hd], q_ref[:, pl.ds(c0 + half, half)])
        (v1, v2), (_, rank2) = _top_values((s1, s2), PEER_TOPK, (False, True))
        for idx, (a, b) in enumerate(PEER_PAIRS):
            cand_ref[idx:idx + 1, :] = v1[a] + v2[b]
        (tops,), _ = _top_values((cand_ref[...],), PEER_TOPK, (False,))
        thr = tops[PEER_TOPK - 1]
        z = tops[0] * 0.0
        for tv in tops:
            z = z + jnp.exp(tv - tops[0])
        n = jnp.zeros(s1.shape, F32)
        for vb in v2:
            n = n + jnp.where(s1 + vb >= thr, 1.0, 0.0)
        a_ref[hd] = jnp.exp(s1 - v1[0]) * (0.5 / z)
        n_ref[hd] = n
        bvals = jnp.exp(s2 - v2[0])
        for c in range(tm // LANE):
            b_ref[hd, c] = bvals[:, c * LANE:(c + 1) * LANE]
            r2_ref[hd, c] = rank2[:, c * LANE:(c + 1) * LANE]
        return carry

    lax.fori_loop(0, PEER_HEADS, head, 0)


def _peer_stats(x, g, mod, rows_per_batch, wq, k1, k2, tm):
    m, d = x.shape
    tm = min(tm, m)
    bidx = lambda i: (i * tm) // rows_per_batch
    const = lambda shape: pl.BlockSpec(shape, lambda i: (0,) * len(shape))
    modspec = lambda k: pl.BlockSpec((1, 1, d), lambda i: (bidx(i) * N_MOD + k, 0, 0))
    return pl.pallas_call(
        _peer_stats_kernel,
        out_shape=(jax.ShapeDtypeStruct((m, d), BF16),
                   jax.ShapeDtypeStruct((PEER_HEADS, PEER_NKEYS, m), F32),
                   jax.ShapeDtypeStruct((PEER_HEADS, PEER_NKEYS, m), F32),
                   jax.ShapeDtypeStruct((PEER_HEADS, m // LANE, PEER_NKEYS, LANE), F32),
                   jax.ShapeDtypeStruct((PEER_HEADS, m // LANE, PEER_NKEYS, LANE), F32)),
        grid=(m // tm,),
        in_specs=[pl.BlockSpec((tm, d), lambda i: (i, 0)), const((1, d)), modspec(3), modspec(4),
                  const(wq.shape), const(k1.shape), const(k2.shape)],
        out_specs=(pl.BlockSpec((tm, d), lambda i: (i, 0)),) + tuple(
            pl.BlockSpec((PEER_HEADS, PEER_NKEYS, tm), lambda i: (0, 0, i)) for _ in range(2)) + tuple(
            pl.BlockSpec((PEER_HEADS, tm // LANE, PEER_NKEYS, LANE), lambda i: (0, i, 0, 0)) for _ in range(2)),
        scratch_shapes=[pltpu.VMEM((tm, PEER_HEADS * PEER_DKEY), BF16),
                        pltpu.VMEM((-(-len(PEER_PAIRS) // SUBLANE) * SUBLANE, tm), F32)],
        compiler_params=_params("parallel"),
        name="peer_topk",
    )(x, g, mod, mod, wq, k1, k2)


def _peer_expert_kernel(t_ref, u_ref, vt_ref, a_ref, n_ref, b_ref, r2_ref, x_ref, gate_ref, o_ref,
                        acc_ref, s_ref, g_ref, *, te):
    j = pl.program_id(1)
    d, tm = acc_ref.shape
    rows = te // PEER_NKEYS
    sub = PEER_NKEYS // 4
    group = rows

    @pl.when(j == 0)
    def _():
        acc_ref[...] = jnp.zeros(acc_ref.shape, F32)

    halves = [slice(h0, h0 + MXU_COLS) for h0 in range(0, tm, MXU_COLS)]
    for hs in halves:
        s_ref[:, hs] = _nt_dot(u_ref[...], t_ref[hs, :])
    for c in range(tm // LANE):
        cs = slice(c * LANE, (c + 1) * LANE)
        for k0 in range(0, PEER_NKEYS, sub):
            ks = slice(k0, k0 + sub)
            for r0 in range(0, rows, group):
                w = [None] * group
                for hd in range(PEER_HEADS):
                    bv, rk = b_ref[hd, c, ks, :], r2_ref[hd, c, ks, :]
                    for q in range(group):
                        r = r0 + q
                        term = a_ref[hd, r:r + 1, cs] * jnp.where(rk < n_ref[hd, r:r + 1, cs], bv, 0.0)
                        w[q] = term if w[q] is None else w[q] + term
                for q in range(group):
                    es = slice((r0 + q) * PEER_NKEYS + k0, (r0 + q) * PEER_NKEYS + k0 + sub)
                    sc = s_ref[es, cs]
                    g_ref[es, cs] = (sc * (1.0 + lax.erf(sc * (2.0 ** -0.5))) * w[q]).astype(BF16)
    for hs in halves:
        acc_ref[:, hs] += jnp.dot(vt_ref[0], g_ref[:, hs], preferred_element_type=F32)

    @pl.when(j == pl.num_programs(1) - 1)
    def _():
        o_ref[...] = x_ref[...] + gate_ref[0] * acc_ref[...].T


def _peer_experts(t, u, vt, a, cnt, b, r2, x, mod, k_gate, rows_per_batch, tm, te):
    m, d = x.shape
    tm = min(tm, m)
    ne = u.shape[0]
    bidx = lambda i: (i * tm) // rows_per_batch
    rows = te // PEER_NKEYS
    return pl.pallas_call(
        functools.partial(_peer_expert_kernel, te=te),
        out_shape=jax.ShapeDtypeStruct((m, d), F32),
        grid=(m // tm, ne // te),
        in_specs=[
            pl.BlockSpec((tm, d), lambda i, j: (i, 0)),
            pl.BlockSpec((te, d), lambda i, j: (j, 0)),
            pl.BlockSpec((1, d, te), lambda i, j: (j, 0, 0)),
            pl.BlockSpec((PEER_HEADS, rows, tm), lambda i, j: (0, j, i)),
            pl.BlockSpec((PEER_HEADS, rows, tm), lambda i, j: (0, j, i)),
            pl.BlockSpec((PEER_HEADS, tm // LANE, PEER_NKEYS, LANE), lambda i, j: (0, i, 0, 0)),
            pl.BlockSpec((PEER_HEADS, tm // LANE, PEER_NKEYS, LANE), lambda i, j: (0, i, 0, 0)),
            pl.BlockSpec((tm, d), lambda i, j: (i, 0)),
            pl.BlockSpec((1, 1, d), lambda i, j: (bidx(i) * N_MOD + k_gate, 0, 0)),
        ],
        out_specs=pl.BlockSpec((tm, d), lambda i, j: (i, 0)),
        scratch_shapes=[pltpu.VMEM((d, tm), F32), pltpu.VMEM((te, tm), F32), pltpu.VMEM((te, tm), BF16)],
        compiler_params=_params("parallel", "arbitrary"),
        name="peer_experts",
    )(t, u, vt, a, cnt, b, r2, x, mod)


def _final_norm_kernel(x_ref, g_ref, o_ref):
    o_ref[...] = _rms(x_ref[...], g_ref[...])


def _final_norm(x, g, tm):
    m, d = x.shape
    tm = min(tm, m)
    return pl.pallas_call(
        _final_norm_kernel,
        out_shape=jax.ShapeDtypeStruct((m, d), F32),
        grid=(m // tm,),
        in_specs=[pl.BlockSpec((tm, d), lambda i: (i, 0)), pl.BlockSpec((1, d), lambda i: (0, 0))],
        out_specs=pl.BlockSpec((tm, d), lambda i: (i, 0)),
        compiler_params=_params("parallel"),
        name="final_rmsnorm",
    )(x, g)


def _rope_tables(n_lat, n_ctx):
    rows = n_lat // GRID_W
    row = jnp.broadcast_to(jnp.arange(rows, dtype=F32)[:, None], (rows, GRID_W)).reshape(-1)
    col = jnp.broadcast_to(jnp.arange(GRID_W, dtype=F32)[None, :], (rows, GRID_W)).reshape(-1)
    n_freq = MLA_ROPE // 4
    inv = ROPE_BASE ** (-jnp.arange(n_freq, dtype=F32) / n_freq)
    ang = jnp.concatenate([row[:, None] * inv, col[:, None] * inv], axis=-1)
    cos, sin = jnp.cos(ang), jnp.sin(ang)
    pad = jnp.zeros((n_lat, LANE - MLA_ROPE), F32)
    cos_l = jnp.concatenate([cos, cos, pad], axis=-1)
    sin_l = jnp.concatenate([-sin, sin, pad], axis=-1)
    cos_c = jnp.concatenate([jnp.ones((n_ctx, MLA_ROPE), F32), jnp.zeros((n_ctx, LANE - MLA_ROPE), F32)], axis=-1)
    return (cos_l, sin_l), (cos_c, jnp.zeros((n_ctx, LANE), F32))


def _swap_halves(w):
    half = w.shape[-1] // 2
    return jnp.concatenate([w[..., half:], w[..., :half]], axis=-1)


def _prep_hybrid(w_in, conv_w, conv_b, a_log, dt_bias, d_skip, norm_g, q_norm_g, w_qb, kv_norm_g, w_kvb, w_out):
    d = w_in.shape[0]
    o = 0
    pieces = {}
    for name, width in (("z", SSD_INNER), ("xbc", SSD_CONV_DIM), ("dt", SSD_HEADS), ("cq", MLA_Q_RANK),
                        ("ckv", MLA_KV_RANK), ("kr", MLA_ROPE)):
        pieces[name] = w_in[:, o:o + width]
        o += width
    zeros = lambda n: jnp.zeros((d, n), w_in.dtype)
    w_hyb = jnp.concatenate([
        pieces["z"], pieces["kr"], pieces["dt"], zeros(LANE - MLA_ROPE - SSD_HEADS), pieces["cq"], pieces["xbc"],
        pieces["ckv"], _swap_halves(pieces["kr"]), zeros(LANE - MLA_ROPE)], axis=1).astype(BF16)
    conv_w8 = jnp.concatenate([conv_w.T, jnp.zeros((SUBLANE - SSD_CONV, SSD_CONV_DIM), F32)], axis=0)
    lane_row = lambda v: jnp.zeros((1, LANE), F32).at[0, DT_LANE0:DT_LANE0 + SSD_HEADS].set(v)
    dirs = [(lane_row(dt_bias[k]), lane_row(-jnp.exp(a_log[k]))) for k in range(2)]
    expand = jnp.zeros((LANE, SSD_INNER), F32).at[DT_LANE0:DT_LANE0 + SSD_HEADS].set(
        jnp.repeat(jnp.eye(SSD_HEADS, dtype=F32), SSD_HEAD_DIM, axis=1)).astype(BF16)
    wq3 = w_qb.reshape(MLA_Q_RANK, MLA_HEADS, MLA_NOPE + MLA_ROPE)
    zq = jnp.zeros((MLA_Q_RANK, MLA_HEADS, LANE - MLA_ROPE), w_qb.dtype)
    q_main = jnp.concatenate([wq3, zq], axis=-1).reshape(MLA_Q_RANK, MLA_HEADS * MLA_QK_PAD)
    q_swap = jnp.concatenate([_swap_halves(wq3[..., MLA_NOPE:]), zq], axis=-1).reshape(MLA_Q_RANK, MLA_HEADS * LANE)
    wq = jnp.concatenate([q_main, q_swap], axis=1).astype(BF16)
    wkv3 = w_kvb.reshape(MLA_KV_RANK, MLA_HEADS, MLA_NOPE + MLA_V)
    wkv = jnp.concatenate([wkv3[..., :MLA_NOPE].reshape(MLA_KV_RANK, -1), wkv3[..., MLA_NOPE:].reshape(MLA_KV_RANK, -1)],
                          axis=1).astype(BF16)
    return dict(
        w_hyb=w_hyb, conv_w8=conv_w8, conv_b=conv_b[None, :], dirs=dirs, expand=expand,
        dskip=jnp.repeat(d_skip, SSD_HEAD_DIM)[None, :], normg=norm_g[None, :],
        q_norm_g=q_norm_g[None, :], wq=wq, kv_norm_g=kv_norm_g[None, :], wkv=wkv,
        w_out1=w_out[:SSD_INNER].astype(BF16), w_out2=w_out[SSD_INNER:].astype(BF16))


def _hybrid_layer(xl, xc, batch, g1, mod_l, mod_c, hp, ropes, update_ctx):
    n_lat, n_ctx = xl.shape[0] // batch, xc.shape[0] // batch
    (cos_l, sin_l), (cos_c, sin_c) = ropes
    q = SSD_CHUNK
    tri_f = jnp.tril(jnp.ones((q, q), F32))
    tri_b = jnp.triu(jnp.ones((q, q), F32))
    pl_ = _nmm(xl, g1, mod_l, 0, 1, n_lat, hp["w_hyb"], 512, HYB_COLS // 3)
    pc_ = _nmm(xc, g1, mod_c, 0, 1, xc.shape[0], hp["w_hyb"], 256, HYB_COLS // 3)
    pl3, pc3 = pl_.reshape(batch, n_lat, HYB_COLS), pc_.reshape(batch, n_ctx, HYB_COLS)
    xbc_l = _conv_silu(pl3, hp["conv_w8"], hp["conv_b"], 512)
    xbc_c = _conv_silu(pc3, hp["conv_w8"], hp["conv_b"], 512)
    h0 = jnp.zeros((batch, SSD_STATE, SSD_INNER), F32)
    (bias0, a0), (bias1, a1) = hp["dirs"]
    yc0, hc0 = _ssd_scan(xbc_c, pc3, h0, tri_f, hp["expand"], bias0, a0, False)
    yl0, _ = _ssd_scan(xbc_l, pl3, hc0, tri_f, hp["expand"], bias0, a0, False)
    fin = lambda y0: (y0, hp["dskip"], hp["normg"])
    sc, hc1 = _ssd_scan(xbc_c, pc3, h0, tri_b, hp["expand"], bias1, a1, True, fin(yc0))
    sl, _ = _ssd_scan(xbc_l, pl3, hc1, tri_b, hp["expand"], bias1, a1, True, fin(yl0))
    ql = _qproj(pl_, hp["q_norm_g"], hp["wq"], cos_l, sin_l, n_lat, 512)
    kl, vl = _kvproj(pl_, hp["kv_norm_g"], hp["wkv"], cos_l, sin_l, n_lat, 512)
    kc, vc = _kvproj(pc_, hp["kv_norm_g"], hp["wkv"], cos_c, sin_c, n_ctx, 256)
    att_l = _attention(ql, kc, vc, kl, vl, batch, 512, 1024)
    xl_new = _outproj(sl.reshape(-1, SSD_INNER), att_l, hp["w_out1"], hp["w_out2"], xl, mod_l, 2, n_lat, 512)
    xc_new = xc
    if update_ctx:
        qc = _qproj(pc_, hp["q_norm_g"], hp["wq"], cos_c, sin_c, n_ctx, 256)
        att_c = _attention(qc, kc, vc, None, None, batch, 256, 256)
        xc_new = _outproj(sc.reshape(-1, SSD_INNER), att_c, hp["w_out1"], hp["w_out2"], xc, mod_c, 2, xc.shape[0], 256)
    return xl_new, xc_new


def _peer_layer(x, g2, mod, rows_per_batch, pp):
    t, a, cnt, b, r2 = _peer_stats(x, g2, mod, rows_per_batch, pp["wq"], pp["k1"], pp["k2"], 256)
    return _peer_experts(t, pp["u"], pp["vt"], a, cnt, b, r2, x, mod, 5, rows_per_batch, PEER_TM, PEER_TE)


def kernel(x, c, ctx, c_ctx, ada_w, ada_b, norm1_g, norm2_g, hyb_w_in, ssd_conv_w, ssd_conv_b, ssd_a_log,
           ssd_dt_bias, ssd_d, ssd_norm_g, mla_q_norm_g, mla_w_qb, mla_kv_norm_g, mla_w_kvb, hyb_w_out, gm_w_in,
           gm_ln_g, gm_ln_b, gm_ws, gm_bs, gm_w_out, peer_wq, peer_k1, peer_k2, peer_u, peer_v, final_norm_g):
    batch, n_lat, d = x.shape
    n_ctx = ctx.shape[1]
    depth = ada_w.shape[0]
    xl = x.reshape(batch * n_lat, d)
    xc = ctx.reshape(batch * n_ctx, d)
    n_cond = -(-(batch + 1) // SUBLANE) * SUBLANE
    conds = jnp.zeros((n_cond, d), F32).at[:batch].set(c).at[batch].set(c_ctx)
    mods = _mods(conds, ada_w, ada_b)
    ropes = _rope_tables(n_lat, n_ctx)
    for layer in range(depth):
        i = layer // 2
        even = layer % 2 == 0
        keep_ctx = any(j % 2 == 0 for j in range(layer + 1, depth))
        mod_l = mods[layer, :batch].reshape(batch * N_MOD, 1, d)
        mod_c = mods[layer, batch].reshape(N_MOD, 1, d)
        g1, g2 = norm1_g[layer][None, :], norm2_g[layer][None, :]
        if even:
            hp = _prep_hybrid(hyb_w_in[i], ssd_conv_w[i], ssd_conv_b[i], ssd_a_log[i], ssd_dt_bias[i], ssd_d[i],
                              ssd_norm_g[i], mla_q_norm_g[i], mla_w_qb[i], mla_kv_norm_g[i], mla_w_kvb[i], hyb_w_out[i])
            xl, xc = _hybrid_layer(xl, xc, batch, g1, mod_l, mod_c, hp, ropes, keep_ctx)
        else:
            gp = (gm_w_in[i].astype(BF16), gm_ln_g[i][None, :], gm_ln_b[i][None, :], gm_ws[i].astype(BF16),
                  jnp.repeat(gm_bs[i].T, GM_INNER // GM_GROUPS, axis=1), gm_w_out[i].astype(BF16))
            xl = _gmlp(xl, g1, mod_l, n_lat, *gp, 256)
            if keep_ctx:
                xc = _gmlp(xc, g1, mod_c, xc.shape[0], *gp, 256)
        pp = dict(wq=peer_wq[layer].astype(BF16), k1=peer_k1[layer].astype(BF16), k2=peer_k2[layer].astype(BF16),
                  u=peer_u[layer].astype(BF16),
                  vt=jnp.transpose(peer_v[layer].astype(BF16).reshape(-1, PEER_TE, d), (0, 2, 1)))
        xl = _peer_layer(xl, g2, mod_l, n_lat, pp)
        if keep_ctx:
            xc = _peer_layer(xc, g2, mod_c, xc.shape[0], pp)
    return _final_norm(xl, final_norm_g[None, :], 512).reshape(batch, n_lat, d)
```

```python
import functools

import jax
import jax.numpy as jnp
from jax import lax
from jax.experimental import pallas as pl
from jax.experimental.pallas import tpu as pltpu

F32, BF16 = jnp.float32, jnp.bfloat16
EPS = 1e-6
LANE = 128
SUBLANE = 8
MXU_COLS = 256
VMEM_LIMIT = 56 * 1024 * 1024

D_MODEL = 1024
DEPTH = 4
GRID_W = 64
N_MOD = 6
SSD_HEADS = 16
SSD_HEAD_DIM = 64
SSD_INNER = SSD_HEADS * SSD_HEAD_DIM
SSD_GROUPS = 2
SSD_STATE = 128
SSD_BC = SSD_GROUPS * SSD_STATE
SSD_CONV = 5
SSD_CONV_DIM = SSD_INNER + 2 * SSD_BC
SSD_CHUNK = 128
MLA_HEADS = 8
MLA_Q_RANK = 384
MLA_KV_RANK = 256
MLA_NOPE = 128
MLA_ROPE = 64
MLA_V = 128
MLA_SCALE = (MLA_NOPE + MLA_ROPE) ** -0.5
EXP2_SCALE = MLA_SCALE * 1.4426950408889634
ROPE_BASE = 10000.0
GM_CHUNK = 128
GM_INNER = 2 * D_MODEL
GM_GROUPS = 8
PEER_HEADS = 8
PEER_NKEYS = 128
PEER_EXPERTS = PEER_NKEYS * PEER_NKEYS
PEER_DKEY = 256
PEER_TOPK = 16
PEER_TE = 1024
PEER_TM = 512

HYB_Z0 = 0
HYB_MISC0 = 1024
HYB_CQ0 = 1152
HYB_XBC0 = 1536
HYB_CKV0 = 3072
HYB_MISC2 = 3328
HYB_COLS = 3456
DT_LANE0 = MLA_ROPE
MLA_QK_PAD = 256

PEER_PAIRS = tuple((a, b) for a in range(PEER_TOPK) for b in range(PEER_TOPK) if (a + 1) * (b + 1) <= PEER_TOPK)


def _params(*sem, flags=None):
    return pltpu.CompilerParams(dimension_semantics=sem, vmem_limit_bytes=VMEM_LIMIT, flags=flags)


def _rms(x, g):
    return x * lax.rsqrt(jnp.mean(x * x, axis=-1, keepdims=True) + EPS) * g


def _silu(x):
    return x * jax.nn.sigmoid(x)


def _gelu(x):
    return 0.5 * x * (1.0 + lax.erf(x * (2.0 ** -0.5)))


def _nt_dot(a, b):
    return lax.dot_general(a, b, (((1,), (1,)), ((), ())), preferred_element_type=F32)


def _split_dot(x, w, terms, left=False):
    acc = None
    rem = x
    for _ in range(terms):
        hi = rem.astype(BF16)
        part = jnp.dot(w, hi, preferred_element_type=F32) if left else jnp.dot(hi, w, preferred_element_type=F32)
        acc = part if acc is None else acc + part
        rem = rem - hi.astype(F32)
    return acc


def _mods_kernel(c_ref, w_ref, b_ref, o_ref):
    h = _silu(c_ref[...]).astype(BF16)
    o_ref[0] = jnp.dot(h, w_ref[0].astype(BF16), preferred_element_type=F32) + b_ref[0]


def _mods(conds, ada_w, ada_b):
    depth, d, n = ada_w.shape
    tn = 1536
    return pl.pallas_call(
        _mods_kernel,
        out_shape=jax.ShapeDtypeStruct((depth, conds.shape[0], n), F32),
        grid=(depth, n // tn),
        in_specs=[
            pl.BlockSpec(conds.shape, lambda l, j: (0, 0)),
            pl.BlockSpec((1, d, tn), lambda l, j: (l, 0, j)),
            pl.BlockSpec((1, 1, tn), lambda l, j: (l, 0, j)),
        ],
        out_specs=pl.BlockSpec((1, conds.shape[0], tn), lambda l, j: (l, 0, j)),
        compiler_params=_params("parallel", "parallel"),
        name="ada_mods",
    )(conds, ada_w, ada_b.reshape(depth, 1, n))


def _nmm_kernel(x_ref, g_ref, sh_ref, sc_ref, w_ref, o_ref, h_ref):
    @pl.when(pl.program_id(1) == 0)
    def _():
        y = _rms(x_ref[...], g_ref[...])
        h_ref[...] = (y * (1.0 + sc_ref[0]) + sh_ref[0]).astype(BF16)

    o_ref[...] = jnp.dot(h_ref[...], w_ref[...], preferred_element_type=F32).astype(o_ref.dtype)


def _nmm(x, g, mod, k_shift, k_scale, rows_per_batch, w, tm, tn):
    m, k = x.shape
    n = w.shape[1]
    tm = min(tm, m)
    bidx = lambda i: (i * tm) // rows_per_batch
    return pl.pallas_call(
        _nmm_kernel,
        out_shape=jax.ShapeDtypeStruct((m, n), F32),
        grid=(m // tm, n // tn),
        in_specs=[
            pl.BlockSpec((tm, k), lambda i, j: (i, 0)),
            pl.BlockSpec((1, k), lambda i, j: (0, 0)),
            pl.BlockSpec((1, 1, k), lambda i, j: (bidx(i) * N_MOD + k_shift, 0, 0)),
            pl.BlockSpec((1, 1, k), lambda i, j: (bidx(i) * N_MOD + k_scale, 0, 0)),
            pl.BlockSpec((k, tn), lambda i, j: (0, j)),
        ],
        out_specs=pl.BlockSpec((tm, tn), lambda i, j: (i, j)),
        scratch_shapes=[pltpu.VMEM((tm, k), BF16)],
        compiler_params=_params("parallel", "arbitrary"),
        name="norm_mod_matmul",
    )(x, g, mod, mod, w)


def _conv_kernel(cur_ref, prev_ref, next_ref, w_ref, b_ref, o_ref, buf_ref, *, tl):
    i = pl.program_id(1)
    halo = SUBLANE
    buf_ref[0:halo] = jnp.where(i > 0, prev_ref[0], 0.0)
    buf_ref[halo:halo + tl] = cur_ref[0]
    buf_ref[halo + tl:2 * halo + tl] = jnp.where(i < pl.num_programs(1) - 1, next_ref[0], 0.0)
    acc = jnp.broadcast_to(b_ref[...], (tl, b_ref.shape[1]))
    for k in range(SSD_CONV):
        start = halo + k - SSD_CONV // 2
        acc = acc + w_ref[k:k + 1, :] * buf_ref[start:start + tl, :]
    o_ref[0] = _silu(acc)


def _conv_silu(p3, conv_w8, conv_b, tl):
    b, l, _ = p3.shape
    c = SSD_CONV_DIM
    tl = min(tl, l)
    cb = HYB_XBC0 // c
    nh = l // SUBLANE
    r = tl // SUBLANE
    return pl.pallas_call(
        functools.partial(_conv_kernel, tl=tl),
        out_shape=jax.ShapeDtypeStruct((b, l, c), F32),
        grid=(b, l // tl),
        in_specs=[
            pl.BlockSpec((1, tl, c), lambda bb, i: (bb, i, cb)),
            pl.BlockSpec((1, SUBLANE, c), lambda bb, i: (bb, jnp.maximum(i * r - 1, 0), cb)),
            pl.BlockSpec((1, SUBLANE, c), lambda bb, i: (bb, jnp.minimum((i + 1) * r, nh - 1), cb)),
            pl.BlockSpec((SUBLANE, c), lambda bb, i: (0, 0)),
            pl.BlockSpec((1, c), lambda bb, i: (0, 0)),
        ],
        out_specs=pl.BlockSpec((1, tl, c), lambda bb, i: (bb, i, 0)),
        scratch_shapes=[pltpu.VMEM((tl + 2 * SUBLANE, c), F32)],
        compiler_params=_params("parallel", "parallel"),
        name="dwconv_silu",
    )(p3, p3, p3, conv_w8, conv_b)


def _ssd_kernel(xs_ref, bm_ref, cm_ref, misc_ref, h0_ref, tri_ref, e_ref, bias_ref, a_ref, *rest,
                reverse, finalize):
    if finalize:
        z_ref, y0_ref, dskip_ref, ng_ref, y_ref, hT_ref, st_ref, yb_ref = rest
    else:
        y_ref, hT_ref, st_ref, yb_ref = rest
    q = SSD_CHUNK
    gw = SSD_INNER // SSD_GROUPS
    hpg = SSD_HEADS // SSD_GROUPS

    @pl.when(pl.program_id(1) == 0)
    def _():
        st_ref[...] = h0_ref[0]

    xs = xs_ref[0]
    tri = tri_ref[...]
    visible = tri > 0.5
    lane = lax.broadcasted_iota(jnp.int32, (q, LANE), 1)
    dt_lanes = (lane >= DT_LANE0) & (lane < DT_LANE0 + SSD_HEADS)
    dt = jnp.where(dt_lanes, jax.nn.softplus(misc_ref[0] + bias_ref[...]), 0.0)
    da = dt * a_ref[...]
    cum = _split_dot(da, tri.astype(BF16), 3, left=True)
    cum_t = cum.T
    dt_t = dt.T
    edge = cum[0:1, :] if reverse else cum[q - 1:q, :]
    expcum = jnp.exp(cum)
    w_end = jnp.exp(edge - cum) * dt
    e = e_ref[...]
    expcum_x = _split_dot(expcum, e, 2)
    w_end_x = _split_dot(w_end, e, 2)
    xs_b = xs.astype(BF16)
    xw_b = (xs * w_end_x).astype(BF16)
    st = st_ref[...]
    st_b = st.astype(BF16)
    first_head = lane < SSD_HEAD_DIM
    decay_row = expcum_x[0:1, :] if reverse else expcum_x[q - 1:q, :]
    for g in range(SSD_GROUPS):
        bm = bm_ref[0][:, g * SSD_STATE:(g + 1) * SSD_STATE]
        cm_b = cm_ref[0][:, g * SSD_STATE:(g + 1) * SSD_STATE].astype(BF16)
        cb = _nt_dot(cm_b, bm.astype(BF16))
        gs = slice(g * gw, (g + 1) * gw)
        y_off = jnp.dot(cm_b, st_b[:, gs], preferred_element_type=F32) * expcum_x[:, gs]
        for jp in range(hpg // 2):
            c0 = (g * hpg + 2 * jp) * SSD_HEAD_DIM
            tiles = []
            for sub in range(2):
                col = DT_LANE0 + g * hpg + 2 * jp + sub
                seg = cum[:, col:col + 1] - cum_t[col:col + 1, :]
                lmat = jnp.exp(jnp.where(visible, seg, -jnp.inf))
                mm = (cb * lmat * dt_t[col:col + 1, :]).astype(BF16)
                tiles.append(jnp.dot(mm, xs_b[:, c0:c0 + LANE], preferred_element_type=F32))
            yb_ref[:, c0:c0 + LANE] = jnp.where(first_head, tiles[0], tiles[1]) + y_off[:, c0 - g * gw:c0 - g * gw + LANE]
        new = jnp.dot(bm.T.astype(BF16), xw_b[:, gs], preferred_element_type=F32)
        st_ref[:, gs] = st[:, gs] * decay_row[:, gs] + new

    if finalize:
        v = (yb_ref[...] + y0_ref[0] + dskip_ref[...] * xs) * _silu(z_ref[0])
        for g in range(SSD_GROUPS):
            gs = slice(g * gw, (g + 1) * gw)
            y_ref[0, :, gs] = _rms(v[:, gs], ng_ref[:, gs]).astype(y_ref.dtype)
    else:
        y_ref[0] = yb_ref[...]

    @pl.when(pl.program_id(1) == pl.num_programs(1) - 1)
    def _():
        hT_ref[0] = st_ref[...]


def _ssd_scan(xbc3, p3, h0, tri, expand, bias_row, a_row, reverse, fin=None):
    b, l, _ = xbc3.shape
    nc = l // SSD_CHUNK
    cidx = (lambda s: nc - 1 - s) if reverse else (lambda s: s)
    q = SSD_CHUNK
    in_specs = [
        pl.BlockSpec((1, q, SSD_INNER), lambda bb, s: (bb, cidx(s), 0)),
        pl.BlockSpec((1, q, SSD_BC), lambda bb, s: (bb, cidx(s), SSD_INNER // SSD_BC)),
        pl.BlockSpec((1, q, SSD_BC), lambda bb, s: (bb, cidx(s), SSD_INNER // SSD_BC + 1)),
        pl.BlockSpec((1, q, LANE), lambda bb, s: (bb, cidx(s), HYB_MISC0 // LANE)),
        pl.BlockSpec((1, SSD_STATE, SSD_INNER), lambda bb, s: (bb, 0, 0)),
        pl.BlockSpec((q, q), lambda bb, s: (0, 0)),
        pl.BlockSpec((LANE, SSD_INNER), lambda bb, s: (0, 0)),
        pl.BlockSpec((1, LANE), lambda bb, s: (0, 0)),
        pl.BlockSpec((1, LANE), lambda bb, s: (0, 0)),
    ]
    args = [xbc3, xbc3, xbc3, p3, h0, tri, expand, bias_row, a_row]
    if fin is not None:
        y0, dskip_row, normg_row = fin
        in_specs += [
            pl.BlockSpec((1, q, SSD_INNER), lambda bb, s: (bb, cidx(s), HYB_Z0 // SSD_INNER)),
            pl.BlockSpec((1, q, SSD_INNER), lambda bb, s: (bb, cidx(s), 0)),
            pl.BlockSpec((1, SSD_INNER), lambda bb, s: (0, 0)),
            pl.BlockSpec((1, SSD_INNER), lambda bb, s: (0, 0)),
        ]
        args += [p3, y0, dskip_row, normg_row]
    return pl.pallas_call(
        functools.partial(_ssd_kernel, reverse=reverse, finalize=fin is not None),
        out_shape=(jax.ShapeDtypeStruct((b, l, SSD_INNER), BF16 if fin is not None else F32),
                   jax.ShapeDtypeStruct((b, SSD_STATE, SSD_INNER), F32)),
        grid=(b, nc),
        in_specs=in_specs,
        out_specs=(pl.BlockSpec((1, q, SSD_INNER), lambda bb, s: (bb, cidx(s), 0)),
                   pl.BlockSpec((1, SSD_STATE, SSD_INNER), lambda bb, s: (bb, 0, 0))),
        scratch_shapes=[pltpu.VMEM((SSD_STATE, SSD_INNER), F32), pltpu.VMEM((q, SSD_INNER), F32)],
        compiler_params=_params("parallel", "arbitrary"),
        name="ssd_scan_bwd" if reverse else "ssd_scan_fwd",
    )(*args)


def _qproj_kernel(x_ref, g_ref, w_ref, cos_ref, sin_ref, o_ref):
    h = _rms(x_ref[...], g_ref[...]).astype(BF16)
    r = jnp.dot(h, w_ref[...], preferred_element_type=F32)
    main = MLA_HEADS * MLA_QK_PAD
    cos, sin = cos_ref[...], sin_ref[...]
    for hd in range(MLA_HEADS):
        c0 = hd * MLA_QK_PAD
        o_ref[:, c0:c0 + LANE] = r[:, c0:c0 + LANE].astype(o_ref.dtype)
        swapped = r[:, main + hd * LANE:main + (hd + 1) * LANE]
        o_ref[:, c0 + LANE:c0 + 2 * LANE] = (r[:, c0 + LANE:c0 + 2 * LANE] * cos + swapped * sin).astype(o_ref.dtype)


def _qproj(p, g, w, cos, sin, rows_per_batch, tm):
    m = p.shape[0]
    tm = min(tm, m, rows_per_batch)
    nb = rows_per_batch // tm
    n_out = MLA_HEADS * MLA_QK_PAD
    return pl.pallas_call(
        _qproj_kernel,
        out_shape=jax.ShapeDtypeStruct((m, n_out), BF16),
        grid=(m // tm,),
        in_specs=[
            pl.BlockSpec((tm, MLA_Q_RANK), lambda i: (i, HYB_CQ0 // MLA_Q_RANK)),
            pl.BlockSpec((1, MLA_Q_RANK), lambda i: (0, 0)),
            pl.BlockSpec(w.shape, lambda i: (0, 0)),
            pl.BlockSpec((tm, LANE), lambda i: (i % nb, 0)),
            pl.BlockSpec((tm, LANE), lambda i: (i % nb, 0)),
        ],
        out_specs=pl.BlockSpec((tm, n_out), lambda i: (i, 0)),
        compiler_params=_params("parallel"),
        name="mla_q_proj",
    )(p, g, w, cos, sin)


def _kvproj_kernel(x_ref, misc_ref, misc2_ref, g_ref, w_ref, cos_ref, sin_ref, k_ref, v_ref):
    h = _rms(x_ref[...], g_ref[...]).astype(BF16)
    r = jnp.dot(h, w_ref[...], preferred_element_type=F32)
    kr = (misc_ref[...] * cos_ref[...] + misc2_ref[...] * sin_ref[...]).astype(k_ref.dtype)
    for hd in range(MLA_HEADS):
        c0 = hd * MLA_QK_PAD
        k_ref[:, c0:c0 + LANE] = r[:, hd * MLA_NOPE:(hd + 1) * MLA_NOPE].astype(k_ref.dtype)
        k_ref[:, c0 + LANE:c0 + 2 * LANE] = kr
    v_ref[...] = r[:, MLA_HEADS * MLA_NOPE:].astype(v_ref.dtype)


def _kvproj(p, g, w, cos, sin, rows_per_batch, tm):
    m = p.shape[0]
    tm = min(tm, m, rows_per_batch)
    nb = rows_per_batch // tm
    nk, nv = MLA_HEADS * MLA_QK_PAD, MLA_HEADS * MLA_V
    return pl.pallas_call(
        _kvproj_kernel,
        out_shape=(jax.ShapeDtypeStruct((m, nk), BF16), jax.ShapeDtypeStruct((m, nv), BF16)),
        grid=(m // tm,),
        in_specs=[
            pl.BlockSpec((tm, MLA_KV_RANK), lambda i: (i, HYB_CKV0 // MLA_KV_RANK)),
            pl.BlockSpec((tm, LANE), lambda i: (i, HYB_MISC0 // LANE)),
            pl.BlockSpec((tm, LANE), lambda i: (i, HYB_MISC2 // LANE)),
            pl.BlockSpec((1, MLA_KV_RANK), lambda i: (0, 0)),
            pl.BlockSpec(w.shape, lambda i: (0, 0)),
            pl.BlockSpec((tm, LANE), lambda i: (i % nb, 0)),
            pl.BlockSpec((tm, LANE), lambda i: (i % nb, 0)),
        ],
        out_specs=(pl.BlockSpec((tm, nk), lambda i: (i, 0)), pl.BlockSpec((tm, nv), lambda i: (i, 0))),
        compiler_params=_params("parallel"),
        name="mla_kv_proj",
    )(p, p, p, g, w, cos, sin)


def _attn_kernel(q_ref, kc_ref, vc_ref, *rest, with_latent):
    if with_latent:
        k_ref, v_ref, o_ref, m_ref, l_ref, acc_ref = rest
    else:
        o_ref, m_ref, l_ref, acc_ref = rest
    ki = pl.program_id(2)

    def update(hd, kblk, vblk):
        s = _nt_dot(q_ref[:, hd * MLA_QK_PAD:(hd + 1) * MLA_QK_PAD], kblk)
        tiles = s.shape[1] // LANE
        m_prev = m_ref[hd]
        m_new = jnp.maximum(m_prev, jnp.max(s, axis=-1, keepdims=True))
        alpha = jnp.exp2((m_prev - m_new) * EXP2_SCALE)
        p = jnp.exp2((s - jnp.tile(m_new, (1, tiles))) * EXP2_SCALE)
        psum = p[:, 0:LANE]
        for c in range(1, tiles):
            psum = psum + p[:, c * LANE:(c + 1) * LANE]
        l_ref[hd] = alpha * l_ref[hd] + psum
        vs = slice(hd * MLA_V, (hd + 1) * MLA_V)
        acc_ref[:, vs] = alpha * acc_ref[:, vs] + jnp.dot(p.astype(BF16), vblk, preferred_element_type=F32)
        m_ref[hd] = m_new

    @pl.when(ki == 0)
    def _():
        m_ref[...] = jnp.full(m_ref.shape, -jnp.inf, F32)
        l_ref[...] = jnp.zeros(l_ref.shape, F32)
        acc_ref[...] = jnp.zeros(acc_ref.shape, F32)
        for hd in range(MLA_HEADS):
            update(hd, kc_ref[:, hd * MLA_QK_PAD:(hd + 1) * MLA_QK_PAD], vc_ref[:, hd * MLA_V:(hd + 1) * MLA_V])

    if with_latent:
        for hd in range(MLA_HEADS):
            update(hd, k_ref[:, hd * MLA_QK_PAD:(hd + 1) * MLA_QK_PAD], v_ref[:, hd * MLA_V:(hd + 1) * MLA_V])

    @pl.when(ki == pl.num_programs(2) - 1)
    def _():
        for hd in range(MLA_HEADS):
            vs = slice(hd * MLA_V, (hd + 1) * MLA_V)
            o_ref[:, vs] = (acc_ref[:, vs] / jnp.sum(l_ref[hd], axis=-1, keepdims=True)).astype(o_ref.dtype)


def _attention(q, kc, vc, k, v, batch, tq, tk):
    m = q.shape[0]
    lq = m // batch
    lc = kc.shape[0] // batch
    tq = min(tq, lq)
    nq = lq // tq
    nq_pad, nv = MLA_HEADS * MLA_QK_PAD, MLA_HEADS * MLA_V
    in_specs = [
        pl.BlockSpec((tq, nq_pad), lambda b, i, j: (b * nq + i, 0)),
        pl.BlockSpec((lc, nq_pad), lambda b, i, j: (b, 0)),
        pl.BlockSpec((lc, nv), lambda b, i, j: (b, 0)),
    ]
    args = [q, kc, vc]
    nk = 1
    if k is not None:
        lk = k.shape[0] // batch
        tk = min(tk, lk)
        nk = lk // tk
        in_specs += [
            pl.BlockSpec((tk, nq_pad), lambda b, i, j: (b * nk + j, 0)),
            pl.BlockSpec((tk, nv), lambda b, i, j: (b * nk + j, 0)),
        ]
        args += [k, v]
    return pl.pallas_call(
        functools.partial(_attn_kernel, with_latent=k is not None),
        out_shape=jax.ShapeDtypeStruct((m, nv), BF16),
        grid=(batch, nq, nk),
        in_specs=in_specs,
        out_specs=pl.BlockSpec((tq, nv), lambda b, i, j: (b * nq + i, 0)),
        scratch_shapes=[pltpu.VMEM((MLA_HEADS, tq, LANE), F32), pltpu.VMEM((MLA_HEADS, tq, LANE), F32),
                        pltpu.VMEM((tq, nv), F32)],
        compiler_params=_params("parallel", "parallel", "arbitrary"),
        name="mla_attention",
    )(*args)


def _outproj_kernel(a1_ref, a2_ref, w1_ref, w2_ref, x_ref, gate_ref, o_ref):
    acc = jnp.dot(a1_ref[...], w1_ref[...], preferred_element_type=F32)
    acc = acc + jnp.dot(a2_ref[...], w2_ref[...], preferred_element_type=F32)
    o_ref[...] = x_ref[...] + gate_ref[0] * acc


def _outproj(a1, a2, w1, w2, x, mod, k_gate, rows_per_batch, tm):
    m, d = x.shape
    tm = min(tm, m)
    bidx = lambda i: (i * tm) // rows_per_batch
    return pl.pallas_call(
        _outproj_kernel,
        out_shape=jax.ShapeDtypeStruct((m, d), F32),
        grid=(m // tm,),
        in_specs=[
            pl.BlockSpec((tm, a1.shape[1]), lambda i: (i, 0)),
            pl.BlockSpec((tm, a2.shape[1]), lambda i: (i, 0)),
            pl.BlockSpec(w1.shape, lambda i: (0, 0)),
            pl.BlockSpec(w2.shape, lambda i: (0, 0)),
            pl.BlockSpec((tm, d), lambda i: (i, 0)),
            pl.BlockSpec((1, 1, d), lambda i: (bidx(i) * N_MOD + k_gate, 0, 0)),
        ],
        out_specs=pl.BlockSpec((tm, d), lambda i: (i, 0)),
        compiler_params=_params("parallel"),
        name="hyb_out_proj",
    )(a1, a2, w1, w2, x, mod)


def _gmlp_kernel(x_ref, g_ref, sh_ref, sc_ref, gate_ref, win_ref, lng_ref, lnb_ref, ws_ref, bs_ref, wout_ref,
                 o_ref, uv_ref):
    x = x_ref[...]
    tm = x.shape[0]
    h = (_rms(x, g_ref[...]) * (1.0 + sc_ref[0]) + sh_ref[0]).astype(BF16)
    uv = _gelu(jnp.dot(h, win_ref[...], preferred_element_type=F32))
    u, v = uv[:, :GM_INNER], uv[:, GM_INNER:]
    mu = jnp.mean(v, axis=-1, keepdims=True)
    var = jnp.mean(jnp.square(v - mu), axis=-1, keepdims=True)
    vn = ((v - mu) * lax.rsqrt(var + EPS) * lng_ref[...] + lnb_ref[...]).astype(BF16)
    gw = GM_INNER // GM_GROUPS
    for c in range(tm // GM_CHUNK):
        rs = slice(c * GM_CHUNK, (c + 1) * GM_CHUNK)
        for g in range(GM_GROUPS):
            cs = slice(g * gw, (g + 1) * gw)
            sv = jnp.dot(ws_ref[g], vn[rs, cs], preferred_element_type=F32) + bs_ref[:, cs]
            uv_ref[rs, cs] = (u[rs, cs] * sv).astype(BF16)
    o_ref[...] = x + gate_ref[0] * jnp.dot(uv_ref[...], wout_ref[...], preferred_element_type=F32)


def _gmlp(x, g, mod, rows_per_batch, w_in, ln_g, ln_b, ws, bs_full, w_out, tm):
    m, d = x.shape
    tm = min(tm, m)
    bidx = lambda i: (i * tm) // rows_per_batch
    const = lambda shape: pl.BlockSpec(shape, lambda i: (0,) * len(shape))
    modspec = lambda k: pl.BlockSpec((1, 1, d), lambda i: (bidx(i) * N_MOD + k, 0, 0))
    return pl.pallas_call(
        _gmlp_kernel,
        out_shape=jax.ShapeDtypeStruct((m, d), F32),
        grid=(m // tm,),
        in_specs=[
            pl.BlockSpec((tm, d), lambda i: (i, 0)), const((1, d)), modspec(0), modspec(1), modspec(2),
            const(w_in.shape), const((1, GM_INNER)), const((1, GM_INNER)), const(ws.shape), const(bs_full.shape),
            const(w_out.shape),
        ],
        out_specs=pl.BlockSpec((tm, d), lambda i: (i, 0)),
        scratch_shapes=[pltpu.VMEM((tm, GM_INNER), BF16)],
        compiler_params=_params("parallel"),
        name="chunk_gmlp",
    )(x, g, mod, mod, mod, w_in, ln_g, ln_b, ws, bs_full, w_out)


def _top_values(xs, k, want_rank):
    xs = list(xs)
    vals = [[] for _ in xs]
    ranks = [jnp.full(x.shape, float(k), F32) if w else None for x, w in zip(xs, want_rank)]
    for r in range(k):
        for i, x in enumerate(xs):
            mx = jnp.max(x, axis=0, keepdims=True)
            vals[i].append(mx)
            hit = x == mx
            if want_rank[i]:
                ranks[i] = jnp.where(hit, float(r), ranks[i])
            if r + 1 < k:
                xs[i] = jnp.where(hit, -jnp.inf, x)
    return vals, ranks


def _peer_stats_kernel(x_ref, g_ref, sh_ref, sc_ref, wq_ref, k1_ref, k2_ref, t_ref, a_ref, n_ref, br_ref,
                       q_ref, cand_ref):
    h = (_rms(x_ref[...], g_ref[...]) * (1.0 + sc_ref[0]) + sh_ref[0]).astype(BF16)
    t_ref[...] = h
    q_ref[...] = jnp.dot(h, wq_ref[...], preferred_element_type=F32).astype(BF16)
    half = PEER_DKEY // 2
    tm = h.shape[0]
    n_pairs = len(PEER_PAIRS)
    cand_ref[n_pairs:, :] = jnp.full((cand_ref.shape[0] - n_pairs, tm), -jnp.inf, F32)

    def head(hd, carry):
        c0 = pl.multiple_of(hd * PEER_DKEY, PEER_DKEY)
        s1 = _nt_dot(k1_ref[hd], q_ref[:, pl.ds(c0, half)])
        s2 = _nt_dot(k2_ref[hd], q_ref[:, pl.ds(c0 + half, half)])
        (v1, v2), (_, rank2) = _top_values((s1, s2), PEER_TOPK, (False, True))
        for idx, (a, b) in enumerate(PEER_PAIRS):
            cand_ref[idx:idx + 1, :] = v1[a] + v2[b]
        (tops,), _ = _top_values((cand_ref[...],), PEER_TOPK, (False,))
        thr = tops[PEER_TOPK - 1]
        z = tops[0] * 0.0
        for tv in tops:
            z = z + jnp.exp(tv - tops[0])
        n = jnp.zeros(s1.shape, F32)
        for vb in v2:
            n = n + jnp.where(s1 + vb >= thr, 1.0, 0.0)
        a_ref[hd] = jnp.exp(s1 - v1[0]) * (0.5 / z)
        n_ref[hd] = n
        bvals = jnp.exp(s2 - v2[0])
        for c in range(tm // LANE):
            br_ref[hd, c, :, 0:LANE] = bvals[:, c * LANE:(c + 1) * LANE]
            br_ref[hd, c, :, LANE:2 * LANE] = rank2[:, c * LANE:(c + 1) * LANE]
        return carry

    lax.fori_loop(0, PEER_HEADS, head, 0)


def _peer_stats(x, g, mod, rows_per_batch, wq, k1, k2, tm):
    m, d = x.shape
    tm = min(tm, m)
    bidx = lambda i: (i * tm) // rows_per_batch
    const = lambda shape: pl.BlockSpec(shape, lambda i: (0,) * len(shape))
    modspec = lambda k: pl.BlockSpec((1, 1, d), lambda i: (bidx(i) * N_MOD + k, 0, 0))
    return pl.pallas_call(
        _peer_stats_kernel,
        out_shape=(jax.ShapeDtypeStruct((m, d), BF16),
                   jax.ShapeDtypeStruct((PEER_HEADS, PEER_NKEYS, m), F32),
                   jax.ShapeDtypeStruct((PEER_HEADS, PEER_NKEYS, m), F32),
                   jax.ShapeDtypeStruct((PEER_HEADS, m // LANE, PEER_NKEYS, 2 * LANE), F32)),
        grid=(m // tm,),
        in_specs=[pl.BlockSpec((tm, d), lambda i: (i, 0)), const((1, d)), modspec(3), modspec(4),
                  const(wq.shape), const(k1.shape), const(k2.shape)],
        out_specs=(pl.BlockSpec((tm, d), lambda i: (i, 0)),) + tuple(
            pl.BlockSpec((PEER_HEADS, PEER_NKEYS, tm), lambda i: (0, 0, i)) for _ in range(2)) + (
            pl.BlockSpec((PEER_HEADS, tm // LANE, PEER_NKEYS, 2 * LANE), lambda i: (0, i, 0, 0)),),
        scratch_shapes=[pltpu.VMEM((tm, PEER_HEADS * PEER_DKEY), BF16),
                        pltpu.VMEM((-(-len(PEER_PAIRS) // SUBLANE) * SUBLANE, tm), F32)],
        compiler_params=_params("parallel"),
        name="peer_topk",
    )(x, g, mod, mod, wq, k1, k2)


def _peer_expert_kernel(t_ref, u_ref, vt_ref, a_ref, n_ref, br_ref, x_ref, gate_ref, o_ref,
                        acc_ref, s_ref, g_ref, *, te):
    j = pl.program_id(1)
    d, tm = acc_ref.shape
    rows = te // PEER_NKEYS
    sub = PEER_NKEYS // 4
    group = 2

    @pl.when(j == 0)
    def _():
        acc_ref[...] = jnp.zeros(acc_ref.shape, F32)

    halves = [slice(h0, h0 + MXU_COLS) for h0 in range(0, tm, MXU_COLS)]
    for hs in halves:
        s_ref[:, hs] = _nt_dot(u_ref[...], t_ref[hs, :])
    subs = [slice(k0, k0 + sub) for k0 in range(0, PEER_NKEYS, sub)]
    for c in range(tm // LANE):
        cs = slice(c * LANE, (c + 1) * LANE)
        for r0 in range(0, rows, group):
            w = [[None] * len(subs) for _ in range(group)]
            for hd in range(PEER_HEADS):
                a_rows = [a_ref[hd, r0 + q:r0 + q + 1, cs] for q in range(group)]
                n_rows = [n_ref[hd, r0 + q:r0 + q + 1, cs] for q in range(group)]
                for k, ks in enumerate(subs):
                    bv, rk = br_ref[hd, c, ks, 0:LANE], br_ref[hd, c, ks, LANE:2 * LANE]
                    for q in range(group):
                        term = a_rows[q] * jnp.where(rk < n_rows[q], bv, 0.0)
                        w[q][k] = term if w[q][k] is None else w[q][k] + term
            for q in range(group):
                for k, ks in enumerate(subs):
                    es = slice((r0 + q) * PEER_NKEYS + ks.start, (r0 + q) * PEER_NKEYS + ks.stop)
                    sc = s_ref[es, cs]
                    g_ref[es, cs] = (sc * (1.0 + lax.erf(sc * (2.0 ** -0.5))) * w[q][k]).astype(BF16)
    for hs in halves:
        acc_ref[:, hs] += jnp.dot(vt_ref[0], g_ref[:, hs], preferred_element_type=F32)

    @pl.when(j == pl.num_programs(1) - 1)
    def _():
        o_ref[...] = x_ref[...] + gate_ref[0] * acc_ref[...].T


def _peer_experts(t, u, vt, a, cnt, br, x, mod, k_gate, rows_per_batch, tm, te):
    m, d = x.shape
    tm = min(tm, m)
    ne = u.shape[0]
    bidx = lambda i: (i * tm) // rows_per_batch
    rows = te // PEER_NKEYS
    return pl.pallas_call(
        functools.partial(_peer_expert_kernel, te=te),
        out_shape=jax.ShapeDtypeStruct((m, d), F32),
        grid=(m // tm, ne // te),
        in_specs=[
            pl.BlockSpec((tm, d), lambda i, j: (i, 0)),
            pl.BlockSpec((te, d), lambda i, j: (j, 0)),
            pl.BlockSpec((1, d, te), lambda i, j: (j, 0, 0)),
            pl.BlockSpec((PEER_HEADS, rows, tm), lambda i, j: (0, j, i)),
            pl.BlockSpec((PEER_HEADS, rows, tm), lambda i, j: (0, j, i)),
            pl.BlockSpec((PEER_HEADS, tm // LANE, PEER_NKEYS, 2 * LANE), lambda i, j: (0, i, 0, 0)),
            pl.BlockSpec((tm, d), lambda i, j: (i, 0)),
            pl.BlockSpec((1, 1, d), lambda i, j: (bidx(i) * N_MOD + k_gate, 0, 0)),
        ],
        out_specs=pl.BlockSpec((tm, d), lambda i, j: (i, 0)),
        scratch_shapes=[pltpu.VMEM((d, tm), F32), pltpu.VMEM((te, tm), F32), pltpu.VMEM((te, tm), BF16)],
        compiler_params=_params("parallel", "arbitrary"),
        name="peer_experts",
    )(t, u, vt, a, cnt, br, x, mod)


def _final_norm_kernel(x_ref, g_ref, o_ref):
    o_ref[...] = _rms(x_ref[...], g_ref[...])


def _final_norm(x, g, tm):
    m, d = x.shape
    tm = min(tm, m)
    return pl.pallas_call(
        _final_norm_kernel,
        out_shape=jax.ShapeDtypeStruct((m, d), F32),
        grid=(m // tm,),
        in_specs=[pl.BlockSpec((tm, d), lambda i: (i, 0)), pl.BlockSpec((1, d), lambda i: (0, 0))],
        out_specs=pl.BlockSpec((tm, d), lambda i: (i, 0)),
        compiler_params=_params("parallel"),
        name="final_rmsnorm",
    )(x, g)


def _rope_tables(n_lat, n_ctx):
    rows = n_lat // GRID_W
    row = jnp.broadcast_to(jnp.arange(rows, dtype=F32)[:, None], (rows, GRID_W)).reshape(-1)
    col = jnp.broadcast_to(jnp.arange(GRID_W, dtype=F32)[None, :], (rows, GRID_W)).reshape(-1)
    n_freq = MLA_ROPE // 4
    inv = ROPE_BASE ** (-jnp.arange(n_freq, dtype=F32) / n_freq)
    ang = jnp.concatenate([row[:, None] * inv, col[:, None] * inv], axis=-1)
    cos, sin = jnp.cos(ang), jnp.sin(ang)
    pad = jnp.zeros((n_lat, LANE - MLA_ROPE), F32)
    cos_l = jnp.concatenate([cos, cos, pad], axis=-1)
    sin_l = jnp.concatenate([-sin, sin, pad], axis=-1)
    cos_c = jnp.concatenate([jnp.ones((n_ctx, MLA_ROPE), F32), jnp.zeros((n_ctx, LANE - MLA_ROPE), F32)], axis=-1)
    return (cos_l, sin_l), (cos_c, jnp.zeros((n_ctx, LANE), F32))


def _swap_halves(w):
    half = w.shape[-1] // 2
    return jnp.concatenate([w[..., half:], w[..., :half]], axis=-1)


def _prep_hybrid(w_in, conv_w, conv_b, a_log, dt_bias, d_skip, norm_g, q_norm_g, w_qb, kv_norm_g, w_kvb, w_out):
    d = w_in.shape[0]
    o = 0
    pieces = {}
    for name, width in (("z", SSD_INNER), ("xbc", SSD_CONV_DIM), ("dt", SSD_HEADS), ("cq", MLA_Q_RANK),
                        ("ckv", MLA_KV_RANK), ("kr", MLA_ROPE)):
        pieces[name] = w_in[:, o:o + width]
        o += width
    zeros = lambda n: jnp.zeros((d, n), w_in.dtype)
    w_hyb = jnp.concatenate([
        pieces["z"], pieces["kr"], pieces["dt"], zeros(LANE - MLA_ROPE - SSD_HEADS), pieces["cq"], pieces["xbc"],
        pieces["ckv"], _swap_halves(pieces["kr"]), zeros(LANE - MLA_ROPE)], axis=1).astype(BF16)
    conv_w8 = jnp.concatenate([conv_w.T, jnp.zeros((SUBLANE - SSD_CONV, SSD_CONV_DIM), F32)], axis=0)
    lane_row = lambda v: jnp.zeros((1, LANE), F32).at[0, DT_LANE0:DT_LANE0 + SSD_HEADS].set(v)
    dirs = [(lane_row(dt_bias[k]), lane_row(-jnp.exp(a_log[k]))) for k in range(2)]
    expand = jnp.zeros((LANE, SSD_INNER), F32).at[DT_LANE0:DT_LANE0 + SSD_HEADS].set(
        jnp.repeat(jnp.eye(SSD_HEADS, dtype=F32), SSD_HEAD_DIM, axis=1)).astype(BF16)
    wq3 = w_qb.reshape(MLA_Q_RANK, MLA_HEADS, MLA_NOPE + MLA_ROPE)
    zq = jnp.zeros((MLA_Q_RANK, MLA_HEADS, LANE - MLA_ROPE), w_qb.dtype)
    q_main = jnp.concatenate([wq3, zq], axis=-1).reshape(MLA_Q_RANK, MLA_HEADS * MLA_QK_PAD)
    q_swap = jnp.concatenate([_swap_halves(wq3[..., MLA_NOPE:]), zq], axis=-1).reshape(MLA_Q_RANK, MLA_HEADS * LANE)
    wq = jnp.concatenate([q_main, q_swap], axis=1).astype(BF16)
    wkv3 = w_kvb.reshape(MLA_KV_RANK, MLA_HEADS, MLA_NOPE + MLA_V)
    wkv = jnp.concatenate([wkv3[..., :MLA_NOPE].reshape(MLA_KV_RANK, -1), wkv3[..., MLA_NOPE:].reshape(MLA_KV_RANK, -1)],
                          axis=1).astype(BF16)
    return dict(
        w_hyb=w_hyb, conv_w8=conv_w8, conv_b=conv_b[None, :], dirs=dirs, expand=expand,
        dskip=jnp.repeat(d_skip, SSD_HEAD_DIM)[None, :], normg=norm_g[None, :],
        q_norm_g=q_norm_g[None, :], wq=wq, kv_norm_g=kv_norm_g[None, :], wkv=wkv,
        w_out1=w_out[:SSD_INNER].astype(BF16), w_out2=w_out[SSD_INNER:].astype(BF16))


def _hybrid_layer(xl, xc, batch, g1, mod_l, mod_c, hp, ropes, update_ctx):
    n_lat, n_ctx = xl.shape[0] // batch, xc.shape[0] // batch
    (cos_l, sin_l), (cos_c, sin_c) = ropes
    q = SSD_CHUNK
    tri_f = jnp.tril(jnp.ones((q, q), F32))
    tri_b = jnp.triu(jnp.ones((q, q), F32))
    pl_ = _nmm(xl, g1, mod_l, 0, 1, n_lat, hp["w_hyb"], 512, HYB_COLS // 3)
    pc_ = _nmm(xc, g1, mod_c, 0, 1, xc.shape[0], hp["w_hyb"], 256, HYB_COLS // 3)
    pl3, pc3 = pl_.reshape(batch, n_lat, HYB_COLS), pc_.reshape(batch, n_ctx, HYB_COLS)
    xbc_l = _conv_silu(pl3, hp["conv_w8"], hp["conv_b"], 512)
    xbc_c = _conv_silu(pc3, hp["conv_w8"], hp["conv_b"], 512)
    h0 = jnp.zeros((batch, SSD_STATE, SSD_INNER), F32)
    (bias0, a0), (bias1, a1) = hp["dirs"]
    yc0, hc0 = _ssd_scan(xbc_c, pc3, h0, tri_f, hp["expand"], bias0, a0, False)
    yl0, _ = _ssd_scan(xbc_l, pl3, hc0, tri_f, hp["expand"], bias0, a0, False)
    fin = lambda y0: (y0, hp["dskip"], hp["normg"])
    sc, hc1 = _ssd_scan(xbc_c, pc3, h0, tri_b, hp["expand"], bias1, a1, True, fin(yc0))
    sl, _ = _ssd_scan(xbc_l, pl3, hc1, tri_b, hp["expand"], bias1, a1, True, fin(yl0))
    ql = _qproj(pl_, hp["q_norm_g"], hp["wq"], cos_l, sin_l, n_lat, 512)
    kl, vl = _kvproj(pl_, hp["kv_norm_g"], hp["wkv"], cos_l, sin_l, n_lat, 512)
    kc, vc = _kvproj(pc_, hp["kv_norm_g"], hp["wkv"], cos_c, sin_c, n_ctx, 256)
    att_l = _attention(ql, kc, vc, kl, vl, batch, 512, 1024)
    xl_new = _outproj(sl.reshape(-1, SSD_INNER), att_l, hp["w_out1"], hp["w_out2"], xl, mod_l, 2, n_lat, 512)
    xc_new = xc
    if update_ctx:
        qc = _qproj(pc_, hp["q_norm_g"], hp["wq"], cos_c, sin_c, n_ctx, 256)
        att_c = _attention(qc, kc, vc, None, None, batch, 256, 256)
        xc_new = _outproj(sc.reshape(-1, SSD_INNER), att_c, hp["w_out1"], hp["w_out2"], xc, mod_c, 2, xc.shape[0], 256)
    return xl_new, xc_new


def _peer_layer(x, g2, mod, rows_per_batch, pp):
    t, a, cnt, br = _peer_stats(x, g2, mod, rows_per_batch, pp["wq"], pp["k1"], pp["k2"], 256)
    return _peer_experts(t, pp["u"], pp["vt"], a, cnt, br, x, mod, 5, rows_per_batch, PEER_TM, PEER_TE)


def kernel(x, c, ctx, c_ctx, ada_w, ada_b, norm1_g, norm2_g, hyb_w_in, ssd_conv_w, ssd_conv_b, ssd_a_log,
           ssd_dt_bias, ssd_d, ssd_norm_g, mla_q_norm_g, mla_w_qb, mla_kv_norm_g, mla_w_kvb, hyb_w_out, gm_w_in,
           gm_ln_g, gm_ln_b, gm_ws, gm_bs, gm_w_out, peer_wq, peer_k1, peer_k2, peer_u, peer_v, final_norm_g):
    batch, n_lat, d = x.shape
    n_ctx = ctx.shape[1]
    depth = ada_w.shape[0]
    xl = x.reshape(batch * n_lat, d)
    xc = ctx.reshape(batch * n_ctx, d)
    n_cond = -(-(batch + 1) // SUBLANE) * SUBLANE
    conds = jnp.zeros((n_cond, d), F32).at[:batch].set(c).at[batch].set(c_ctx)
    mods = _mods(conds, ada_w, ada_b)
    ropes = _rope_tables(n_lat, n_ctx)
    for layer in range(depth):
        i = layer // 2
        even = layer % 2 == 0
        keep_ctx = any(j % 2 == 0 for j in range(layer + 1, depth))
        mod_l = mods[layer, :batch].reshape(batch * N_MOD, 1, d)
        mod_c = mods[layer, batch].reshape(N_MOD, 1, d)
        g1, g2 = norm1_g[layer][None, :], norm2_g[layer][None, :]
        if even:
            hp = _prep_hybrid(hyb_w_in[i], ssd_conv_w[i], ssd_conv_b[i], ssd_a_log[i], ssd_dt_bias[i], ssd_d[i],
                              ssd_norm_g[i], mla_q_norm_g[i], mla_w_qb[i], mla_kv_norm_g[i], mla_w_kvb[i], hyb_w_out[i])
            xl, xc = _hybrid_layer(xl, xc, batch, g1, mod_l, mod_c, hp, ropes, keep_ctx)
        else:
            gp = (gm_w_in[i].astype(BF16), gm_ln_g[i][None, :], gm_ln_b[i][None, :], gm_ws[i].astype(BF16),
                  jnp.repeat(gm_bs[i].T, GM_INNER // GM_GROUPS, axis=1), gm_w_out[i].astype(BF16))
            xl = _gmlp(xl, g1, mod_l, n_lat, *gp, 256)
            if keep_ctx:
                xc = _gmlp(xc, g1, mod_c, xc.shape[0], *gp, 256)
        pp = dict(wq=peer_wq[layer].astype(BF16), k1=peer_k1[layer].astype(BF16), k2=peer_k2[layer].astype(BF16),
                  u=peer_u[layer].astype(BF16),
                  vt=jnp.transpose(peer_v[layer].astype(BF16).reshape(-1, PEER_TE, d), (0, 2, 1)))
        xl = _peer_layer(xl, g2, mod_l, n_lat, pp)
        if keep_ctx:
            xc = _peer_layer(xc, g2, mod_c, xc.shape[0], pp)
    return _final_norm(xl, final_norm_g[None, :], 512).reshape(batch, n_lat, d)
```

```python
import functools

import jax
import jax.numpy as jnp
from jax import lax
from jax.experimental import pallas as pl
from jax.experimental.pallas import tpu as pltpu

F32, BF16 = jnp.float32, jnp.bfloat16
EPS = 1e-6
LANE = 128
SUBLANE = 8
MXU_COLS = 256
VMEM_LIMIT = 56 * 1024 * 1024

D_MODEL = 1024
DEPTH = 4
GRID_W = 64
N_MOD = 6
SSD_HEADS = 16
SSD_HEAD_DIM = 64
SSD_INNER = SSD_HEADS * SSD_HEAD_DIM
SSD_GROUPS = 2
SSD_STATE = 128
SSD_BC = SSD_GROUPS * SSD_STATE
SSD_CONV = 5
SSD_CONV_DIM = SSD_INNER + 2 * SSD_BC
SSD_CHUNK = 128
MLA_HEADS = 8
MLA_Q_RANK = 384
MLA_KV_RANK = 256
MLA_NOPE = 128
MLA_ROPE = 64
MLA_V = 128
MLA_SCALE = (MLA_NOPE + MLA_ROPE) ** -0.5
EXP2_SCALE = MLA_SCALE * 1.4426950408889634
ROPE_BASE = 10000.0
GM_CHUNK = 128
GM_INNER = 2 * D_MODEL
GM_GROUPS = 8
PEER_HEADS = 8
PEER_NKEYS = 128
PEER_EXPERTS = PEER_NKEYS * PEER_NKEYS
PEER_DKEY = 256
PEER_TOPK = 16
PEER_TE = 1024
PEER_TM = 512
PEER_STATS_TM = 256

HYB_Z0 = 0
HYB_MISC0 = 1024
HYB_CQ0 = 1152
HYB_XBC0 = 1536
HYB_CKV0 = 3072
HYB_MISC2 = 3328
HYB_COLS = 3456
DT_LANE0 = MLA_ROPE
MLA_QK_PAD = 256

PEER_PAIRS = tuple((a, b) for a in range(PEER_TOPK) for b in range(PEER_TOPK) if (a + 1) * (b + 1) <= PEER_TOPK)


def _params(*sem, flags=None):
    return pltpu.CompilerParams(dimension_semantics=sem, vmem_limit_bytes=VMEM_LIMIT, flags=flags)


def _rms(x, g):
    return x * lax.rsqrt(jnp.mean(x * x, axis=-1, keepdims=True) + EPS) * g


def _silu(x):
    return x * jax.nn.sigmoid(x)


def _gelu(x):
    return 0.5 * x * (1.0 + lax.erf(x * (2.0 ** -0.5)))


def _nt_dot(a, b):
    return lax.dot_general(a, b, (((1,), (1,)), ((), ())), preferred_element_type=F32)


def _split_dot(x, w, terms, left=False):
    acc = None
    rem = x
    for _ in range(terms):
        hi = rem.astype(BF16)
        part = jnp.dot(w, hi, preferred_element_type=F32) if left else jnp.dot(hi, w, preferred_element_type=F32)
        acc = part if acc is None else acc + part
        rem = rem - hi.astype(F32)
    return acc


def _mods_kernel(c_ref, w_ref, b_ref, o_ref):
    h = _silu(c_ref[...]).astype(BF16)
    o_ref[0] = jnp.dot(h, w_ref[0].astype(BF16), preferred_element_type=F32) + b_ref[0]


def _mods(conds, ada_w, ada_b):
    depth, d, n = ada_w.shape
    tn = 1536
    return pl.pallas_call(
        _mods_kernel,
        out_shape=jax.ShapeDtypeStruct((depth, conds.shape[0], n), F32),
        grid=(depth, n // tn),
        in_specs=[
            pl.BlockSpec(conds.shape, lambda l, j: (0, 0)),
            pl.BlockSpec((1, d, tn), lambda l, j: (l, 0, j)),
            pl.BlockSpec((1, 1, tn), lambda l, j: (l, 0, j)),
        ],
        out_specs=pl.BlockSpec((1, conds.shape[0], tn), lambda l, j: (l, 0, j)),
        compiler_params=_params("parallel", "parallel"),
        name="ada_mods",
    )(conds, ada_w, ada_b.reshape(depth, 1, n))


def _nmm_kernel(x_ref, g_ref, sh_ref, sc_ref, w_ref, o_ref, h_ref):
    @pl.when(pl.program_id(1) == 0)
    def _():
        y = _rms(x_ref[...], g_ref[...])
        h_ref[...] = (y * (1.0 + sc_ref[0]) + sh_ref[0]).astype(BF16)

    o_ref[...] = jnp.dot(h_ref[...], w_ref[...], preferred_element_type=F32).astype(o_ref.dtype)


def _nmm(x, g, mod, k_shift, k_scale, rows_per_batch, w, tm, tn):
    m, k = x.shape
    n = w.shape[1]
    tm = min(tm, m)
    bidx = lambda i: (i * tm) // rows_per_batch
    return pl.pallas_call(
        _nmm_kernel,
        out_shape=jax.ShapeDtypeStruct((m, n), F32),
        grid=(m // tm, n // tn),
        in_specs=[
            pl.BlockSpec((tm, k), lambda i, j: (i, 0)),
            pl.BlockSpec((1, k), lambda i, j: (0, 0)),
            pl.BlockSpec((1, 1, k), lambda i, j: (bidx(i) * N_MOD + k_shift, 0, 0)),
            pl.BlockSpec((1, 1, k), lambda i, j: (bidx(i) * N_MOD + k_scale, 0, 0)),
            pl.BlockSpec((k, tn), lambda i, j: (0, j)),
        ],
        out_specs=pl.BlockSpec((tm, tn), lambda i, j: (i, j)),
        scratch_shapes=[pltpu.VMEM((tm, k), BF16)],
        compiler_params=_params("parallel", "arbitrary"),
        name="norm_mod_matmul",
    )(x, g, mod, mod, w)


def _conv_kernel(cur_ref, prev_ref, next_ref, w_ref, b_ref, o_ref, buf_ref, *, tl):
    i = pl.program_id(1)
    halo = SUBLANE
    buf_ref[0:halo] = jnp.where(i > 0, prev_ref[0], 0.0)
    buf_ref[halo:halo + tl] = cur_ref[0]
    buf_ref[halo + tl:2 * halo + tl] = jnp.where(i < pl.num_programs(1) - 1, next_ref[0], 0.0)
    acc = jnp.broadcast_to(b_ref[...], (tl, b_ref.shape[1]))
    for k in range(SSD_CONV):
        start = halo + k - SSD_CONV // 2
        acc = acc + w_ref[k:k + 1, :] * buf_ref[start:start + tl, :]
    o_ref[0] = _silu(acc)


def _conv_silu(p3, conv_w8, conv_b, tl):
    b, l, _ = p3.shape
    c = SSD_CONV_DIM
    tl = min(tl, l)
    cb = HYB_XBC0 // c
    nh = l // SUBLANE
    r = tl // SUBLANE
    return pl.pallas_call(
        functools.partial(_conv_kernel, tl=tl),
        out_shape=jax.ShapeDtypeStruct((b, l, c), F32),
        grid=(b, l // tl),
        in_specs=[
            pl.BlockSpec((1, tl, c), lambda bb, i: (bb, i, cb)),
            pl.BlockSpec((1, SUBLANE, c), lambda bb, i: (bb, jnp.maximum(i * r - 1, 0), cb)),
            pl.BlockSpec((1, SUBLANE, c), lambda bb, i: (bb, jnp.minimum((i + 1) * r, nh - 1), cb)),
            pl.BlockSpec((SUBLANE, c), lambda bb, i: (0, 0)),
            pl.BlockSpec((1, c), lambda bb, i: (0, 0)),
        ],
        out_specs=pl.BlockSpec((1, tl, c), lambda bb, i: (bb, i, 0)),
        scratch_shapes=[pltpu.VMEM((tl + 2 * SUBLANE, c), F32)],
        compiler_params=_params("parallel", "parallel"),
        name="dwconv_silu",
    )(p3, p3, p3, conv_w8, conv_b)


def _ssd_kernel(xs_ref, bm_ref, cm_ref, misc_ref, h0_ref, tri_ref, e_ref, bias_ref, a_ref, *rest,
                reverse, finalize):
    if finalize:
        z_ref, y0_ref, dskip_ref, ng_ref, y_ref, hT_ref, st_ref, yb_ref = rest
    else:
        y_ref, hT_ref, st_ref, yb_ref = rest
    q = SSD_CHUNK
    gw = SSD_INNER // SSD_GROUPS
    hpg = SSD_HEADS // SSD_GROUPS

    @pl.when(pl.program_id(1) == 0)
    def _():
        st_ref[...] = h0_ref[0]

    xs = xs_ref[0]
    tri = tri_ref[...]
    visible = tri > 0.5
    lane = lax.broadcasted_iota(jnp.int32, (q, LANE), 1)
    dt_lanes = (lane >= DT_LANE0) & (lane < DT_LANE0 + SSD_HEADS)
    dt = jnp.where(dt_lanes, jax.nn.softplus(misc_ref[0] + bias_ref[...]), 0.0)
    da = dt * a_ref[...]
    cum = _split_dot(da, tri.astype(BF16), 3, left=True)
    cum_t = cum.T
    dt_t = dt.T
    edge = cum[0:1, :] if reverse else cum[q - 1:q, :]
    expcum = jnp.exp(cum)
    w_end = jnp.exp(edge - cum) * dt
    e = e_ref[...]
    expcum_x = _split_dot(expcum, e, 2)
    w_end_x = _split_dot(w_end, e, 2)
    xs_b = xs.astype(BF16)
    xw_b = (xs * w_end_x).astype(BF16)
    st = st_ref[...]
    st_b = st.astype(BF16)
    first_head = lane < SSD_HEAD_DIM
    decay_row = expcum_x[0:1, :] if reverse else expcum_x[q - 1:q, :]
    for g in range(SSD_GROUPS):
        bm = bm_ref[0][:, g * SSD_STATE:(g + 1) * SSD_STATE]
        cm_b = cm_ref[0][:, g * SSD_STATE:(g + 1) * SSD_STATE].astype(BF16)
        cb = _nt_dot(cm_b, bm.astype(BF16))
        gs = slice(g * gw, (g + 1) * gw)
        y_off = jnp.dot(cm_b, st_b[:, gs], preferred_element_type=F32) * expcum_x[:, gs]
        for jp in range(hpg // 2):
            c0 = (g * hpg + 2 * jp) * SSD_HEAD_DIM
            tiles = []
            for sub in range(2):
                col = DT_LANE0 + g * hpg + 2 * jp + sub
                seg = cum[:, col:col + 1] - cum_t[col:col + 1, :]
                lmat = jnp.exp(jnp.where(visible, seg, -jnp.inf))
                mm = (cb * lmat * dt_t[col:col + 1, :]).astype(BF16)
                tiles.append(jnp.dot(mm, xs_b[:, c0:c0 + LANE], preferred_element_type=F32))
            yb_ref[:, c0:c0 + LANE] = jnp.where(first_head, tiles[0], tiles[1]) + y_off[:, c0 - g * gw:c0 - g * gw + LANE]
        new = jnp.dot(bm.T.astype(BF16), xw_b[:, gs], preferred_element_type=F32)
        st_ref[:, gs] = st[:, gs] * decay_row[:, gs] + new

    if finalize:
        v = (yb_ref[...] + y0_ref[0] + dskip_ref[...] * xs) * _silu(z_ref[0])
        for g in range(SSD_GROUPS):
            gs = slice(g * gw, (g + 1) * gw)
            y_ref[0, :, gs] = _rms(v[:, gs], ng_ref[:, gs]).astype(y_ref.dtype)
    else:
        y_ref[0] = yb_ref[...]

    @pl.when(pl.program_id(1) == pl.num_programs(1) - 1)
    def _():
        hT_ref[0] = st_ref[...]


def _ssd_scan(xbc3, p3, h0, tri, expand, bias_row, a_row, reverse, fin=None):
    b, l, _ = xbc3.shape
    nc = l // SSD_CHUNK
    cidx = (lambda s: nc - 1 - s) if reverse else (lambda s: s)
    q = SSD_CHUNK
    in_specs = [
        pl.BlockSpec((1, q, SSD_INNER), lambda bb, s: (bb, cidx(s), 0)),
        pl.BlockSpec((1, q, SSD_BC), lambda bb, s: (bb, cidx(s), SSD_INNER // SSD_BC)),
        pl.BlockSpec((1, q, SSD_BC), lambda bb, s: (bb, cidx(s), SSD_INNER // SSD_BC + 1)),
        pl.BlockSpec((1, q, LANE), lambda bb, s: (bb, cidx(s), HYB_MISC0 // LANE)),
        pl.BlockSpec((1, SSD_STATE, SSD_INNER), lambda bb, s: (bb, 0, 0)),
        pl.BlockSpec((q, q), lambda bb, s: (0, 0)),
        pl.BlockSpec((LANE, SSD_INNER), lambda bb, s: (0, 0)),
        pl.BlockSpec((1, LANE), lambda bb, s: (0, 0)),
        pl.BlockSpec((1, LANE), lambda bb, s: (0, 0)),
    ]
    args = [xbc3, xbc3, xbc3, p3, h0, tri, expand, bias_row, a_row]
    if fin is not None:
        y0, dskip_row, normg_row = fin
        in_specs += [
            pl.BlockSpec((1, q, SSD_INNER), lambda bb, s: (bb, cidx(s), HYB_Z0 // SSD_INNER)),
            pl.BlockSpec((1, q, SSD_INNER), lambda bb, s: (bb, cidx(s), 0)),
            pl.BlockSpec((1, SSD_INNER), lambda bb, s: (0, 0)),
            pl.BlockSpec((1, SSD_INNER), lambda bb, s: (0, 0)),
        ]
        args += [p3, y0, dskip_row, normg_row]
    return pl.pallas_call(
        functools.partial(_ssd_kernel, reverse=reverse, finalize=fin is not None),
        out_shape=(jax.ShapeDtypeStruct((b, l, SSD_INNER), BF16 if fin is not None else F32),
                   jax.ShapeDtypeStruct((b, SSD_STATE, SSD_INNER), F32)),
        grid=(b, nc),
        in_specs=in_specs,
        out_specs=(pl.BlockSpec((1, q, SSD_INNER), lambda bb, s: (bb, cidx(s), 0)),
                   pl.BlockSpec((1, SSD_STATE, SSD_INNER), lambda bb, s: (bb, 0, 0))),
        scratch_shapes=[pltpu.VMEM((SSD_STATE, SSD_INNER), F32), pltpu.VMEM((q, SSD_INNER), F32)],
        compiler_params=_params("parallel", "arbitrary"),
        name="ssd_scan_bwd" if reverse else "ssd_scan_fwd",
    )(*args)


def _qproj_kernel(x_ref, g_ref, w_ref, cos_ref, sin_ref, o_ref):
    h = _rms(x_ref[...], g_ref[...]).astype(BF16)
    r = jnp.dot(h, w_ref[...], preferred_element_type=F32)
    main = MLA_HEADS * MLA_QK_PAD
    cos, sin = cos_ref[...], sin_ref[...]
    for hd in range(MLA_HEADS):
        c0 = hd * MLA_QK_PAD
        o_ref[:, c0:c0 + LANE] = r[:, c0:c0 + LANE].astype(o_ref.dtype)
        swapped = r[:, main + hd * LANE:main + (hd + 1) * LANE]
        o_ref[:, c0 + LANE:c0 + 2 * LANE] = (r[:, c0 + LANE:c0 + 2 * LANE] * cos + swapped * sin).astype(o_ref.dtype)


def _qproj(p, g, w, cos, sin, rows_per_batch, tm):
    m = p.shape[0]
    tm = min(tm, m, rows_per_batch)
    nb = rows_per_batch // tm
    n_out = MLA_HEADS * MLA_QK_PAD
    return pl.pallas_call(
        _qproj_kernel,
        out_shape=jax.ShapeDtypeStruct((m, n_out), BF16),
        grid=(m // tm,),
        in_specs=[
            pl.BlockSpec((tm, MLA_Q_RANK), lambda i: (i, HYB_CQ0 // MLA_Q_RANK)),
            pl.BlockSpec((1, MLA_Q_RANK), lambda i: (0, 0)),
            pl.BlockSpec(w.shape, lambda i: (0, 0)),
            pl.BlockSpec((tm, LANE), lambda i: (i % nb, 0)),
            pl.BlockSpec((tm, LANE), lambda i: (i % nb, 0)),
        ],
        out_specs=pl.BlockSpec((tm, n_out), lambda i: (i, 0)),
        compiler_params=_params("parallel"),
        name="mla_q_proj",
    )(p, g, w, cos, sin)


def _kvproj_kernel(x_ref, misc_ref, misc2_ref, g_ref, w_ref, cos_ref, sin_ref, k_ref, v_ref):
    h = _rms(x_ref[...], g_ref[...]).astype(BF16)
    r = jnp.dot(h, w_ref[...], preferred_element_type=F32)
    kr = (misc_ref[...] * cos_ref[...] + misc2_ref[...] * sin_ref[...]).astype(k_ref.dtype)
    for hd in range(MLA_HEADS):
        c0 = hd * MLA_QK_PAD
        k_ref[:, c0:c0 + LANE] = r[:, hd * MLA_NOPE:(hd + 1) * MLA_NOPE].astype(k_ref.dtype)
        k_ref[:, c0 + LANE:c0 + 2 * LANE] = kr
    v_ref[...] = r[:, MLA_HEADS * MLA_NOPE:].astype(v_ref.dtype)


def _kvproj(p, g, w, cos, sin, rows_per_batch, tm):
    m = p.shape[0]
    tm = min(tm, m, rows_per_batch)
    nb = rows_per_batch // tm
    nk, nv = MLA_HEADS * MLA_QK_PAD, MLA_HEADS * MLA_V
    return pl.pallas_call(
        _kvproj_kernel,
        out_shape=(jax.ShapeDtypeStruct((m, nk), BF16), jax.ShapeDtypeStruct((m, nv), BF16)),
        grid=(m // tm,),
        in_specs=[
            pl.BlockSpec((tm, MLA_KV_RANK), lambda i: (i, HYB_CKV0 // MLA_KV_RANK)),
            pl.BlockSpec((tm, LANE), lambda i: (i, HYB_MISC0 // LANE)),
            pl.BlockSpec((tm, LANE), lambda i: (i, HYB_MISC2 // LANE)),
            pl.BlockSpec((1, MLA_KV_RANK), lambda i: (0, 0)),
            pl.BlockSpec(w.shape, lambda i: (0, 0)),
            pl.BlockSpec((tm, LANE), lambda i: (i % nb, 0)),
            pl.BlockSpec((tm, LANE), lambda i: (i % nb, 0)),
        ],
        out_specs=(pl.BlockSpec((tm, nk), lambda i: (i, 0)), pl.BlockSpec((tm, nv), lambda i: (i, 0))),
        compiler_params=_params("parallel"),
        name="mla_kv_proj",
    )(p, p, p, g, w, cos, sin)


def _attn_kernel(q_ref, kc_ref, vc_ref, *rest, with_latent):
    if with_latent:
        k_ref, v_ref, o_ref, m_ref, l_ref, acc_ref = rest
    else:
        o_ref, m_ref, l_ref, acc_ref = rest
    ki = pl.program_id(2)

    def update(hd, kblk, vblk):
        s = _nt_dot(q_ref[:, hd * MLA_QK_PAD:(hd + 1) * MLA_QK_PAD], kblk)
        tiles = s.shape[1] // LANE
        m_prev = m_ref[hd]
        m_new = jnp.maximum(m_prev, jnp.max(s, axis=-1, keepdims=True))
        alpha = jnp.exp2((m_prev - m_new) * EXP2_SCALE)
        p = jnp.exp2((s - jnp.tile(m_new, (1, tiles))) * EXP2_SCALE)
        psum = p[:, 0:LANE]
        for c in range(1, tiles):
            psum = psum + p[:, c * LANE:(c + 1) * LANE]
        l_ref[hd] = alpha * l_ref[hd] + psum
        vs = slice(hd * MLA_V, (hd + 1) * MLA_V)
        acc_ref[:, vs] = alpha * acc_ref[:, vs] + jnp.dot(p.astype(BF16), vblk, preferred_element_type=F32)
        m_ref[hd] = m_new

    @pl.when(ki == 0)
    def _():
        m_ref[...] = jnp.full(m_ref.shape, -jnp.inf, F32)
        l_ref[...] = jnp.zeros(l_ref.shape, F32)
        acc_ref[...] = jnp.zeros(acc_ref.shape, F32)
        for hd in range(MLA_HEADS):
            update(hd, kc_ref[:, hd * MLA_QK_PAD:(hd + 1) * MLA_QK_PAD], vc_ref[:, hd * MLA_V:(hd + 1) * MLA_V])

    if with_latent:
        for hd in range(MLA_HEADS):
            update(hd, k_ref[:, hd * MLA_QK_PAD:(hd + 1) * MLA_QK_PAD], v_ref[:, hd * MLA_V:(hd + 1) * MLA_V])

    @pl.when(ki == pl.num_programs(2) - 1)
    def _():
        for hd in range(MLA_HEADS):
            vs = slice(hd * MLA_V, (hd + 1) * MLA_V)
            o_ref[:, vs] = (acc_ref[:, vs] / jnp.sum(l_ref[hd], axis=-1, keepdims=True)).astype(o_ref.dtype)


def _attention(q, kc, vc, k, v, batch, tq, tk):
    m = q.shape[0]
    lq = m // batch
    lc = kc.shape[0] // batch
    tq = min(tq, lq)
    nq = lq // tq
    nq_pad, nv = MLA_HEADS * MLA_QK_PAD, MLA_HEADS * MLA_V
    in_specs = [
        pl.BlockSpec((tq, nq_pad), lambda b, i, j: (b * nq + i, 0)),
        pl.BlockSpec((lc, nq_pad), lambda b, i, j: (b, 0)),
        pl.BlockSpec((lc, nv), lambda b, i, j: (b, 0)),
    ]
    args = [q, kc, vc]
    nk = 1
    if k is not None:
        lk = k.shape[0] // batch
        tk = min(tk, lk)
        nk = lk // tk
        in_specs += [
            pl.BlockSpec((tk, nq_pad), lambda b, i, j: (b * nk + j, 0)),
            pl.BlockSpec((tk, nv), lambda b, i, j: (b * nk + j, 0)),
        ]
        args += [k, v]
    return pl.pallas_call(
        functools.partial(_attn_kernel, with_latent=k is not None),
        out_shape=jax.ShapeDtypeStruct((m, nv), BF16),
        grid=(batch, nq, nk),
        in_specs=in_specs,
        out_specs=pl.BlockSpec((tq, nv), lambda b, i, j: (b * nq + i, 0)),
        scratch_shapes=[pltpu.VMEM((MLA_HEADS, tq, LANE), F32), pltpu.VMEM((MLA_HEADS, tq, LANE), F32),
                        pltpu.VMEM((tq, nv), F32)],
        compiler_params=_params("parallel", "parallel", "arbitrary"),
        name="mla_attention",
    )(*args)


def _outproj_kernel(a1_ref, a2_ref, w1_ref, w2_ref, x_ref, gate_ref, o_ref):
    acc = jnp.dot(a1_ref[...], w1_ref[...], preferred_element_type=F32)
    acc = acc + jnp.dot(a2_ref[...], w2_ref[...], preferred_element_type=F32)
    o_ref[...] = x_ref[...] + gate_ref[0] * acc


def _outproj(a1, a2, w1, w2, x, mod, k_gate, rows_per_batch, tm):
    m, d = x.shape
    tm = min(tm, m)
    bidx = lambda i: (i * tm) // rows_per_batch
    return pl.pallas_call(
        _outproj_kernel,
        out_shape=jax.ShapeDtypeStruct((m, d), F32),
        grid=(m // tm,),
        in_specs=[
            pl.BlockSpec((tm, a1.shape[1]), lambda i: (i, 0)),
            pl.BlockSpec((tm, a2.shape[1]), lambda i: (i, 0)),
            pl.BlockSpec(w1.shape, lambda i: (0, 0)),
            pl.BlockSpec(w2.shape, lambda i: (0, 0)),
            pl.BlockSpec((tm, d), lambda i: (i, 0)),
            pl.BlockSpec((1, 1, d), lambda i: (bidx(i) * N_MOD + k_gate, 0, 0)),
        ],
        out_specs=pl.BlockSpec((tm, d), lambda i: (i, 0)),
        compiler_params=_params("parallel"),
        name="hyb_out_proj",
    )(a1, a2, w1, w2, x, mod)


def _gmlp_kernel(x_ref, g_ref, sh_ref, sc_ref, gate_ref, win_ref, lng_ref, lnb_ref, ws_ref, bs_ref, wout_ref,
                 o_ref, uv_ref):
    x = x_ref[...]
    tm = x.shape[0]
    h = (_rms(x, g_ref[...]) * (1.0 + sc_ref[0]) + sh_ref[0]).astype(BF16)
    uv = _gelu(jnp.dot(h, win_ref[...], preferred_element_type=F32))
    u, v = uv[:, :GM_INNER], uv[:, GM_INNER:]
    mu = jnp.mean(v, axis=-1, keepdims=True)
    var = jnp.mean(jnp.square(v - mu), axis=-1, keepdims=True)
    vn = ((v - mu) * lax.rsqrt(var + EPS) * lng_ref[...] + lnb_ref[...]).astype(BF16)
    gw = GM_INNER // GM_GROUPS
    for c in range(tm // GM_CHUNK):
        rs = slice(c * GM_CHUNK, (c + 1) * GM_CHUNK)
        for g in range(GM_GROUPS):
            cs = slice(g * gw, (g + 1) * gw)
            sv = jnp.dot(ws_ref[g], vn[rs, cs], preferred_element_type=F32) + bs_ref[:, cs]
            uv_ref[rs, cs] = (u[rs, cs] * sv).astype(BF16)
    o_ref[...] = x + gate_ref[0] * jnp.dot(uv_ref[...], wout_ref[...], preferred_element_type=F32)


def _gmlp(x, g, mod, rows_per_batch, w_in, ln_g, ln_b, ws, bs_full, w_out, tm):
    m, d = x.shape
    tm = min(tm, m)
    bidx = lambda i: (i * tm) // rows_per_batch
    const = lambda shape: pl.BlockSpec(shape, lambda i: (0,) * len(shape))
    modspec = lambda k: pl.BlockSpec((1, 1, d), lambda i: (bidx(i) * N_MOD + k, 0, 0))
    return pl.pallas_call(
        _gmlp_kernel,
        out_shape=jax.ShapeDtypeStruct((m, d), F32),
        grid=(m // tm,),
        in_specs=[
            pl.BlockSpec((tm, d), lambda i: (i, 0)), const((1, d)), modspec(0), modspec(1), modspec(2),
            const(w_in.shape), const((1, GM_INNER)), const((1, GM_INNER)), const(ws.shape), const(bs_full.shape),
            const(w_out.shape),
        ],
        out_specs=pl.BlockSpec((tm, d), lambda i: (i, 0)),
        scratch_shapes=[pltpu.VMEM((tm, GM_INNER), BF16)],
        compiler_params=_params("parallel"),
        name="chunk_gmlp",
    )(x, g, mod, mod, mod, w_in, ln_g, ln_b, ws, bs_full, w_out)


def _top_values(xs, k, want_rank):
    xs = list(xs)
    vals = [[] for _ in xs]
    ranks = [jnp.full(x.shape, float(k), F32) if w else None for x, w in zip(xs, want_rank)]
    for r in range(k):
        for i, x in enumerate(xs):
            mx = jnp.max(x, axis=0, keepdims=True)
            vals[i].append(mx)
            hit = x == mx
            if want_rank[i]:
                ranks[i] = jnp.where(hit, float(r), ranks[i])
            if r + 1 < k:
                xs[i] = jnp.where(hit, -jnp.inf, x)
    return vals, ranks


def _peer_stats_kernel(x_ref, g_ref, sh_ref, sc_ref, wq_ref, k1_ref, k2_ref, tt_ref, an_ref, br_ref,
                       q_ref, cand_ref):
    y = _rms(x_ref[...], g_ref[...]) * (1.0 + sc_ref[0]) + sh_ref[0]
    h = y.astype(BF16)
    tt_ref[...] = y.T.astype(BF16)
    q_ref[...] = jnp.dot(h, wq_ref[...], preferred_element_type=F32).astype(BF16)
    half = PEER_DKEY // 2
    tm = h.shape[0]
    n_pairs = len(PEER_PAIRS)
    cand_ref[n_pairs:, :] = jnp.full((cand_ref.shape[0] - n_pairs, tm), -jnp.inf, F32)

    def head(hd, carry):
        c0 = pl.multiple_of(hd * PEER_DKEY, PEER_DKEY)
        s1 = _nt_dot(k1_ref[hd], q_ref[:, pl.ds(c0, half)])
        s2 = _nt_dot(k2_ref[hd], q_ref[:, pl.ds(c0 + half, half)])
        (v1, v2), (_, rank2) = _top_values((s1, s2), PEER_TOPK, (False, True))
        for idx, (a, b) in enumerate(PEER_PAIRS):
            cand_ref[idx:idx + 1, :] = v1[a] + v2[b]
        (tops,), _ = _top_values((cand_ref[...],), PEER_TOPK, (False,))
        thr = tops[PEER_TOPK - 1]
        z = tops[0] * 0.0
        for tv in tops:
            z = z + jnp.exp(tv - tops[0])
        counts = [None] * PEER_TOPK
        for a, b in PEER_PAIRS:
            hit = jnp.where(v1[a] + v2[b] >= thr, 1.0, 0.0)
            counts[a] = hit if counts[a] is None else counts[a] + hit
        n = jnp.zeros(s1.shape, F32)
        for a in range(PEER_TOPK):
            n = jnp.where(s1 == v1[a], counts[a], n)
        an_ref[hd, :, 0:tm] = jnp.exp(s1 - v1[0]) * (0.5 / z)
        an_ref[hd, :, tm:2 * tm] = n
        bvals = jnp.exp(s2 - v2[0])
        for c in range(tm // LANE):
            br_ref[hd, c, :, 0:LANE] = bvals[:, c * LANE:(c + 1) * LANE]
            br_ref[hd, c, :, LANE:2 * LANE] = rank2[:, c * LANE:(c + 1) * LANE]
        return carry

    lax.fori_loop(0, PEER_HEADS, head, 0)


def _peer_stats(x, g, mod, rows_per_batch, wq, k1, k2, tm):
    m, d = x.shape
    tm = min(tm, m)
    bidx = lambda i: (i * tm) // rows_per_batch
    const = lambda shape: pl.BlockSpec(shape, lambda i: (0,) * len(shape))
    modspec = lambda k: pl.BlockSpec((1, 1, d), lambda i: (bidx(i) * N_MOD + k, 0, 0))
    return pl.pallas_call(
        _peer_stats_kernel,
        out_shape=(jax.ShapeDtypeStruct((d, m), BF16),
                   jax.ShapeDtypeStruct((PEER_HEADS, PEER_NKEYS, 2 * m), F32),
                   jax.ShapeDtypeStruct((PEER_HEADS, m // LANE, PEER_NKEYS, 2 * LANE), F32)),
        grid=(m // tm,),
        in_specs=[pl.BlockSpec((tm, d), lambda i: (i, 0)), const((1, d)), modspec(3), modspec(4),
                  const(wq.shape), const(k1.shape), const(k2.shape)],
        out_specs=(pl.BlockSpec((d, tm), lambda i: (0, i)),
                   pl.BlockSpec((PEER_HEADS, PEER_NKEYS, 2 * tm), lambda i: (0, 0, i)),
                   pl.BlockSpec((PEER_HEADS, tm // LANE, PEER_NKEYS, 2 * LANE), lambda i: (0, i, 0, 0))),
        scratch_shapes=[pltpu.VMEM((tm, PEER_HEADS * PEER_DKEY), BF16),
                        pltpu.VMEM((-(-len(PEER_PAIRS) // SUBLANE) * SUBLANE, tm), F32)],
        compiler_params=_params("parallel"),
        name="peer_topk",
    )(x, g, mod, mod, wq, k1, k2)


def _peer_expert_kernel(tt_ref, u_ref, vt_ref, an_ref, br_ref, x_ref, gate_ref, o_ref,
                        acc_ref, s_ref, g_ref, *, te):
    j = pl.program_id(1)
    d, tm = acc_ref.shape
    rows = te // PEER_NKEYS
    sub = PEER_NKEYS // 4
    group = 2

    @pl.when(j == 0)
    def _():
        acc_ref[...] = jnp.zeros(acc_ref.shape, F32)

    halves = [slice(h0, h0 + MXU_COLS) for h0 in range(0, tm, MXU_COLS)]
    for hs in halves:
        s_ref[:, hs] = jnp.dot(u_ref[...], tt_ref[:, hs], preferred_element_type=F32)
    subs = [slice(k0, k0 + sub) for k0 in range(0, PEER_NKEYS, sub)]
    stm = min(PEER_STATS_TM, tm)
    for c in range(tm // LANE):
        cs = slice(c * LANE, (c + 1) * LANE)
        a0 = (c * LANE // stm) * 2 * stm + (c * LANE) % stm
        for r0 in range(0, rows, group):
            w = [[None] * len(subs) for _ in range(group)]
            for hd in range(PEER_HEADS):
                a_rows = [an_ref[hd, r0 + q:r0 + q + 1, a0:a0 + LANE] for q in range(group)]
                n_rows = [an_ref[hd, r0 + q:r0 + q + 1, a0 + stm:a0 + stm + LANE] for q in range(group)]
                for k, ks in enumerate(subs):
                    bv, rk = br_ref[hd, c, ks, 0:LANE], br_ref[hd, c, ks, LANE:2 * LANE]
                    for q in range(group):
                        term = a_rows[q] * jnp.where(rk < n_rows[q], bv, 0.0)
                        w[q][k] = term if w[q][k] is None else w[q][k] + term
            for q in range(group):
                for k, ks in enumerate(subs):
                    es = slice((r0 + q) * PEER_NKEYS + ks.start, (r0 + q) * PEER_NKEYS + ks.stop)
                    sc = s_ref[es, cs]
                    g_ref[es, cs] = (sc * (1.0 + lax.erf(sc * (2.0 ** -0.5))) * w[q][k]).astype(BF16)
    for hs in halves:
        acc_ref[:, hs] += jnp.dot(vt_ref[0], g_ref[:, hs], preferred_element_type=F32)

    @pl.when(j == pl.num_programs(1) - 1)
    def _():
        o_ref[...] = x_ref[...] + gate_ref[0] * acc_ref[...].T


def _peer_experts(tt, u, vt, an, br, x, mod, k_gate, rows_per_batch, tm, te):
    m, d = x.shape
    tm = min(tm, m)
    ne = u.shape[0]
    bidx = lambda i: (i * tm) // rows_per_batch
    rows = te // PEER_NKEYS
    return pl.pallas_call(
        functools.partial(_peer_expert_kernel, te=te),
        out_shape=jax.ShapeDtypeStruct((m, d), F32),
        grid=(m // tm, ne // te),
        in_specs=[
            pl.BlockSpec((d, tm), lambda i, j: (0, i)),
            pl.BlockSpec((te, d), lambda i, j: (j, 0)),
            pl.BlockSpec((1, d, te), lambda i, j: (j, 0, 0)),
            pl.BlockSpec((PEER_HEADS, rows, 2 * tm), lambda i, j: (0, j, i)),
            pl.BlockSpec((PEER_HEADS, tm // LANE, PEER_NKEYS, 2 * LANE), lambda i, j: (0, i, 0, 0)),
            pl.BlockSpec((tm, d), lambda i, j: (i, 0)),
            pl.BlockSpec((1, 1, d), lambda i, j: (bidx(i) * N_MOD + k_gate, 0, 0)),
        ],
        out_specs=pl.BlockSpec((tm, d), lambda i, j: (i, 0)),
        scratch_shapes=[pltpu.VMEM((d, tm), F32), pltpu.VMEM((te, tm), F32), pltpu.VMEM((te, tm), BF16)],
        compiler_params=_params("parallel", "arbitrary"),
        name="peer_experts",
    )(tt, u, vt, an, br, x, mod)


def _final_norm_kernel(x_ref, g_ref, o_ref):
    o_ref[...] = _rms(x_ref[...], g_ref[...])


def _final_norm(x, g, tm):
    m, d = x.shape
    tm = min(tm, m)
    return pl.pallas_call(
        _final_norm_kernel,
        out_shape=jax.ShapeDtypeStruct((m, d), F32),
        grid=(m // tm,),
        in_specs=[pl.BlockSpec((tm, d), lambda i: (i, 0)), pl.BlockSpec((1, d), lambda i: (0, 0))],
        out_specs=pl.BlockSpec((tm, d), lambda i: (i, 0)),
        compiler_params=_params("parallel"),
        name="final_rmsnorm",
    )(x, g)


def _rope_tables(n_lat, n_ctx):
    rows = n_lat // GRID_W
    row = jnp.broadcast_to(jnp.arange(rows, dtype=F32)[:, None], (rows, GRID_W)).reshape(-1)
    col = jnp.broadcast_to(jnp.arange(GRID_W, dtype=F32)[None, :], (rows, GRID_W)).reshape(-1)
    n_freq = MLA_ROPE // 4
    inv = ROPE_BASE ** (-jnp.arange(n_freq, dtype=F32) / n_freq)
    ang = jnp.concatenate([row[:, None] * inv, col[:, None] * inv], axis=-1)
    cos, sin = jnp.cos(ang), jnp.sin(ang)
    pad = jnp.zeros((n_lat, LANE - MLA_ROPE), F32)
    cos_l = jnp.concatenate([cos, cos, pad], axis=-1)
    sin_l = jnp.concatenate([-sin, sin, pad], axis=-1)
    cos_c = jnp.concatenate([jnp.ones((n_ctx, MLA_ROPE), F32), jnp.zeros((n_ctx, LANE - MLA_ROPE), F32)], axis=-1)
    return (cos_l, sin_l), (cos_c, jnp.zeros((n_ctx, LANE), F32))


def _swap_halves(w):
    half = w.shape[-1] // 2
    return jnp.concatenate([w[..., half:], w[..., :half]], axis=-1)


def _prep_hybrid(w_in, conv_w, conv_b, a_log, dt_bias, d_skip, norm_g, q_norm_g, w_qb, kv_norm_g, w_kvb, w_out):
    d = w_in.shape[0]
    o = 0
    pieces = {}
    for name, width in (("z", SSD_INNER), ("xbc", SSD_CONV_DIM), ("dt", SSD_HEADS), ("cq", MLA_Q_RANK),
                        ("ckv", MLA_KV_RANK), ("kr", MLA_ROPE)):
        pieces[name] = w_in[:, o:o + width]
        o += width
    zeros = lambda n: jnp.zeros((d, n), w_in.dtype)
    w_hyb = jnp.concatenate([
        pieces["z"], pieces["kr"], pieces["dt"], zeros(LANE - MLA_ROPE - SSD_HEADS), pieces["cq"], pieces["xbc"],
        pieces["ckv"], _swap_halves(pieces["kr"]), zeros(LANE - MLA_ROPE)], axis=1).astype(BF16)
    conv_w8 = jnp.concatenate([conv_w.T, jnp.zeros((SUBLANE - SSD_CONV, SSD_CONV_DIM), F32)], axis=0)
    lane_row = lambda v: jnp.zeros((1, LANE), F32).at[0, DT_LANE0:DT_LANE0 + SSD_HEADS].set(v)
    dirs = [(lane_row(dt_bias[k]), lane_row(-jnp.exp(a_log[k]))) for k in range(2)]
    expand = jnp.zeros((LANE, SSD_INNER), F32).at[DT_LANE0:DT_LANE0 + SSD_HEADS].set(
        jnp.repeat(jnp.eye(SSD_HEADS, dtype=F32), SSD_HEAD_DIM, axis=1)).astype(BF16)
    wq3 = w_qb.reshape(MLA_Q_RANK, MLA_HEADS, MLA_NOPE + MLA_ROPE)
    zq = jnp.zeros((MLA_Q_RANK, MLA_HEADS, LANE - MLA_ROPE), w_qb.dtype)
    q_main = jnp.concatenate([wq3, zq], axis=-1).reshape(MLA_Q_RANK, MLA_HEADS * MLA_QK_PAD)
    q_swap = jnp.concatenate([_swap_halves(wq3[..., MLA_NOPE:]), zq], axis=-1).reshape(MLA_Q_RANK, MLA_HEADS * LANE)
    wq = jnp.concatenate([q_main, q_swap], axis=1).astype(BF16)
    wkv3 = w_kvb.reshape(MLA_KV_RANK, MLA_HEADS, MLA_NOPE + MLA_V)
    wkv = jnp.concatenate([wkv3[..., :MLA_NOPE].reshape(MLA_KV_RANK, -1), wkv3[..., MLA_NOPE:].reshape(MLA_KV_RANK, -1)],
                          axis=1).astype(BF16)
    return dict(
        w_hyb=w_hyb, conv_w8=conv_w8, conv_b=conv_b[None, :], dirs=dirs, expand=expand,
        dskip=jnp.repeat(d_skip, SSD_HEAD_DIM)[None, :], normg=norm_g[None, :],
        q_norm_g=q_norm_g[None, :], wq=wq, kv_norm_g=kv_norm_g[None, :], wkv=wkv,
        w_out1=w_out[:SSD_INNER].astype(BF16), w_out2=w_out[SSD_INNER:].astype(BF16))


def _hybrid_layer(xl, xc, batch, g1, mod_l, mod_c, hp, ropes, update_ctx):
    n_lat, n_ctx = xl.shape[0] // batch, xc.shape[0] // batch
    (cos_l, sin_l), (cos_c, sin_c) = ropes
    q = SSD_CHUNK
    tri_f = jnp.tril(jnp.ones((q, q), F32))
    tri_b = jnp.triu(jnp.ones((q, q), F32))
    pl_ = _nmm(xl, g1, mod_l, 0, 1, n_lat, hp["w_hyb"], 512, HYB_COLS // 3)
    pc_ = _nmm(xc, g1, mod_c, 0, 1, xc.shape[0], hp["w_hyb"], 256, HYB_COLS // 3)
    pl3, pc3 = pl_.reshape(batch, n_lat, HYB_COLS), pc_.reshape(batch, n_ctx, HYB_COLS)
    xbc_l = _conv_silu(pl3, hp["conv_w8"], hp["conv_b"], 512)
    xbc_c = _conv_silu(pc3, hp["conv_w8"], hp["conv_b"], 512)
    h0 = jnp.zeros((batch, SSD_STATE, SSD_INNER), F32)
    (bias0, a0), (bias1, a1) = hp["dirs"]
    yc0, hc0 = _ssd_scan(xbc_c, pc3, h0, tri_f, hp["expand"], bias0, a0, False)
    yl0, _ = _ssd_scan(xbc_l, pl3, hc0, tri_f, hp["expand"], bias0, a0, False)
    fin = lambda y0: (y0, hp["dskip"], hp["normg"])
    sc, hc1 = _ssd_scan(xbc_c, pc3, h0, tri_b, hp["expand"], bias1, a1, True, fin(yc0))
    sl, _ = _ssd_scan(xbc_l, pl3, hc1, tri_b, hp["expand"], bias1, a1, True, fin(yl0))
    ql = _qproj(pl_, hp["q_norm_g"], hp["wq"], cos_l, sin_l, n_lat, 512)
    kl, vl = _kvproj(pl_, hp["kv_norm_g"], hp["wkv"], cos_l, sin_l, n_lat, 512)
    kc, vc = _kvproj(pc_, hp["kv_norm_g"], hp["wkv"], cos_c, sin_c, n_ctx, 256)
    att_l = _attention(ql, kc, vc, kl, vl, batch, 512, 1024)
    xl_new = _outproj(sl.reshape(-1, SSD_INNER), att_l, hp["w_out1"], hp["w_out2"], xl, mod_l, 2, n_lat, 512)
    xc_new = xc
    if update_ctx:
        qc = _qproj(pc_, hp["q_norm_g"], hp["wq"], cos_c, sin_c, n_ctx, 256)
        att_c = _attention(qc, kc, vc, None, None, batch, 256, 256)
        xc_new = _outproj(sc.reshape(-1, SSD_INNER), att_c, hp["w_out1"], hp["w_out2"], xc, mod_c, 2, xc.shape[0], 256)
    return xl_new, xc_new


def _peer_layer(x, g2, mod, rows_per_batch, pp):
    tt, an, br = _peer_stats(x, g2, mod, rows_per_batch, pp["wq"], pp["k1"], pp["k2"], PEER_STATS_TM)
    return _peer_experts(tt, pp["u"], pp["vt"], an, br, x, mod, 5, rows_per_batch, PEER_TM, PEER_TE)


def kernel(x, c, ctx, c_ctx, ada_w, ada_b, norm1_g, norm2_g, hyb_w_in, ssd_conv_w, ssd_conv_b, ssd_a_log,
           ssd_dt_bias, ssd_d, ssd_norm_g, mla_q_norm_g, mla_w_qb, mla_kv_norm_g, mla_w_kvb, hyb_w_out, gm_w_in,
           gm_ln_g, gm_ln_b, gm_ws, gm_bs, gm_w_out, peer_wq, peer_k1, peer_k2, peer_u, peer_v, final_norm_g):
    batch, n_lat, d = x.shape
    n_ctx = ctx.shape[1]
    depth = ada_w.shape[0]
    xl = x.reshape(batch * n_lat, d)
    xc = ctx.reshape(batch * n_ctx, d)
    n_cond = -(-(batch + 1) // SUBLANE) * SUBLANE
    conds = jnp.zeros((n_cond, d), F32).at[:batch].set(c).at[batch].set(c_ctx)
    mods = _mods(conds, ada_w, ada_b)
    ropes = _rope_tables(n_lat, n_ctx)
    for layer in range(depth):
        i = layer // 2
        even = layer % 2 == 0
        keep_ctx = any(j % 2 == 0 for j in range(layer + 1, depth))
        mod_l = mods[layer, :batch].reshape(batch * N_MOD, 1, d)
        mod_c = mods[layer, batch].reshape(N_MOD, 1, d)
        g1, g2 = norm1_g[layer][None, :], norm2_g[layer][None, :]
        if even:
            hp = _prep_hybrid(hyb_w_in[i], ssd_conv_w[i], ssd_conv_b[i], ssd_a_log[i], ssd_dt_bias[i], ssd_d[i],
                              ssd_norm_g[i], mla_q_norm_g[i], mla_w_qb[i], mla_kv_norm_g[i], mla_w_kvb[i], hyb_w_out[i])
            xl, xc = _hybrid_layer(xl, xc, batch, g1, mod_l, mod_c, hp, ropes, keep_ctx)
        else:
            gp = (gm_w_in[i].astype(BF16), gm_ln_g[i][None, :], gm_ln_b[i][None, :], gm_ws[i].astype(BF16),
                  jnp.repeat(gm_bs[i].T, GM_INNER // GM_GROUPS, axis=1), gm_w_out[i].astype(BF16))
            xl = _gmlp(xl, g1, mod_l, n_lat, *gp, 256)
            if keep_ctx:
                xc = _gmlp(xc, g1, mod_c, xc.shape[0], *gp, 256)
        pp = dict(wq=peer_wq[layer].astype(BF16), k1=peer_k1[layer].astype(BF16), k2=peer_k2[layer].astype(BF16),
                  u=peer_u[layer].astype(BF16),
                  vt=jnp.transpose(peer_v[layer].astype(BF16).reshape(-1, PEER_TE, d), (0, 2, 1)))
        xl = _peer_layer(xl, g2, mod_l, n_lat, pp)
        if keep_ctx:
            xc = _peer_layer(xc, g2, mod_c, xc.shape[0], pp)
    return _final_norm(xl, final_norm_g[None, :], 512).reshape(batch, n_lat, d)
```

```python
import functools

import jax
import jax.numpy as jnp
from jax import lax
from jax.experimental import pallas as pl
from jax.experimental.pallas import tpu as pltpu

F32, BF16 = jnp.float32, jnp.bfloat16
EPS = 1e-6
LANE = 128
SUBLANE = 8
MXU_COLS = 256
VMEM_LIMIT = 56 * 1024 * 1024

D_MODEL = 1024
DEPTH = 4
GRID_W = 64
N_MOD = 6
SSD_HEADS = 16
SSD_HEAD_DIM = 64
SSD_INNER = SSD_HEADS * SSD_HEAD_DIM
SSD_GROUPS = 2
SSD_STATE = 128
SSD_BC = SSD_GROUPS * SSD_STATE
SSD_CONV = 5
SSD_CONV_DIM = SSD_INNER + 2 * SSD_BC
SSD_CHUNK = 128
MLA_HEADS = 8
MLA_Q_RANK = 384
MLA_KV_RANK = 256
MLA_NOPE = 128
MLA_ROPE = 64
MLA_V = 128
MLA_SCALE = (MLA_NOPE + MLA_ROPE) ** -0.5
EXP2_SCALE = MLA_SCALE * 1.4426950408889634
ROPE_BASE = 10000.0
GM_CHUNK = 128
GM_INNER = 2 * D_MODEL
GM_GROUPS = 8
PEER_HEADS = 8
PEER_NKEYS = 128
PEER_EXPERTS = PEER_NKEYS * PEER_NKEYS
PEER_DKEY = 256
PEER_TOPK = 16
PEER_TE = 1024
PEER_TM = 512
PEER_STATS_TM = 256

HYB_Z0 = 0
HYB_MISC0 = 1024
HYB_CQ0 = 1152
HYB_XBC0 = 1536
HYB_CKV0 = 3072
HYB_MISC2 = 3328
HYB_COLS = 3456
DT_LANE0 = MLA_ROPE
MLA_QK_PAD = 256

PEER_PAIRS = tuple((a, b) for a in range(PEER_TOPK) for b in range(PEER_TOPK) if (a + 1) * (b + 1) <= PEER_TOPK)


def _params(*sem, flags=None):
    return pltpu.CompilerParams(dimension_semantics=sem, vmem_limit_bytes=VMEM_LIMIT, flags=flags)


def _rms(x, g):
    return x * lax.rsqrt(jnp.mean(x * x, axis=-1, keepdims=True) + EPS) * g


def _silu(x):
    return x * jax.nn.sigmoid(x)


def _gelu(x):
    return 0.5 * x * (1.0 + lax.erf(x * (2.0 ** -0.5)))


def _nt_dot(a, b):
    return lax.dot_general(a, b, (((1,), (1,)), ((), ())), preferred_element_type=F32)


def _split_dot(x, w, terms, left=False):
    acc = None
    rem = x
    for _ in range(terms):
        hi = rem.astype(BF16)
        part = jnp.dot(w, hi, preferred_element_type=F32) if left else jnp.dot(hi, w, preferred_element_type=F32)
        acc = part if acc is None else acc + part
        rem = rem - hi.astype(F32)
    return acc


def _mods_kernel(c_ref, w_ref, b_ref, o_ref):
    h = _silu(c_ref[...]).astype(BF16)
    o_ref[0] = jnp.dot(h, w_ref[0].astype(BF16), preferred_element_type=F32) + b_ref[0]


def _mods(conds, ada_w, ada_b):
    depth, d, n = ada_w.shape
    tn = 1536
    return pl.pallas_call(
        _mods_kernel,
        out_shape=jax.ShapeDtypeStruct((depth, conds.shape[0], n), F32),
        grid=(depth, n // tn),
        in_specs=[
            pl.BlockSpec(conds.shape, lambda l, j: (0, 0)),
            pl.BlockSpec((1, d, tn), lambda l, j: (l, 0, j)),
            pl.BlockSpec((1, 1, tn), lambda l, j: (l, 0, j)),
        ],
        out_specs=pl.BlockSpec((1, conds.shape[0], tn), lambda l, j: (l, 0, j)),
        compiler_params=_params("parallel", "parallel"),
        name="ada_mods",
    )(conds, ada_w, ada_b.reshape(depth, 1, n))


def _nmm_kernel(x_ref, g_ref, sh_ref, sc_ref, w_ref, o_ref, h_ref):
    @pl.when(pl.program_id(1) == 0)
    def _():
        y = _rms(x_ref[...], g_ref[...])
        h_ref[...] = (y * (1.0 + sc_ref[0]) + sh_ref[0]).astype(BF16)

    o_ref[...] = jnp.dot(h_ref[...], w_ref[...], preferred_element_type=F32).astype(o_ref.dtype)


def _nmm(x, g, mod, k_shift, k_scale, rows_per_batch, w, tm, tn):
    m, k = x.shape
    n = w.shape[1]
    tm = min(tm, m)
    bidx = lambda i: (i * tm) // rows_per_batch
    return pl.pallas_call(
        _nmm_kernel,
        out_shape=jax.ShapeDtypeStruct((m, n), F32),
        grid=(m // tm, n // tn),
        in_specs=[
            pl.BlockSpec((tm, k), lambda i, j: (i, 0)),
            pl.BlockSpec((1, k), lambda i, j: (0, 0)),
            pl.BlockSpec((1, 1, k), lambda i, j: (bidx(i) * N_MOD + k_shift, 0, 0)),
            pl.BlockSpec((1, 1, k), lambda i, j: (bidx(i) * N_MOD + k_scale, 0, 0)),
            pl.BlockSpec((k, tn), lambda i, j: (0, j)),
        ],
        out_specs=pl.BlockSpec((tm, tn), lambda i, j: (i, j)),
        scratch_shapes=[pltpu.VMEM((tm, k), BF16)],
        compiler_params=_params("parallel", "arbitrary"),
        name="norm_mod_matmul",
    )(x, g, mod, mod, w)


def _conv_kernel(cur_ref, prev_ref, next_ref, w_ref, b_ref, o_ref, buf_ref, *, tl):
    i = pl.program_id(1)
    halo = SUBLANE
    buf_ref[0:halo] = jnp.where(i > 0, prev_ref[0], 0.0)
    buf_ref[halo:halo + tl] = cur_ref[0]
    buf_ref[halo + tl:2 * halo + tl] = jnp.where(i < pl.num_programs(1) - 1, next_ref[0], 0.0)
    acc = jnp.broadcast_to(b_ref[...], (tl, b_ref.shape[1]))
    for k in range(SSD_CONV):
        start = halo + k - SSD_CONV // 2
        acc = acc + w_ref[k:k + 1, :] * buf_ref[start:start + tl, :]
    o_ref[0] = _silu(acc)


def _conv_silu(p3, conv_w8, conv_b, tl):
    b, l, _ = p3.shape
    c = SSD_CONV_DIM
    tl = min(tl, l)
    cb = HYB_XBC0 // c
    nh = l // SUBLANE
    r = tl // SUBLANE
    return pl.pallas_call(
        functools.partial(_conv_kernel, tl=tl),
        out_shape=jax.ShapeDtypeStruct((b, l, c), F32),
        grid=(b, l // tl),
        in_specs=[
            pl.BlockSpec((1, tl, c), lambda bb, i: (bb, i, cb)),
            pl.BlockSpec((1, SUBLANE, c), lambda bb, i: (bb, jnp.maximum(i * r - 1, 0), cb)),
            pl.BlockSpec((1, SUBLANE, c), lambda bb, i: (bb, jnp.minimum((i + 1) * r, nh - 1), cb)),
            pl.BlockSpec((SUBLANE, c), lambda bb, i: (0, 0)),
            pl.BlockSpec((1, c), lambda bb, i: (0, 0)),
        ],
        out_specs=pl.BlockSpec((1, tl, c), lambda bb, i: (bb, i, 0)),
        scratch_shapes=[pltpu.VMEM((tl + 2 * SUBLANE, c), F32)],
        compiler_params=_params("parallel", "parallel"),
        name="dwconv_silu",
    )(p3, p3, p3, conv_w8, conv_b)


def _ssd_kernel(xs_ref, bm_ref, cm_ref, misc_ref, h0_ref, tri_ref, e_ref, bias_ref, a_ref, *rest,
                reverse, finalize):
    if finalize:
        z_ref, y0_ref, dskip_ref, ng_ref, y_ref, hT_ref, st_ref, yb_ref = rest
    else:
        y_ref, hT_ref, st_ref, yb_ref = rest
    q = SSD_CHUNK
    gw = SSD_INNER // SSD_GROUPS
    hpg = SSD_HEADS // SSD_GROUPS

    @pl.when(pl.program_id(1) == 0)
    def _():
        st_ref[...] = h0_ref[0]

    xs = xs_ref[0]
    tri = tri_ref[...]
    visible = tri > 0.5
    lane = lax.broadcasted_iota(jnp.int32, (q, LANE), 1)
    dt_lanes = (lane >= DT_LANE0) & (lane < DT_LANE0 + SSD_HEADS)
    dt = jnp.where(dt_lanes, jax.nn.softplus(misc_ref[0] + bias_ref[...]), 0.0)
    da = dt * a_ref[...]
    cum = _split_dot(da, tri.astype(BF16), 3, left=True)
    cum_t = cum.T
    dt_t = dt.T
    edge = cum[0:1, :] if reverse else cum[q - 1:q, :]
    expcum = jnp.exp(cum)
    w_end = jnp.exp(edge - cum) * dt
    e = e_ref[...]
    expcum_x = _split_dot(expcum, e, 2)
    w_end_x = _split_dot(w_end, e, 2)
    xs_b = xs.astype(BF16)
    xw_b = (xs * w_end_x).astype(BF16)
    st = st_ref[...]
    st_b = st.astype(BF16)
    first_head = lane < SSD_HEAD_DIM
    decay_row = expcum_x[0:1, :] if reverse else expcum_x[q - 1:q, :]
    for g in range(SSD_GROUPS):
        bm = bm_ref[0][:, g * SSD_STATE:(g + 1) * SSD_STATE]
        cm_b = cm_ref[0][:, g * SSD_STATE:(g + 1) * SSD_STATE].astype(BF16)
        cb = _nt_dot(cm_b, bm.astype(BF16))
        gs = slice(g * gw, (g + 1) * gw)
        y_off = jnp.dot(cm_b, st_b[:, gs], preferred_element_type=F32) * expcum_x[:, gs]
        for jp in range(hpg // 2):
            c0 = (g * hpg + 2 * jp) * SSD_HEAD_DIM
            tiles = []
            for sub in range(2):
                col = DT_LANE0 + g * hpg + 2 * jp + sub
                seg = cum[:, col:col + 1] - cum_t[col:col + 1, :]
                lmat = jnp.exp(jnp.where(visible, seg, -jnp.inf))
                mm = (cb * lmat * dt_t[col:col + 1, :]).astype(BF16)
                tiles.append(jnp.dot(mm, xs_b[:, c0:c0 + LANE], preferred_element_type=F32))
            yb_ref[:, c0:c0 + LANE] = jnp.where(first_head, tiles[0], tiles[1]) + y_off[:, c0 - g * gw:c0 - g * gw + LANE]
        new = jnp.dot(bm.T.astype(BF16), xw_b[:, gs], preferred_element_type=F32)
        st_ref[:, gs] = st[:, gs] * decay_row[:, gs] + new

    if finalize:
        v = (yb_ref[...] + y0_ref[0] + dskip_ref[...] * xs) * _silu(z_ref[0])
        for g in range(SSD_GROUPS):
            gs = slice(g * gw, (g + 1) * gw)
            y_ref[0, :, gs] = _rms(v[:, gs], ng_ref[:, gs]).astype(y_ref.dtype)
    else:
        y_ref[0] = yb_ref[...]

    @pl.when(pl.program_id(1) == pl.num_programs(1) - 1)
    def _():
        hT_ref[0] = st_ref[...]


def _ssd_scan(xbc3, p3, h0, tri, expand, bias_row, a_row, reverse, fin=None):
    b, l, _ = xbc3.shape
    nc = l // SSD_CHUNK
    cidx = (lambda s: nc - 1 - s) if reverse else (lambda s: s)
    q = SSD_CHUNK
    in_specs = [
        pl.BlockSpec((1, q, SSD_INNER), lambda bb, s: (bb, cidx(s), 0)),
        pl.BlockSpec((1, q, SSD_BC), lambda bb, s: (bb, cidx(s), SSD_INNER // SSD_BC)),
        pl.BlockSpec((1, q, SSD_BC), lambda bb, s: (bb, cidx(s), SSD_INNER // SSD_BC + 1)),
        pl.BlockSpec((1, q, LANE), lambda bb, s: (bb, cidx(s), HYB_MISC0 // LANE)),
        pl.BlockSpec((1, SSD_STATE, SSD_INNER), lambda bb, s: (bb, 0, 0)),
        pl.BlockSpec((q, q), lambda bb, s: (0, 0)),
        pl.BlockSpec((LANE, SSD_INNER), lambda bb, s: (0, 0)),
        pl.BlockSpec((1, LANE), lambda bb, s: (0, 0)),
        pl.BlockSpec((1, LANE), lambda bb, s: (0, 0)),
    ]
    args = [xbc3, xbc3, xbc3, p3, h0, tri, expand, bias_row, a_row]
    if fin is not None:
        y0, dskip_row, normg_row = fin
        in_specs += [
            pl.BlockSpec((1, q, SSD_INNER), lambda bb, s: (bb, cidx(s), HYB_Z0 // SSD_INNER)),
            pl.BlockSpec((1, q, SSD_INNER), lambda bb, s: (bb, cidx(s), 0)),
            pl.BlockSpec((1, SSD_INNER), lambda bb, s: (0, 0)),
            pl.BlockSpec((1, SSD_INNER), lambda bb, s: (0, 0)),
        ]
        args += [p3, y0, dskip_row, normg_row]
    return pl.pallas_call(
        functools.partial(_ssd_kernel, reverse=reverse, finalize=fin is not None),
        out_shape=(jax.ShapeDtypeStruct((b, l, SSD_INNER), BF16 if fin is not None else F32),
                   jax.ShapeDtypeStruct((b, SSD_STATE, SSD_INNER), F32)),
        grid=(b, nc),
        in_specs=in_specs,
        out_specs=(pl.BlockSpec((1, q, SSD_INNER), lambda bb, s: (bb, cidx(s), 0)),
                   pl.BlockSpec((1, SSD_STATE, SSD_INNER), lambda bb, s: (bb, 0, 0))),
        scratch_shapes=[pltpu.VMEM((SSD_STATE, SSD_INNER), F32), pltpu.VMEM((q, SSD_INNER), F32)],
        compiler_params=_params("parallel", "arbitrary"),
        name="ssd_scan_bwd" if reverse else "ssd_scan_fwd",
    )(*args)


def _qproj_kernel(x_ref, g_ref, w_ref, cos_ref, sin_ref, o_ref):
    h = _rms(x_ref[...], g_ref[...]).astype(BF16)
    r = jnp.dot(h, w_ref[...], preferred_element_type=F32)
    main = MLA_HEADS * MLA_QK_PAD
    cos, sin = cos_ref[...], sin_ref[...]
    for hd in range(MLA_HEADS):
        c0 = hd * MLA_QK_PAD
        o_ref[:, c0:c0 + LANE] = r[:, c0:c0 + LANE].astype(o_ref.dtype)
        swapped = r[:, main + hd * LANE:main + (hd + 1) * LANE]
        o_ref[:, c0 + LANE:c0 + 2 * LANE] = (r[:, c0 + LANE:c0 + 2 * LANE] * cos + swapped * sin).astype(o_ref.dtype)


def _qproj(p, g, w, cos, sin, rows_per_batch, tm):
    m = p.shape[0]
    tm = min(tm, m, rows_per_batch)
    nb = rows_per_batch // tm
    n_out = MLA_HEADS * MLA_QK_PAD
    return pl.pallas_call(
        _qproj_kernel,
        out_shape=jax.ShapeDtypeStruct((m, n_out), BF16),
        grid=(m // tm,),
        in_specs=[
            pl.BlockSpec((tm, MLA_Q_RANK), lambda i: (i, HYB_CQ0 // MLA_Q_RANK)),
            pl.BlockSpec((1, MLA_Q_RANK), lambda i: (0, 0)),
            pl.BlockSpec(w.shape, lambda i: (0, 0)),
            pl.BlockSpec((tm, LANE), lambda i: (i % nb, 0)),
            pl.BlockSpec((tm, LANE), lambda i: (i % nb, 0)),
        ],
        out_specs=pl.BlockSpec((tm, n_out), lambda i: (i, 0)),
        compiler_params=_params("parallel"),
        name="mla_q_proj",
    )(p, g, w, cos, sin)


def _kvproj_kernel(x_ref, misc_ref, misc2_ref, g_ref, w_ref, cos_ref, sin_ref, k_ref, v_ref):
    h = _rms(x_ref[...], g_ref[...]).astype(BF16)
    r = jnp.dot(h, w_ref[...], preferred_element_type=F32)
    kr = (misc_ref[...] * cos_ref[...] + misc2_ref[...] * sin_ref[...]).astype(k_ref.dtype)
    for hd in range(MLA_HEADS):
        c0 = hd * MLA_QK_PAD
        k_ref[:, c0:c0 + LANE] = r[:, hd * MLA_NOPE:(hd + 1) * MLA_NOPE].astype(k_ref.dtype)
        k_ref[:, c0 + LANE:c0 + 2 * LANE] = kr
    v_ref[...] = r[:, MLA_HEADS * MLA_NOPE:].astype(v_ref.dtype)


def _kvproj(p, g, w, cos, sin, rows_per_batch, tm):
    m = p.shape[0]
    tm = min(tm, m, rows_per_batch)
    nb = rows_per_batch // tm
    nk, nv = MLA_HEADS * MLA_QK_PAD, MLA_HEADS * MLA_V
    return pl.pallas_call(
        _kvproj_kernel,
        out_shape=(jax.ShapeDtypeStruct((m, nk), BF16), jax.ShapeDtypeStruct((m, nv), BF16)),
        grid=(m // tm,),
        in_specs=[
            pl.BlockSpec((tm, MLA_KV_RANK), lambda i: (i, HYB_CKV0 // MLA_KV_RANK)),
            pl.BlockSpec((tm, LANE), lambda i: (i, HYB_MISC0 // LANE)),
            pl.BlockSpec((tm, LANE), lambda i: (i, HYB_MISC2 // LANE)),
            pl.BlockSpec((1, MLA_KV_RANK), lambda i: (0, 0)),
            pl.BlockSpec(w.shape, lambda i: (0, 0)),
            pl.BlockSpec((tm, LANE), lambda i: (i % nb, 0)),
            pl.BlockSpec((tm, LANE), lambda i: (i % nb, 0)),
        ],
        out_specs=(pl.BlockSpec((tm, nk), lambda i: (i, 0)), pl.BlockSpec((tm, nv), lambda i: (i, 0))),
        compiler_params=_params("parallel"),
        name="mla_kv_proj",
    )(p, p, p, g, w, cos, sin)


def _attn_kernel(q_ref, kc_ref, vc_ref, *rest, with_latent):
    if with_latent:
        k_ref, v_ref, o_ref, m_ref, l_ref, acc_ref = rest
    else:
        o_ref, m_ref, l_ref, acc_ref = rest
    ki = pl.program_id(2)

    def update(hd, kblk, vblk):
        s = _nt_dot(q_ref[:, hd * MLA_QK_PAD:(hd + 1) * MLA_QK_PAD], kblk)
        tiles = s.shape[1] // LANE
        m_prev = m_ref[hd]
        m_new = jnp.maximum(m_prev, jnp.max(s, axis=-1, keepdims=True))
        alpha = jnp.exp2((m_prev - m_new) * EXP2_SCALE)
        p = jnp.exp2((s - jnp.tile(m_new, (1, tiles))) * EXP2_SCALE)
        psum = p[:, 0:LANE]
        for c in range(1, tiles):
            psum = psum + p[:, c * LANE:(c + 1) * LANE]
        l_ref[hd] = alpha * l_ref[hd] + psum
        vs = slice(hd * MLA_V, (hd + 1) * MLA_V)
        acc_ref[:, vs] = alpha * acc_ref[:, vs] + jnp.dot(p.astype(BF16), vblk, preferred_element_type=F32)
        m_ref[hd] = m_new

    @pl.when(ki == 0)
    def _():
        m_ref[...] = jnp.full(m_ref.shape, -jnp.inf, F32)
        l_ref[...] = jnp.zeros(l_ref.shape, F32)
        acc_ref[...] = jnp.zeros(acc_ref.shape, F32)
        for hd in range(MLA_HEADS):
            update(hd, kc_ref[:, hd * MLA_QK_PAD:(hd + 1) * MLA_QK_PAD], vc_ref[:, hd * MLA_V:(hd + 1) * MLA_V])

    if with_latent:
        for hd in range(MLA_HEADS):
            update(hd, k_ref[:, hd * MLA_QK_PAD:(hd + 1) * MLA_QK_PAD], v_ref[:, hd * MLA_V:(hd + 1) * MLA_V])

    @pl.when(ki == pl.num_programs(2) - 1)
    def _():
        for hd in range(MLA_HEADS):
            vs = slice(hd * MLA_V, (hd + 1) * MLA_V)
            o_ref[:, vs] = (acc_ref[:, vs] / jnp.sum(l_ref[hd], axis=-1, keepdims=True)).astype(o_ref.dtype)


def _attention(q, kc, vc, k, v, batch, tq, tk):
    m = q.shape[0]
    lq = m // batch
    lc = kc.shape[0] // batch
    tq = min(tq, lq)
    nq = lq // tq
    nq_pad, nv = MLA_HEADS * MLA_QK_PAD, MLA_HEADS * MLA_V
    in_specs = [
        pl.BlockSpec((tq, nq_pad), lambda b, i, j: (b * nq + i, 0)),
        pl.BlockSpec((lc, nq_pad), lambda b, i, j: (b, 0)),
        pl.BlockSpec((lc, nv), lambda b, i, j: (b, 0)),
    ]
    args = [q, kc, vc]
    nk = 1
    if k is not None:
        lk = k.shape[0] // batch
        tk = min(tk, lk)
        nk = lk // tk
        in_specs += [
            pl.BlockSpec((tk, nq_pad), lambda b, i, j: (b * nk + j, 0)),
            pl.BlockSpec((tk, nv), lambda b, i, j: (b * nk + j, 0)),
        ]
        args += [k, v]
    return pl.pallas_call(
        functools.partial(_attn_kernel, with_latent=k is not None),
        out_shape=jax.ShapeDtypeStruct((m, nv), BF16),
        grid=(batch, nq, nk),
        in_specs=in_specs,
        out_specs=pl.BlockSpec((tq, nv), lambda b, i, j: (b * nq + i, 0)),
        scratch_shapes=[pltpu.VMEM((MLA_HEADS, tq, LANE), F32), pltpu.VMEM((MLA_HEADS, tq, LANE), F32),
                        pltpu.VMEM((tq, nv), F32)],
        compiler_params=_params("parallel", "parallel", "arbitrary"),
        name="mla_attention",
    )(*args)


def _outproj_kernel(a1_ref, a2_ref, w1_ref, w2_ref, x_ref, gate_ref, o_ref):
    acc = jnp.dot(a1_ref[...], w1_ref[...], preferred_element_type=F32)
    acc = acc + jnp.dot(a2_ref[...], w2_ref[...], preferred_element_type=F32)
    o_ref[...] = x_ref[...] + gate_ref[0] * acc


def _outproj(a1, a2, w1, w2, x, mod, k_gate, rows_per_batch, tm):
    m, d = x.shape
    tm = min(tm, m)
    bidx = lambda i: (i * tm) // rows_per_batch
    return pl.pallas_call(
        _outproj_kernel,
        out_shape=jax.ShapeDtypeStruct((m, d), F32),
        grid=(m // tm,),
        in_specs=[
            pl.BlockSpec((tm, a1.shape[1]), lambda i: (i, 0)),
            pl.BlockSpec((tm, a2.shape[1]), lambda i: (i, 0)),
            pl.BlockSpec(w1.shape, lambda i: (0, 0)),
            pl.BlockSpec(w2.shape, lambda i: (0, 0)),
            pl.BlockSpec((tm, d), lambda i: (i, 0)),
            pl.BlockSpec((1, 1, d), lambda i: (bidx(i) * N_MOD + k_gate, 0, 0)),
        ],
        out_specs=pl.BlockSpec((tm, d), lambda i: (i, 0)),
        compiler_params=_params("parallel"),
        name="hyb_out_proj",
    )(a1, a2, w1, w2, x, mod)


def _gmlp_kernel(x_ref, g_ref, sh_ref, sc_ref, gate_ref, win_ref, lng_ref, lnb_ref, ws_ref, bs_ref, wout_ref,
                 o_ref, uv_ref):
    x = x_ref[...]
    tm = x.shape[0]
    h = (_rms(x, g_ref[...]) * (1.0 + sc_ref[0]) + sh_ref[0]).astype(BF16)
    uv = _gelu(jnp.dot(h, win_ref[...], preferred_element_type=F32))
    u, v = uv[:, :GM_INNER], uv[:, GM_INNER:]
    mu = jnp.mean(v, axis=-1, keepdims=True)
    var = jnp.mean(jnp.square(v - mu), axis=-1, keepdims=True)
    vn = ((v - mu) * lax.rsqrt(var + EPS) * lng_ref[...] + lnb_ref[...]).astype(BF16)
    gw = GM_INNER // GM_GROUPS
    for c in range(tm // GM_CHUNK):
        rs = slice(c * GM_CHUNK, (c + 1) * GM_CHUNK)
        for g in range(GM_GROUPS):
            cs = slice(g * gw, (g + 1) * gw)
            sv = jnp.dot(ws_ref[g], vn[rs, cs], preferred_element_type=F32) + bs_ref[:, cs]
            uv_ref[rs, cs] = (u[rs, cs] * sv).astype(BF16)
    o_ref[...] = x + gate_ref[0] * jnp.dot(uv_ref[...], wout_ref[...], preferred_element_type=F32)


def _gmlp(x, g, mod, rows_per_batch, w_in, ln_g, ln_b, ws, bs_full, w_out, tm):
    m, d = x.shape
    tm = min(tm, m)
    bidx = lambda i: (i * tm) // rows_per_batch
    const = lambda shape: pl.BlockSpec(shape, lambda i: (0,) * len(shape))
    modspec = lambda k: pl.BlockSpec((1, 1, d), lambda i: (bidx(i) * N_MOD + k, 0, 0))
    return pl.pallas_call(
        _gmlp_kernel,
        out_shape=jax.ShapeDtypeStruct((m, d), F32),
        grid=(m // tm,),
        in_specs=[
            pl.BlockSpec((tm, d), lambda i: (i, 0)), const((1, d)), modspec(0), modspec(1), modspec(2),
            const(w_in.shape), const((1, GM_INNER)), const((1, GM_INNER)), const(ws.shape), const(bs_full.shape),
            const(w_out.shape),
        ],
        out_specs=pl.BlockSpec((tm, d), lambda i: (i, 0)),
        scratch_shapes=[pltpu.VMEM((tm, GM_INNER), BF16)],
        compiler_params=_params("parallel"),
        name="chunk_gmlp",
    )(x, g, mod, mod, mod, w_in, ln_g, ln_b, ws, bs_full, w_out)


def _top_values(xs, k, want_rank):
    xs = list(xs)
    vals = [[] for _ in xs]
    ranks = [jnp.full(x.shape, float(k), F32) if w else None for x, w in zip(xs, want_rank)]
    for r in range(k):
        for i, x in enumerate(xs):
            mx = jnp.max(x, axis=0, keepdims=True)
            vals[i].append(mx)
            hit = x == mx
            if want_rank[i]:
                ranks[i] = jnp.where(hit, float(r), ranks[i])
            if r + 1 < k:
                xs[i] = jnp.where(hit, -jnp.inf, x)
    return vals, ranks


def _peer_stats_kernel(x_ref, g_ref, sh_ref, sc_ref, wq_ref, k1_ref, k2_ref, tt_ref, an_ref, br_ref,
                       q_ref, cand_ref):
    y = _rms(x_ref[...], g_ref[...]) * (1.0 + sc_ref[0]) + sh_ref[0]
    h = y.astype(BF16)
    tt_ref[...] = y.T.astype(BF16)
    q_ref[...] = jnp.dot(h, wq_ref[...], preferred_element_type=F32).astype(BF16)
    half = PEER_DKEY // 2
    tm = h.shape[0]
    n_pairs = len(PEER_PAIRS)
    cand_ref[n_pairs:, :] = jnp.full((cand_ref.shape[0] - n_pairs, tm), -jnp.inf, F32)

    def head(hd, carry):
        c0 = pl.multiple_of(hd * PEER_DKEY, PEER_DKEY)
        s1 = _nt_dot(k1_ref[hd], q_ref[:, pl.ds(c0, half)])
        s2 = _nt_dot(k2_ref[hd], q_ref[:, pl.ds(c0 + half, half)])
        (v1, v2), (_, rank2) = _top_values((s1, s2), PEER_TOPK, (False, True))
        for idx, (a, b) in enumerate(PEER_PAIRS):
            cand_ref[idx:idx + 1, :] = v1[a] + v2[b]
        (tops,), _ = _top_values((cand_ref[...],), PEER_TOPK, (False,))
        thr = tops[PEER_TOPK - 1]
        z = tops[0] * 0.0
        for tv in tops:
            z = z + jnp.exp(tv - tops[0])
        counts = [None] * PEER_TOPK
        for a, b in PEER_PAIRS:
            hit = jnp.where(v1[a] + v2[b] >= thr, 1.0, 0.0)
            counts[a] = hit if counts[a] is None else counts[a] + hit
        n = jnp.zeros(s1.shape, F32)
        for a in range(PEER_TOPK):
            n = jnp.where(s1 == v1[a], counts[a], n)
        an_ref[hd, :, 0:tm] = jnp.exp(s1 - v1[0]) * (0.5 / z)
        an_ref[hd, :, tm:2 * tm] = n
        bvals = jnp.exp(s2 - v2[0])
        for c in range(tm // LANE):
            br_ref[hd, c, :, 0:LANE] = bvals[:, c * LANE:(c + 1) * LANE]
            br_ref[hd, c, :, LANE:2 * LANE] = rank2[:, c * LANE:(c + 1) * LANE]
        return carry

    lax.fori_loop(0, PEER_HEADS, head, 0)


def _peer_stats(x, g, mod, rows_per_batch, wq, k1, k2, tm):
    m, d = x.shape
    tm = min(tm, m)
    bidx = lambda i: (i * tm) // rows_per_batch
    const = lambda shape: pl.BlockSpec(shape, lambda i: (0,) * len(shape))
    modspec = lambda k: pl.BlockSpec((1, 1, d), lambda i: (bidx(i) * N_MOD + k, 0, 0))
    return pl.pallas_call(
        _peer_stats_kernel,
        out_shape=(jax.ShapeDtypeStruct((d, m), BF16),
                   jax.ShapeDtypeStruct((PEER_HEADS, PEER_NKEYS, 2 * m), F32),
                   jax.ShapeDtypeStruct((PEER_HEADS, m // LANE, PEER_NKEYS, 2 * LANE), F32)),
        grid=(m // tm,),
        in_specs=[pl.BlockSpec((tm, d), lambda i: (i, 0)), const((1, d)), modspec(3), modspec(4),
                  const(wq.shape), const(k1.shape), const(k2.shape)],
        out_specs=(pl.BlockSpec((d, tm), lambda i: (0, i)),
                   pl.BlockSpec((PEER_HEADS, PEER_NKEYS, 2 * tm), lambda i: (0, 0, i)),
                   pl.BlockSpec((PEER_HEADS, tm // LANE, PEER_NKEYS, 2 * LANE), lambda i: (0, i, 0, 0))),
        scratch_shapes=[pltpu.VMEM((tm, PEER_HEADS * PEER_DKEY), BF16),
                        pltpu.VMEM((-(-len(PEER_PAIRS) // SUBLANE) * SUBLANE, tm), F32)],
        compiler_params=_params("parallel"),
        name="peer_topk",
    )(x, g, mod, mod, wq, k1, k2)


def _peer_expert_kernel(tt_ref, u_ref, vt_ref, an_ref, br_ref, x_ref, gate_ref, o_ref,
                        acc_ref, s_ref, g_ref, *, te):
    j = pl.program_id(1)
    d, tm = acc_ref.shape
    rows = te // PEER_NKEYS
    sub = PEER_NKEYS // 4
    group = 2

    @pl.when(j == 0)
    def _():
        acc_ref[...] = jnp.zeros(acc_ref.shape, F32)

    halves = [slice(h0, h0 + MXU_COLS) for h0 in range(0, tm, MXU_COLS)]
    for hs in halves:
        s_ref[:, hs] = jnp.dot(u_ref[...], tt_ref[:, hs], preferred_element_type=F32)
    subs = [slice(k0, k0 + sub) for k0 in range(0, PEER_NKEYS, sub)]
    stm = min(PEER_STATS_TM, tm)
    for c in range(tm // LANE):
        cs = slice(c * LANE, (c + 1) * LANE)
        a0 = (c * LANE // stm) * 2 * stm + (c * LANE) % stm
        for r0 in range(0, rows, group):
            w = [[None] * len(subs) for _ in range(group)]
            for hd in range(PEER_HEADS):
                a_rows = [an_ref[hd, r0 + q:r0 + q + 1, a0:a0 + LANE] for q in range(group)]
                n_rows = [an_ref[hd, r0 + q:r0 + q + 1, a0 + stm:a0 + stm + LANE] for q in range(group)]
                for k, ks in enumerate(subs):
                    bv, rk = br_ref[hd, c, ks, 0:LANE], br_ref[hd, c, ks, LANE:2 * LANE]
                    for q in range(group):
                        term = a_rows[q] * jnp.where(rk < n_rows[q], bv, 0.0)
                        w[q][k] = term if w[q][k] is None else w[q][k] + term
            for q in range(group):
                for k, ks in enumerate(subs):
                    es = slice((r0 + q) * PEER_NKEYS + ks.start, (r0 + q) * PEER_NKEYS + ks.stop)
                    sc = s_ref[es, cs]
                    g_ref[es, cs] = (sc * (1.0 + lax.erf(sc * (2.0 ** -0.5))) * w[q][k]).astype(BF16)
    for hs in halves:
        acc_ref[:, hs] += jnp.dot(vt_ref[0], g_ref[:, hs], preferred_element_type=F32)

    @pl.when(j == pl.num_programs(1) - 1)
    def _():
        o_ref[...] = x_ref[...] + gate_ref[0] * acc_ref[...].T


def _peer_experts(tt, u, vt, an, br, x, mod, k_gate, rows_per_batch, tm, te):
    m, d = x.shape
    tm = min(tm, m)
    ne = u.shape[0]
    bidx = lambda i: (i * tm) // rows_per_batch
    rows = te // PEER_NKEYS
    return pl.pallas_call(
        functools.partial(_peer_expert_kernel, te=te),
        out_shape=jax.ShapeDtypeStruct((m, d), F32),
        grid=(m // tm, ne // te),
        in_specs=[
            pl.BlockSpec((d, tm), lambda i, j: (0, i)),
            pl.BlockSpec((te, d), lambda i, j: (j, 0)),
            pl.BlockSpec((1, d, te), lambda i, j: (j, 0, 0)),
            pl.BlockSpec((PEER_HEADS, rows, 2 * tm), lambda i, j: (0, j, i)),
            pl.BlockSpec((PEER_HEADS, tm // LANE, PEER_NKEYS, 2 * LANE), lambda i, j: (0, i, 0, 0)),
            pl.BlockSpec((tm, d), lambda i, j: (i, 0)),
            pl.BlockSpec((1, 1, d), lambda i, j: (bidx(i) * N_MOD + k_gate, 0, 0)),
        ],
        out_specs=pl.BlockSpec((tm, d), lambda i, j: (i, 0)),
        scratch_shapes=[pltpu.VMEM((d, tm), F32), pltpu.VMEM((te, tm), F32), pltpu.VMEM((te, tm), BF16)],
        compiler_params=_params("parallel", "arbitrary"),
        name="peer_experts",
    )(tt, u, vt, an, br, x, mod)


def _final_norm_kernel(x_ref, g_ref, o_ref):
    o_ref[...] = _rms(x_ref[...], g_ref[...])


def _final_norm(x, g, tm):
    m, d = x.shape
    tm = min(tm, m)
    return pl.pallas_call(
        _final_norm_kernel,
        out_shape=jax.ShapeDtypeStruct((m, d), F32),
        grid=(m // tm,),
        in_specs=[pl.BlockSpec((tm, d), lambda i: (i, 0)), pl.BlockSpec((1, d), lambda i: (0, 0))],
        out_specs=pl.BlockSpec((tm, d), lambda i: (i, 0)),
        compiler_params=_params("parallel"),
        name="final_rmsnorm",
    )(x, g)


def _rope_tables(n_lat, n_ctx):
    rows = n_lat // GRID_W
    row = jnp.broadcast_to(jnp.arange(rows, dtype=F32)[:, None], (rows, GRID_W)).reshape(-1)
    col = jnp.broadcast_to(jnp.arange(GRID_W, dtype=F32)[None, :], (rows, GRID_W)).reshape(-1)
    n_freq = MLA_ROPE // 4
    inv = ROPE_BASE ** (-jnp.arange(n_freq, dtype=F32) / n_freq)
    ang = jnp.concatenate([row[:, None] * inv, col[:, None] * inv], axis=-1)
    cos, sin = jnp.cos(ang), jnp.sin(ang)
    pad = jnp.zeros((n_lat, LANE - MLA_ROPE), F32)
    cos_l = jnp.concatenate([cos, cos, pad], axis=-1)
    sin_l = jnp.concatenate([-sin, sin, pad], axis=-1)
    cos_c = jnp.concatenate([jnp.ones((n_ctx, MLA_ROPE), F32), jnp.zeros((n_ctx, LANE - MLA_ROPE), F32)], axis=-1)
    return (cos_l, sin_l), (cos_c, jnp.zeros((n_ctx, LANE), F32))


def _swap_halves(w):
    half = w.shape[-1] // 2
    return jnp.concatenate([w[..., half:], w[..., :half]], axis=-1)


def _prep_hybrid(w_in, conv_w, conv_b, a_log, dt_bias, d_skip, norm_g, q_norm_g, w_qb, kv_norm_g, w_kvb, w_out):
    d = w_in.shape[0]
    o = 0
    pieces = {}
    for name, width in (("z", SSD_INNER), ("xbc", SSD_CONV_DIM), ("dt", SSD_HEADS), ("cq", MLA_Q_RANK),
                        ("ckv", MLA_KV_RANK), ("kr", MLA_ROPE)):
        pieces[name] = w_in[:, o:o + width]
        o += width
    zeros = lambda n: jnp.zeros((d, n), w_in.dtype)
    w_hyb = jnp.concatenate([
        pieces["z"], pieces["kr"], pieces["dt"], zeros(LANE - MLA_ROPE - SSD_HEADS), pieces["cq"], pieces["xbc"],
        pieces["ckv"], _swap_halves(pieces["kr"]), zeros(LANE - MLA_ROPE)], axis=1).astype(BF16)
    conv_w8 = jnp.concatenate([conv_w.T, jnp.zeros((SUBLANE - SSD_CONV, SSD_CONV_DIM), F32)], axis=0)
    lane_row = lambda v: jnp.zeros((1, LANE), F32).at[0, DT_LANE0:DT_LANE0 + SSD_HEADS].set(v)
    dirs = [(lane_row(dt_bias[k]), lane_row(-jnp.exp(a_log[k]))) for k in range(2)]
    expand = jnp.zeros((LANE, SSD_INNER), F32).at[DT_LANE0:DT_LANE0 + SSD_HEADS].set(
        jnp.repeat(jnp.eye(SSD_HEADS, dtype=F32), SSD_HEAD_DIM, axis=1)).astype(BF16)
    wq3 = w_qb.reshape(MLA_Q_RANK, MLA_HEADS, MLA_NOPE + MLA_ROPE)
    zq = jnp.zeros((MLA_Q_RANK, MLA_HEADS, LANE - MLA_ROPE), w_qb.dtype)
    q_main = jnp.concatenate([wq3, zq], axis=-1).reshape(MLA_Q_RANK, MLA_HEADS * MLA_QK_PAD)
    q_swap = jnp.concatenate([_swap_halves(wq3[..., MLA_NOPE:]), zq], axis=-1).reshape(MLA_Q_RANK, MLA_HEADS * LANE)
    wq = jnp.concatenate([q_main, q_swap], axis=1).astype(BF16)
    wkv3 = w_kvb.reshape(MLA_KV_RANK, MLA_HEADS, MLA_NOPE + MLA_V)
    wkv = jnp.concatenate([wkv3[..., :MLA_NOPE].reshape(MLA_KV_RANK, -1), wkv3[..., MLA_NOPE:].reshape(MLA_KV_RANK, -1)],
                          axis=1).astype(BF16)
    return dict(
        w_hyb=w_hyb, conv_w8=conv_w8, conv_b=conv_b[None, :], dirs=dirs, expand=expand,
        dskip=jnp.repeat(d_skip, SSD_HEAD_DIM)[None, :], normg=norm_g[None, :],
        q_norm_g=q_norm_g[None, :], wq=wq, kv_norm_g=kv_norm_g[None, :], wkv=wkv,
        w_out1=w_out[:SSD_INNER].astype(BF16), w_out2=w_out[SSD_INNER:].astype(BF16))


def _hybrid_layer(xl, xc, batch, g1, mod_l, mod_c, hp, ropes, update_ctx):
    n_lat, n_ctx = xl.shape[0] // batch, xc.shape[0] // batch
    (cos_l, sin_l), (cos_c, sin_c) = ropes
    q = SSD_CHUNK
    tri_f = jnp.tril(jnp.ones((q, q), F32))
    tri_b = jnp.triu(jnp.ones((q, q), F32))
    pl_ = _nmm(xl, g1, mod_l, 0, 1, n_lat, hp["w_hyb"], 1024, HYB_COLS // 3)
    pc_ = _nmm(xc, g1, mod_c, 0, 1, xc.shape[0], hp["w_hyb"], 256, HYB_COLS // 3)
    pl3, pc3 = pl_.reshape(batch, n_lat, HYB_COLS), pc_.reshape(batch, n_ctx, HYB_COLS)
    xbc_l = _conv_silu(pl3, hp["conv_w8"], hp["conv_b"], 512)
    xbc_c = _conv_silu(pc3, hp["conv_w8"], hp["conv_b"], 512)
    h0 = jnp.zeros((batch, SSD_STATE, SSD_INNER), F32)
    (bias0, a0), (bias1, a1) = hp["dirs"]
    yc0, hc0 = _ssd_scan(xbc_c, pc3, h0, tri_f, hp["expand"], bias0, a0, False)
    yl0, _ = _ssd_scan(xbc_l, pl3, hc0, tri_f, hp["expand"], bias0, a0, False)
    fin = lambda y0: (y0, hp["dskip"], hp["normg"])
    sc, hc1 = _ssd_scan(xbc_c, pc3, h0, tri_b, hp["expand"], bias1, a1, True, fin(yc0))
    sl, _ = _ssd_scan(xbc_l, pl3, hc1, tri_b, hp["expand"], bias1, a1, True, fin(yl0))
    ql = _qproj(pl_, hp["q_norm_g"], hp["wq"], cos_l, sin_l, n_lat, 1024)
    kl, vl = _kvproj(pl_, hp["kv_norm_g"], hp["wkv"], cos_l, sin_l, n_lat, 1024)
    kc, vc = _kvproj(pc_, hp["kv_norm_g"], hp["wkv"], cos_c, sin_c, n_ctx, 256)
    att_l = _attention(ql, kc, vc, kl, vl, batch, 2048, 256)
    xl_new = _outproj(sl.reshape(-1, SSD_INNER), att_l, hp["w_out1"], hp["w_out2"], xl, mod_l, 2, n_lat, 1024)
    xc_new = xc
    if update_ctx:
        qc = _qproj(pc_, hp["q_norm_g"], hp["wq"], cos_c, sin_c, n_ctx, 256)
        att_c = _attention(qc, kc, vc, None, None, batch, 256, 256)
        xc_new = _outproj(sc.reshape(-1, SSD_INNER), att_c, hp["w_out1"], hp["w_out2"], xc, mod_c, 2, xc.shape[0], 256)
    return xl_new, xc_new


def _peer_layer(x, g2, mod, rows_per_batch, pp):
    tt, an, br = _peer_stats(x, g2, mod, rows_per_batch, pp["wq"], pp["k1"], pp["k2"], PEER_STATS_TM)
    return _peer_experts(tt, pp["u"], pp["vt"], an, br, x, mod, 5, rows_per_batch, PEER_TM, PEER_TE)


def kernel(x, c, ctx, c_ctx, ada_w, ada_b, norm1_g, norm2_g, hyb_w_in, ssd_conv_w, ssd_conv_b, ssd_a_log,
           ssd_dt_bias, ssd_d, ssd_norm_g, mla_q_norm_g, mla_w_qb, mla_kv_norm_g, mla_w_kvb, hyb_w_out, gm_w_in,
           gm_ln_g, gm_ln_b, gm_ws, gm_bs, gm_w_out, peer_wq, peer_k1, peer_k2, peer_u, peer_v, final_norm_g):
    batch, n_lat, d = x.shape
    n_ctx = ctx.shape[1]
    depth = ada_w.shape[0]
    xl = x.reshape(batch * n_lat, d)
    xc = ctx.reshape(batch * n_ctx, d)
    n_cond = -(-(batch + 1) // SUBLANE) * SUBLANE
    conds = jnp.zeros((n_cond, d), F32).at[:batch].set(c).at[batch].set(c_ctx)
    mods = _mods(conds, ada_w, ada_b)
    ropes = _rope_tables(n_lat, n_ctx)
    for layer in range(depth):
        i = layer // 2
        even = layer % 2 == 0
        keep_ctx = any(j % 2 == 0 for j in range(layer + 1, depth))
        mod_l = mods[layer, :batch].reshape(batch * N_MOD, 1, d)
        mod_c = mods[layer, batch].reshape(N_MOD, 1, d)
        g1, g2 = norm1_g[layer][None, :], norm2_g[layer][None, :]
        if even:
            hp = _prep_hybrid(hyb_w_in[i], ssd_conv_w[i], ssd_conv_b[i], ssd_a_log[i], ssd_dt_bias[i], ssd_d[i],
                              ssd_norm_g[i], mla_q_norm_g[i], mla_w_qb[i], mla_kv_norm_g[i], mla_w_kvb[i], hyb_w_out[i])
            xl, xc = _hybrid_layer(xl, xc, batch, g1, mod_l, mod_c, hp, ropes, keep_ctx)
        else:
            gp = (gm_w_in[i].astype(BF16), gm_ln_g[i][None, :], gm_ln_b[i][None, :], gm_ws[i].astype(BF16),
                  jnp.repeat(gm_bs[i].T, GM_INNER // GM_GROUPS, axis=1), gm_w_out[i].astype(BF16))
            xl = _gmlp(xl, g1, mod_l, n_lat, *gp, 512)
            if keep_ctx:
                xc = _gmlp(xc, g1, mod_c, xc.shape[0], *gp, 256)
        pp = dict(wq=peer_wq[layer].astype(BF16), k1=peer_k1[layer].astype(BF16), k2=peer_k2[layer].astype(BF16),
                  u=peer_u[layer].astype(BF16),
                  vt=jnp.transpose(peer_v[layer].astype(BF16).reshape(-1, PEER_TE, d), (0, 2, 1)))
        xl = _peer_layer(xl, g2, mod_l, n_lat, pp)
        if keep_ctx:
            xc = _peer_layer(xc, g2, mod_c, xc.shape[0], pp)
    return _final_norm(xl, final_norm_g[None, :], 512).reshape(batch, n_lat, d)
```

```python
import functools

import jax
import jax.numpy as jnp
from jax import lax
from jax.experimental import pallas as pl
from jax.experimental.pallas import tpu as pltpu

F32, BF16 = jnp.float32, jnp.bfloat16
EPS = 1e-6
LANE = 128
SUBLANE = 8
MXU_COLS = 256
VMEM_LIMIT = 56 * 1024 * 1024

D_MODEL = 1024
DEPTH = 4
GRID_W = 64
N_MOD = 6
SSD_HEADS = 16
SSD_HEAD_DIM = 64
SSD_INNER = SSD_HEADS * SSD_HEAD_DIM
SSD_GROUPS = 2
SSD_STATE = 128
SSD_BC = SSD_GROUPS * SSD_STATE
SSD_CONV = 5
SSD_CONV_DIM = SSD_INNER + 2 * SSD_BC
SSD_CHUNK = 128
MLA_HEADS = 8
MLA_Q_RANK = 384
MLA_KV_RANK = 256
MLA_NOPE = 128
MLA_ROPE = 64
MLA_V = 128
MLA_SCALE = (MLA_NOPE + MLA_ROPE) ** -0.5
EXP2_SCALE = MLA_SCALE * 1.4426950408889634
ROPE_BASE = 10000.0
GM_CHUNK = 128
GM_INNER = 2 * D_MODEL
GM_GROUPS = 8
PEER_HEADS = 8
PEER_NKEYS = 128
PEER_EXPERTS = PEER_NKEYS * PEER_NKEYS
PEER_DKEY = 256
PEER_TOPK = 16
PEER_TE = 1024
PEER_TM = 512
PEER_STATS_TM = 512
PEER_AN_SEG = 256

HYB_Z0 = 0
HYB_MISC0 = 1024
HYB_CQ0 = 1152
HYB_XBC0 = 1536
HYB_CKV0 = 3072
HYB_MISC2 = 3328
HYB_COLS = 3456
DT_LANE0 = MLA_ROPE
MLA_QK_PAD = 256

PEER_PAIRS = tuple((a, b) for a in range(PEER_TOPK) for b in range(PEER_TOPK) if (a + 1) * (b + 1) <= PEER_TOPK)


def _params(*sem, flags=None):
    return pltpu.CompilerParams(dimension_semantics=sem, vmem_limit_bytes=VMEM_LIMIT, flags=flags)


def _rms(x, g):
    return x * lax.rsqrt(jnp.mean(x * x, axis=-1, keepdims=True) + EPS) * g


def _silu(x):
    return x * jax.nn.sigmoid(x)


def _gelu(x):
    return 0.5 * x * (1.0 + lax.erf(x * (2.0 ** -0.5)))


def _nt_dot(a, b):
    return lax.dot_general(a, b, (((1,), (1,)), ((), ())), preferred_element_type=F32)


def _split_dot(x, w, terms, left=False):
    acc = None
    rem = x
    for _ in range(terms):
        hi = rem.astype(BF16)
        part = jnp.dot(w, hi, preferred_element_type=F32) if left else jnp.dot(hi, w, preferred_element_type=F32)
        acc = part if acc is None else acc + part
        rem = rem - hi.astype(F32)
    return acc


def _mods_kernel(c_ref, w_ref, b_ref, o_ref):
    h = _silu(c_ref[...]).astype(BF16)
    o_ref[0] = jnp.dot(h, w_ref[0].astype(BF16), preferred_element_type=F32) + b_ref[0]


def _mods(conds, ada_w, ada_b):
    depth, d, n = ada_w.shape
    tn = 1536
    return pl.pallas_call(
        _mods_kernel,
        out_shape=jax.ShapeDtypeStruct((depth, conds.shape[0], n), F32),
        grid=(depth, n // tn),
        in_specs=[
            pl.BlockSpec(conds.shape, lambda l, j: (0, 0)),
            pl.BlockSpec((1, d, tn), lambda l, j: (l, 0, j)),
            pl.BlockSpec((1, 1, tn), lambda l, j: (l, 0, j)),
        ],
        out_specs=pl.BlockSpec((1, conds.shape[0], tn), lambda l, j: (l, 0, j)),
        compiler_params=_params("parallel", "parallel"),
        name="ada_mods",
    )(conds, ada_w, ada_b.reshape(depth, 1, n))


def _nmm_kernel(x_ref, g_ref, sh_ref, sc_ref, w_ref, o_ref, h_ref):
    @pl.when(pl.program_id(1) == 0)
    def _():
        y = _rms(x_ref[...], g_ref[...])
        h_ref[...] = (y * (1.0 + sc_ref[0]) + sh_ref[0]).astype(BF16)

    o_ref[...] = jnp.dot(h_ref[...], w_ref[...], preferred_element_type=F32).astype(o_ref.dtype)


def _nmm(x, g, mod, k_shift, k_scale, rows_per_batch, w, tm, tn):
    m, k = x.shape
    n = w.shape[1]
    tm = min(tm, m)
    bidx = lambda i: (i * tm) // rows_per_batch
    return pl.pallas_call(
        _nmm_kernel,
        out_shape=jax.ShapeDtypeStruct((m, n), F32),
        grid=(m // tm, n // tn),
        in_specs=[
            pl.BlockSpec((tm, k), lambda i, j: (i, 0)),
            pl.BlockSpec((1, k), lambda i, j: (0, 0)),
            pl.BlockSpec((1, 1, k), lambda i, j: (bidx(i) * N_MOD + k_shift, 0, 0)),
            pl.BlockSpec((1, 1, k), lambda i, j: (bidx(i) * N_MOD + k_scale, 0, 0)),
            pl.BlockSpec((k, tn), lambda i, j: (0, j)),
        ],
        out_specs=pl.BlockSpec((tm, tn), lambda i, j: (i, j)),
        scratch_shapes=[pltpu.VMEM((tm, k), BF16)],
        compiler_params=_params("parallel", "arbitrary"),
        name="norm_mod_matmul",
    )(x, g, mod, mod, w)


def _conv_kernel(cur_ref, prev_ref, next_ref, w_ref, b_ref, o_ref, buf_ref, *, tl):
    i = pl.program_id(1)
    halo = SUBLANE
    buf_ref[0:halo] = jnp.where(i > 0, prev_ref[0], 0.0)
    buf_ref[halo:halo + tl] = cur_ref[0]
    buf_ref[halo + tl:2 * halo + tl] = jnp.where(i < pl.num_programs(1) - 1, next_ref[0], 0.0)
    acc = jnp.broadcast_to(b_ref[...], (tl, b_ref.shape[1]))
    for k in range(SSD_CONV):
        start = halo + k - SSD_CONV // 2
        acc = acc + w_ref[k:k + 1, :] * buf_ref[start:start + tl, :]
    o_ref[0] = _silu(acc)


def _conv_silu(p3, conv_w8, conv_b, tl):
    b, l, _ = p3.shape
    c = SSD_CONV_DIM
    tl = min(tl, l)
    cb = HYB_XBC0 // c
    nh = l // SUBLANE
    r = tl // SUBLANE
    return pl.pallas_call(
        functools.partial(_conv_kernel, tl=tl),
        out_shape=jax.ShapeDtypeStruct((b, l, c), F32),
        grid=(b, l // tl),
        in_specs=[
            pl.BlockSpec((1, tl, c), lambda bb, i: (bb, i, cb)),
            pl.BlockSpec((1, SUBLANE, c), lambda bb, i: (bb, jnp.maximum(i * r - 1, 0), cb)),
            pl.BlockSpec((1, SUBLANE, c), lambda bb, i: (bb, jnp.minimum((i + 1) * r, nh - 1), cb)),
            pl.BlockSpec((SUBLANE, c), lambda bb, i: (0, 0)),
            pl.BlockSpec((1, c), lambda bb, i: (0, 0)),
        ],
        out_specs=pl.BlockSpec((1, tl, c), lambda bb, i: (bb, i, 0)),
        scratch_shapes=[pltpu.VMEM((tl + 2 * SUBLANE, c), F32)],
        compiler_params=_params("parallel", "parallel"),
        name="dwconv_silu",
    )(p3, p3, p3, conv_w8, conv_b)


def _ssd_kernel(xs_ref, bm_ref, cm_ref, misc_ref, h0_ref, tri_ref, e_ref, bias_ref, a_ref, *rest,
                reverse, finalize):
    if finalize:
        z_ref, y0_ref, dskip_ref, ng_ref, y_ref, hT_ref, st_ref, yb_ref = rest
    else:
        y_ref, hT_ref, st_ref, yb_ref = rest
    q = SSD_CHUNK
    gw = SSD_INNER // SSD_GROUPS
    hpg = SSD_HEADS // SSD_GROUPS

    @pl.when(pl.program_id(1) == 0)
    def _():
        st_ref[...] = h0_ref[0]

    xs = xs_ref[0]
    tri = tri_ref[...]
    visible = tri > 0.5
    lane = lax.broadcasted_iota(jnp.int32, (q, LANE), 1)
    dt_lanes = (lane >= DT_LANE0) & (lane < DT_LANE0 + SSD_HEADS)
    dt = jnp.where(dt_lanes, jax.nn.softplus(misc_ref[0] + bias_ref[...]), 0.0)
    da = dt * a_ref[...]
    cum = _split_dot(da, tri.astype(BF16), 3, left=True)
    cum_t = cum.T
    dt_t = dt.T
    edge = cum[0:1, :] if reverse else cum[q - 1:q, :]
    expcum = jnp.exp(cum)
    w_end = jnp.exp(edge - cum) * dt
    e = e_ref[...]
    expcum_x = _split_dot(expcum, e, 2)
    w_end_x = _split_dot(w_end, e, 2)
    xs_b = xs.astype(BF16)
    xw_b = (xs * w_end_x).astype(BF16)
    st = st_ref[...]
    st_b = st.astype(BF16)
    first_head = lane < SSD_HEAD_DIM
    decay_row = expcum_x[0:1, :] if reverse else expcum_x[q - 1:q, :]
    for g in range(SSD_GROUPS):
        bm = bm_ref[0][:, g * SSD_STATE:(g + 1) * SSD_STATE]
        cm_b = cm_ref[0][:, g * SSD_STATE:(g + 1) * SSD_STATE].astype(BF16)
        cb = _nt_dot(cm_b, bm.astype(BF16))
        gs = slice(g * gw, (g + 1) * gw)
        y_off = jnp.dot(cm_b, st_b[:, gs], preferred_element_type=F32) * expcum_x[:, gs]
        for jp in range(hpg // 2):
            c0 = (g * hpg + 2 * jp) * SSD_HEAD_DIM
            tiles = []
            for sub in range(2):
                col = DT_LANE0 + g * hpg + 2 * jp + sub
                seg = cum[:, col:col + 1] - cum_t[col:col + 1, :]
                lmat = jnp.exp(jnp.where(visible, seg, -jnp.inf))
                mm = (cb * lmat * dt_t[col:col + 1, :]).astype(BF16)
                tiles.append(jnp.dot(mm, xs_b[:, c0:c0 + LANE], preferred_element_type=F32))
            yb_ref[:, c0:c0 + LANE] = jnp.where(first_head, tiles[0], tiles[1]) + y_off[:, c0 - g * gw:c0 - g * gw + LANE]
        new = jnp.dot(bm.T.astype(BF16), xw_b[:, gs], preferred_element_type=F32)
        st_ref[:, gs] = st[:, gs] * decay_row[:, gs] + new

    if finalize:
        v = (yb_ref[...] + y0_ref[0] + dskip_ref[...] * xs) * _silu(z_ref[0])
        for g in range(SSD_GROUPS):
            gs = slice(g * gw, (g + 1) * gw)
            y_ref[0, :, gs] = _rms(v[:, gs], ng_ref[:, gs]).astype(y_ref.dtype)
    else:
        y_ref[0] = yb_ref[...]

    @pl.when(pl.program_id(1) == pl.num_programs(1) - 1)
    def _():
        hT_ref[0] = st_ref[...]


def _ssd_scan(xbc3, p3, h0, tri, expand, bias_row, a_row, reverse, fin=None):
    b, l, _ = xbc3.shape
    nc = l // SSD_CHUNK
    cidx = (lambda s: nc - 1 - s) if reverse else (lambda s: s)
    q = SSD_CHUNK
    in_specs = [
        pl.BlockSpec((1, q, SSD_INNER), lambda bb, s: (bb, cidx(s), 0)),
        pl.BlockSpec((1, q, SSD_BC), lambda bb, s: (bb, cidx(s), SSD_INNER // SSD_BC)),
        pl.BlockSpec((1, q, SSD_BC), lambda bb, s: (bb, cidx(s), SSD_INNER // SSD_BC + 1)),
        pl.BlockSpec((1, q, LANE), lambda bb, s: (bb, cidx(s), HYB_MISC0 // LANE)),
        pl.BlockSpec((1, SSD_STATE, SSD_INNER), lambda bb, s: (bb, 0, 0)),
        pl.BlockSpec((q, q), lambda bb, s: (0, 0)),
        pl.BlockSpec((LANE, SSD_INNER), lambda bb, s: (0, 0)),
        pl.BlockSpec((1, LANE), lambda bb, s: (0, 0)),
        pl.BlockSpec((1, LANE), lambda bb, s: (0, 0)),
    ]
    args = [xbc3, xbc3, xbc3, p3, h0, tri, expand, bias_row, a_row]
    if fin is not None:
        y0, dskip_row, normg_row = fin
        in_specs += [
            pl.BlockSpec((1, q, SSD_INNER), lambda bb, s: (bb, cidx(s), HYB_Z0 // SSD_INNER)),
            pl.BlockSpec((1, q, SSD_INNER), lambda bb, s: (bb, cidx(s), 0)),
            pl.BlockSpec((1, SSD_INNER), lambda bb, s: (0, 0)),
            pl.BlockSpec((1, SSD_INNER), lambda bb, s: (0, 0)),
        ]
        args += [p3, y0, dskip_row, normg_row]
    return pl.pallas_call(
        functools.partial(_ssd_kernel, reverse=reverse, finalize=fin is not None),
        out_shape=(jax.ShapeDtypeStruct((b, l, SSD_INNER), BF16 if fin is not None else F32),
                   jax.ShapeDtypeStruct((b, SSD_STATE, SSD_INNER), F32)),
        grid=(b, nc),
        in_specs=in_specs,
        out_specs=(pl.BlockSpec((1, q, SSD_INNER), lambda bb, s: (bb, cidx(s), 0)),
                   pl.BlockSpec((1, SSD_STATE, SSD_INNER), lambda bb, s: (bb, 0, 0))),
        scratch_shapes=[pltpu.VMEM((SSD_STATE, SSD_INNER), F32), pltpu.VMEM((q, SSD_INNER), F32)],
        compiler_params=_params("parallel", "arbitrary"),
        name="ssd_scan_bwd" if reverse else "ssd_scan_fwd",
    )(*args)


def _qproj_kernel(x_ref, g_ref, w_ref, cos_ref, sin_ref, o_ref):
    h = _rms(x_ref[...], g_ref[...]).astype(BF16)
    r = jnp.dot(h, w_ref[...], preferred_element_type=F32)
    main = MLA_HEADS * MLA_QK_PAD
    cos, sin = cos_ref[...], sin_ref[...]
    for hd in range(MLA_HEADS):
        c0 = hd * MLA_QK_PAD
        o_ref[:, c0:c0 + LANE] = r[:, c0:c0 + LANE].astype(o_ref.dtype)
        swapped = r[:, main + hd * LANE:main + (hd + 1) * LANE]
        o_ref[:, c0 + LANE:c0 + 2 * LANE] = (r[:, c0 + LANE:c0 + 2 * LANE] * cos + swapped * sin).astype(o_ref.dtype)


def _qproj(p, g, w, cos, sin, rows_per_batch, tm):
    m = p.shape[0]
    tm = min(tm, m, rows_per_batch)
    nb = rows_per_batch // tm
    n_out = MLA_HEADS * MLA_QK_PAD
    return pl.pallas_call(
        _qproj_kernel,
        out_shape=jax.ShapeDtypeStruct((m, n_out), BF16),
        grid=(m // tm,),
        in_specs=[
            pl.BlockSpec((tm, MLA_Q_RANK), lambda i: (i, HYB_CQ0 // MLA_Q_RANK)),
            pl.BlockSpec((1, MLA_Q_RANK), lambda i: (0, 0)),
            pl.BlockSpec(w.shape, lambda i: (0, 0)),
            pl.BlockSpec((tm, LANE), lambda i: (i % nb, 0)),
            pl.BlockSpec((tm, LANE), lambda i: (i % nb, 0)),
        ],
        out_specs=pl.BlockSpec((tm, n_out), lambda i: (i, 0)),
        compiler_params=_params("parallel"),
        name="mla_q_proj",
    )(p, g, w, cos, sin)


def _kvproj_kernel(x_ref, misc_ref, misc2_ref, g_ref, w_ref, cos_ref, sin_ref, k_ref, v_ref):
    h = _rms(x_ref[...], g_ref[...]).astype(BF16)
    r = jnp.dot(h, w_ref[...], preferred_element_type=F32)
    kr = (misc_ref[...] * cos_ref[...] + misc2_ref[...] * sin_ref[...]).astype(k_ref.dtype)
    for hd in range(MLA_HEADS):
        c0 = hd * MLA_QK_PAD
        k_ref[:, c0:c0 + LANE] = r[:, hd * MLA_NOPE:(hd + 1) * MLA_NOPE].astype(k_ref.dtype)
        k_ref[:, c0 + LANE:c0 + 2 * LANE] = kr
    v_ref[...] = r[:, MLA_HEADS * MLA_NOPE:].astype(v_ref.dtype)


def _kvproj(p, g, w, cos, sin, rows_per_batch, tm):
    m = p.shape[0]
    tm = min(tm, m, rows_per_batch)
    nb = rows_per_batch // tm
    nk, nv = MLA_HEADS * MLA_QK_PAD, MLA_HEADS * MLA_V
    return pl.pallas_call(
        _kvproj_kernel,
        out_shape=(jax.ShapeDtypeStruct((m, nk), BF16), jax.ShapeDtypeStruct((m, nv), BF16)),
        grid=(m // tm,),
        in_specs=[
            pl.BlockSpec((tm, MLA_KV_RANK), lambda i: (i, HYB_CKV0 // MLA_KV_RANK)),
            pl.BlockSpec((tm, LANE), lambda i: (i, HYB_MISC0 // LANE)),
            pl.BlockSpec((tm, LANE), lambda i: (i, HYB_MISC2 // LANE)),
            pl.BlockSpec((1, MLA_KV_RANK), lambda i: (0, 0)),
            pl.BlockSpec(w.shape, lambda i: (0, 0)),
            pl.BlockSpec((tm, LANE), lambda i: (i % nb, 0)),
            pl.BlockSpec((tm, LANE), lambda i: (i % nb, 0)),
        ],
        out_specs=(pl.BlockSpec((tm, nk), lambda i: (i, 0)), pl.BlockSpec((tm, nv), lambda i: (i, 0))),
        compiler_params=_params("parallel"),
        name="mla_kv_proj",
    )(p, p, p, g, w, cos, sin)


def _attn_kernel(q_ref, kc_ref, vc_ref, *rest, with_latent):
    if with_latent:
        k_ref, v_ref, o_ref, m_ref, l_ref, acc_ref = rest
    else:
        o_ref, m_ref, l_ref, acc_ref = rest
    ki = pl.program_id(2)

    def update(hd, kblk, vblk):
        s = _nt_dot(q_ref[:, hd * MLA_QK_PAD:(hd + 1) * MLA_QK_PAD], kblk)
        tiles = s.shape[1] // LANE
        m_prev = m_ref[hd]
        m_new = jnp.maximum(m_prev, jnp.max(s, axis=-1, keepdims=True))
        alpha = jnp.exp2((m_prev - m_new) * EXP2_SCALE)
        p = jnp.exp2((s - jnp.tile(m_new, (1, tiles))) * EXP2_SCALE)
        psum = p[:, 0:LANE]
        for c in range(1, tiles):
            psum = psum + p[:, c * LANE:(c + 1) * LANE]
        l_ref[hd] = alpha * l_ref[hd] + psum
        vs = slice(hd * MLA_V, (hd + 1) * MLA_V)
        acc_ref[:, vs] = alpha * acc_ref[:, vs] + jnp.dot(p.astype(BF16), vblk, preferred_element_type=F32)
        m_ref[hd] = m_new

    @pl.when(ki == 0)
    def _():
        m_ref[...] = jnp.full(m_ref.shape, -jnp.inf, F32)
        l_ref[...] = jnp.zeros(l_ref.shape, F32)
        acc_ref[...] = jnp.zeros(acc_ref.shape, F32)
        for hd in range(MLA_HEADS):
            update(hd, kc_ref[:, hd * MLA_QK_PAD:(hd + 1) * MLA_QK_PAD], vc_ref[:, hd * MLA_V:(hd + 1) * MLA_V])

    if with_latent:
        for hd in range(MLA_HEADS):
            update(hd, k_ref[:, hd * MLA_QK_PAD:(hd + 1) * MLA_QK_PAD], v_ref[:, hd * MLA_V:(hd + 1) * MLA_V])

    @pl.when(ki == pl.num_programs(2) - 1)
    def _():
        for hd in range(MLA_HEADS):
            vs = slice(hd * MLA_V, (hd + 1) * MLA_V)
            o_ref[:, vs] = (acc_ref[:, vs] / jnp.sum(l_ref[hd], axis=-1, keepdims=True)).astype(o_ref.dtype)


def _attention(q, kc, vc, k, v, batch, tq, tk):
    m = q.shape[0]
    lq = m // batch
    lc = kc.shape[0] // batch
    tq = min(tq, lq)
    nq = lq // tq
    nq_pad, nv = MLA_HEADS * MLA_QK_PAD, MLA_HEADS * MLA_V
    in_specs = [
        pl.BlockSpec((tq, nq_pad), lambda b, i, j: (b * nq + i, 0)),
        pl.BlockSpec((lc, nq_pad), lambda b, i, j: (b, 0)),
        pl.BlockSpec((lc, nv), lambda b, i, j: (b, 0)),
    ]
    args = [q, kc, vc]
    nk = 1
    if k is not None:
        lk = k.shape[0] // batch
        tk = min(tk, lk)
        nk = lk // tk
        in_specs += [
            pl.BlockSpec((tk, nq_pad), lambda b, i, j: (b * nk + j, 0)),
            pl.BlockSpec((tk, nv), lambda b, i, j: (b * nk + j, 0)),
        ]
        args += [k, v]
    return pl.pallas_call(
        functools.partial(_attn_kernel, with_latent=k is not None),
        out_shape=jax.ShapeDtypeStruct((m, nv), BF16),
        grid=(batch, nq, nk),
        in_specs=in_specs,
        out_specs=pl.BlockSpec((tq, nv), lambda b, i, j: (b * nq + i, 0)),
        scratch_shapes=[pltpu.VMEM((MLA_HEADS, tq, LANE), F32), pltpu.VMEM((MLA_HEADS, tq, LANE), F32),
                        pltpu.VMEM((tq, nv), F32)],
        compiler_params=_params("parallel", "parallel", "arbitrary"),
        name="mla_attention",
    )(*args)


def _outproj_kernel(a1_ref, a2_ref, w1_ref, w2_ref, x_ref, gate_ref, o_ref):
    acc = jnp.dot(a1_ref[...], w1_ref[...], preferred_element_type=F32)
    acc = acc + jnp.dot(a2_ref[...], w2_ref[...], preferred_element_type=F32)
    o_ref[...] = x_ref[...] + gate_ref[0] * acc


def _outproj(a1, a2, w1, w2, x, mod, k_gate, rows_per_batch, tm):
    m, d = x.shape
    tm = min(tm, m)
    bidx = lambda i: (i * tm) // rows_per_batch
    return pl.pallas_call(
        _outproj_kernel,
        out_shape=jax.ShapeDtypeStruct((m, d), F32),
        grid=(m // tm,),
        in_specs=[
            pl.BlockSpec((tm, a1.shape[1]), lambda i: (i, 0)),
            pl.BlockSpec((tm, a2.shape[1]), lambda i: (i, 0)),
            pl.BlockSpec(w1.shape, lambda i: (0, 0)),
            pl.BlockSpec(w2.shape, lambda i: (0, 0)),
            pl.BlockSpec((tm, d), lambda i: (i, 0)),
            pl.BlockSpec((1, 1, d), lambda i: (bidx(i) * N_MOD + k_gate, 0, 0)),
        ],
        out_specs=pl.BlockSpec((tm, d), lambda i: (i, 0)),
        compiler_params=_params("parallel"),
        name="hyb_out_proj",
    )(a1, a2, w1, w2, x, mod)


def _gmlp_kernel(x_ref, g_ref, sh_ref, sc_ref, gate_ref, win_ref, lng_ref, lnb_ref, ws_ref, bs_ref, wout_ref,
                 o_ref, uv_ref):
    x = x_ref[...]
    tm = x.shape[0]
    h = (_rms(x, g_ref[...]) * (1.0 + sc_ref[0]) + sh_ref[0]).astype(BF16)
    uv = _gelu(jnp.dot(h, win_ref[...], preferred_element_type=F32))
    u, v = uv[:, :GM_INNER], uv[:, GM_INNER:]
    mu = jnp.mean(v, axis=-1, keepdims=True)
    var = jnp.mean(jnp.square(v - mu), axis=-1, keepdims=True)
    vn = ((v - mu) * lax.rsqrt(var + EPS) * lng_ref[...] + lnb_ref[...]).astype(BF16)
    gw = GM_INNER // GM_GROUPS
    for c in range(tm // GM_CHUNK):
        rs = slice(c * GM_CHUNK, (c + 1) * GM_CHUNK)
        for g in range(GM_GROUPS):
            cs = slice(g * gw, (g + 1) * gw)
            sv = jnp.dot(ws_ref[g], vn[rs, cs], preferred_element_type=F32) + bs_ref[:, cs]
            uv_ref[rs, cs] = (u[rs, cs] * sv).astype(BF16)
    o_ref[...] = x + gate_ref[0] * jnp.dot(uv_ref[...], wout_ref[...], preferred_element_type=F32)


def _gmlp(x, g, mod, rows_per_batch, w_in, ln_g, ln_b, ws, bs_full, w_out, tm):
    m, d = x.shape
    tm = min(tm, m)
    bidx = lambda i: (i * tm) // rows_per_batch
    const = lambda shape: pl.BlockSpec(shape, lambda i: (0,) * len(shape))
    modspec = lambda k: pl.BlockSpec((1, 1, d), lambda i: (bidx(i) * N_MOD + k, 0, 0))
    return pl.pallas_call(
        _gmlp_kernel,
        out_shape=jax.ShapeDtypeStruct((m, d), F32),
        grid=(m // tm,),
        in_specs=[
            pl.BlockSpec((tm, d), lambda i: (i, 0)), const((1, d)), modspec(0), modspec(1), modspec(2),
            const(w_in.shape), const((1, GM_INNER)), const((1, GM_INNER)), const(ws.shape), const(bs_full.shape),
            const(w_out.shape),
        ],
        out_specs=pl.BlockSpec((tm, d), lambda i: (i, 0)),
        scratch_shapes=[pltpu.VMEM((tm, GM_INNER), BF16)],
        compiler_params=_params("parallel"),
        name="chunk_gmlp",
    )(x, g, mod, mod, mod, w_in, ln_g, ln_b, ws, bs_full, w_out)


def _top_values(xs, k, want_rank):
    xs = list(xs)
    vals = [[] for _ in xs]
    ranks = [jnp.full(x.shape, float(k), F32) if w else None for x, w in zip(xs, want_rank)]
    for r in range(k):
        for i, x in enumerate(xs):
            mx = jnp.max(x, axis=0, keepdims=True)
            vals[i].append(mx)
            hit = x == mx
            if want_rank[i]:
                ranks[i] = jnp.where(hit, float(r), ranks[i])
            if r + 1 < k:
                xs[i] = jnp.where(hit, -jnp.inf, x)
    return vals, ranks


def _peer_stats_kernel(x_ref, g_ref, sh_ref, sc_ref, wq_ref, k1_ref, k2_ref, tt_ref, an_ref, br_ref,
                       q_ref, cand_ref):
    y = _rms(x_ref[...], g_ref[...]) * (1.0 + sc_ref[0]) + sh_ref[0]
    h = y.astype(BF16)
    tt_ref[...] = y.T.astype(BF16)
    q_ref[...] = jnp.dot(h, wq_ref[...], preferred_element_type=F32).astype(BF16)
    half = PEER_DKEY // 2
    tm = h.shape[0]
    n_pairs = len(PEER_PAIRS)
    cand_ref[n_pairs:, :] = jnp.full((cand_ref.shape[0] - n_pairs, tm), -jnp.inf, F32)

    def head(hd, carry):
        c0 = pl.multiple_of(hd * PEER_DKEY, PEER_DKEY)
        s1 = _nt_dot(k1_ref[hd], q_ref[:, pl.ds(c0, half)])
        s2 = _nt_dot(k2_ref[hd], q_ref[:, pl.ds(c0 + half, half)])
        (v1, v2), (_, rank2) = _top_values((s1, s2), PEER_TOPK, (False, True))
        for idx, (a, b) in enumerate(PEER_PAIRS):
            cand_ref[idx:idx + 1, :] = v1[a] + v2[b]
        (tops,), _ = _top_values((cand_ref[...],), PEER_TOPK, (False,))
        thr = tops[PEER_TOPK - 1]
        z = tops[0] * 0.0
        for tv in tops:
            z = z + jnp.exp(tv - tops[0])
        counts = [None] * PEER_TOPK
        for a, b in PEER_PAIRS:
            hit = jnp.where(v1[a] + v2[b] >= thr, 1.0, 0.0)
            counts[a] = hit if counts[a] is None else counts[a] + hit
        n = jnp.zeros(s1.shape, F32)
        for a in range(PEER_TOPK):
            n = jnp.where(s1 == v1[a], counts[a], n)
        avals = jnp.exp(s1 - v1[0]) * (0.5 / z)
        seg = min(PEER_AN_SEG, tm)
        for sg in range(tm // seg):
            an_ref[hd, :, 2 * sg * seg:(2 * sg + 1) * seg] = avals[:, sg * seg:(sg + 1) * seg]
            an_ref[hd, :, (2 * sg + 1) * seg:(2 * sg + 2) * seg] = n[:, sg * seg:(sg + 1) * seg]
        bvals = jnp.exp(s2 - v2[0])
        for c in range(tm // LANE):
            br_ref[hd, c, :, 0:LANE] = bvals[:, c * LANE:(c + 1) * LANE]
            br_ref[hd, c, :, LANE:2 * LANE] = rank2[:, c * LANE:(c + 1) * LANE]
        return carry

    lax.fori_loop(0, PEER_HEADS, head, 0)


def _peer_stats(x, g, mod, rows_per_batch, wq, k1, k2, tm):
    m, d = x.shape
    tm = min(tm, m)
    bidx = lambda i: (i * tm) // rows_per_batch
    const = lambda shape: pl.BlockSpec(shape, lambda i: (0,) * len(shape))
    modspec = lambda k: pl.BlockSpec((1, 1, d), lambda i: (bidx(i) * N_MOD + k, 0, 0))
    return pl.pallas_call(
        _peer_stats_kernel,
        out_shape=(jax.ShapeDtypeStruct((d, m), BF16),
                   jax.ShapeDtypeStruct((PEER_HEADS, PEER_NKEYS, 2 * m), F32),
                   jax.ShapeDtypeStruct((PEER_HEADS, m // LANE, PEER_NKEYS, 2 * LANE), F32)),
        grid=(m // tm,),
        in_specs=[pl.BlockSpec((tm, d), lambda i: (i, 0)), const((1, d)), modspec(3), modspec(4),
                  const(wq.shape), const(k1.shape), const(k2.shape)],
        out_specs=(pl.BlockSpec((d, tm), lambda i: (0, i)),
                   pl.BlockSpec((PEER_HEADS, PEER_NKEYS, 2 * tm), lambda i: (0, 0, i)),
                   pl.BlockSpec((PEER_HEADS, tm // LANE, PEER_NKEYS, 2 * LANE), lambda i: (0, i, 0, 0))),
        scratch_shapes=[pltpu.VMEM((tm, PEER_HEADS * PEER_DKEY), BF16),
                        pltpu.VMEM((-(-len(PEER_PAIRS) // SUBLANE) * SUBLANE, tm), F32)],
        compiler_params=_params("parallel"),
        name="peer_topk",
    )(x, g, mod, mod, wq, k1, k2)


def _peer_expert_kernel(tt_ref, u_ref, vt_ref, an_ref, br_ref, x_ref, gate_ref, o_ref,
                        acc_ref, s_ref, g_ref, *, te):
    j = pl.program_id(1)
    d, tm = acc_ref.shape
    rows = te // PEER_NKEYS
    sub = PEER_NKEYS // 4
    group = 2

    @pl.when(j == 0)
    def _():
        acc_ref[...] = jnp.zeros(acc_ref.shape, F32)

    halves = [slice(h0, h0 + MXU_COLS) for h0 in range(0, tm, MXU_COLS)]
    for hs in halves:
        s_ref[:, hs] = jnp.dot(u_ref[...], tt_ref[:, hs], preferred_element_type=F32)
    subs = [slice(k0, k0 + sub) for k0 in range(0, PEER_NKEYS, sub)]
    stm = min(PEER_AN_SEG, tm)
    for c in range(tm // LANE):
        cs = slice(c * LANE, (c + 1) * LANE)
        a0 = (c * LANE // stm) * 2 * stm + (c * LANE) % stm
        for r0 in range(0, rows, group):
            w = [[None] * len(subs) for _ in range(group)]
            for hd in range(PEER_HEADS):
                a_rows = [an_ref[hd, r0 + q:r0 + q + 1, a0:a0 + LANE] for q in range(group)]
                n_rows = [an_ref[hd, r0 + q:r0 + q + 1, a0 + stm:a0 + stm + LANE] for q in range(group)]
                for k, ks in enumerate(subs):
                    bv, rk = br_ref[hd, c, ks, 0:LANE], br_ref[hd, c, ks, LANE:2 * LANE]
                    for q in range(group):
                        term = a_rows[q] * jnp.where(rk < n_rows[q], bv, 0.0)
                        w[q][k] = term if w[q][k] is None else w[q][k] + term
            for q in range(group):
                for k, ks in enumerate(subs):
                    es = slice((r0 + q) * PEER_NKEYS + ks.start, (r0 + q) * PEER_NKEYS + ks.stop)
                    sc = s_ref[es, cs]
                    g_ref[es, cs] = (sc * (1.0 + lax.erf(sc * (2.0 ** -0.5))) * w[q][k]).astype(BF16)
    for hs in halves:
        acc_ref[:, hs] += jnp.dot(vt_ref[0], g_ref[:, hs], preferred_element_type=F32)

    @pl.when(j == pl.num_programs(1) - 1)
    def _():
        o_ref[...] = x_ref[...] + gate_ref[0] * acc_ref[...].T


def _peer_experts(tt, u, vt, an, br, x, mod, k_gate, rows_per_batch, tm, te):
    m, d = x.shape
    tm = min(tm, m)
    ne = u.shape[0]
    bidx = lambda i: (i * tm) // rows_per_batch
    rows = te // PEER_NKEYS
    return pl.pallas_call(
        functools.partial(_peer_expert_kernel, te=te),
        out_shape=jax.ShapeDtypeStruct((m, d), F32),
        grid=(m // tm, ne // te),
        in_specs=[
            pl.BlockSpec((d, tm), lambda i, j: (0, i)),
            pl.BlockSpec((te, d), lambda i, j: (j, 0)),
            pl.BlockSpec((1, d, te), lambda i, j: (j, 0, 0)),
            pl.BlockSpec((PEER_HEADS, rows, 2 * tm), lambda i, j: (0, j, i)),
            pl.BlockSpec((PEER_HEADS, tm // LANE, PEER_NKEYS, 2 * LANE), lambda i, j: (0, i, 0, 0)),
            pl.BlockSpec((tm, d), lambda i, j: (i, 0)),
            pl.BlockSpec((1, 1, d), lambda i, j: (bidx(i) * N_MOD + k_gate, 0, 0)),
        ],
        out_specs=pl.BlockSpec((tm, d), lambda i, j: (i, 0)),
        scratch_shapes=[pltpu.VMEM((d, tm), F32), pltpu.VMEM((te, tm), F32), pltpu.VMEM((te, tm), BF16)],
        compiler_params=_params("parallel", "arbitrary"),
        name="peer_experts",
    )(tt, u, vt, an, br, x, mod)


def _final_norm_kernel(x_ref, g_ref, o_ref):
    o_ref[...] = _rms(x_ref[...], g_ref[...])


def _final_norm(x, g, tm):
    m, d = x.shape
    tm = min(tm, m)
    return pl.pallas_call(
        _final_norm_kernel,
        out_shape=jax.ShapeDtypeStruct((m, d), F32),
        grid=(m // tm,),
        in_specs=[pl.BlockSpec((tm, d), lambda i: (i, 0)), pl.BlockSpec((1, d), lambda i: (0, 0))],
        out_specs=pl.BlockSpec((tm, d), lambda i: (i, 0)),
        compiler_params=_params("parallel"),
        name="final_rmsnorm",
    )(x, g)


def _rope_tables(n_lat, n_ctx):
    rows = n_lat // GRID_W
    row = jnp.broadcast_to(jnp.arange(rows, dtype=F32)[:, None], (rows, GRID_W)).reshape(-1)
    col = jnp.broadcast_to(jnp.arange(GRID_W, dtype=F32)[None, :], (rows, GRID_W)).reshape(-1)
    n_freq = MLA_ROPE // 4
    inv = ROPE_BASE ** (-jnp.arange(n_freq, dtype=F32) / n_freq)
    ang = jnp.concatenate([row[:, None] * inv, col[:, None] * inv], axis=-1)
    cos, sin = jnp.cos(ang), jnp.sin(ang)
    pad = jnp.zeros((n_lat, LANE - MLA_ROPE), F32)
    cos_l = jnp.concatenate([cos, cos, pad], axis=-1)
    sin_l = jnp.concatenate([-sin, sin, pad], axis=-1)
    cos_c = jnp.concatenate([jnp.ones((n_ctx, MLA_ROPE), F32), jnp.zeros((n_ctx, LANE - MLA_ROPE), F32)], axis=-1)
    return (cos_l, sin_l), (cos_c, jnp.zeros((n_ctx, LANE), F32))


def _swap_halves(w):
    half = w.shape[-1] // 2
    return jnp.concatenate([w[..., half:], w[..., :half]], axis=-1)


def _prep_hybrid(w_in, conv_w, conv_b, a_log, dt_bias, d_skip, norm_g, q_norm_g, w_qb, kv_norm_g, w_kvb, w_out):
    d = w_in.shape[0]
    o = 0
    pieces = {}
    for name, width in (("z", SSD_INNER), ("xbc", SSD_CONV_DIM), ("dt", SSD_HEADS), ("cq", MLA_Q_RANK),
                        ("ckv", MLA_KV_RANK), ("kr", MLA_ROPE)):
        pieces[name] = w_in[:, o:o + width]
        o += width
    zeros = lambda n: jnp.zeros((d, n), w_in.dtype)
    w_hyb = jnp.concatenate([
        pieces["z"], pieces["kr"], pieces["dt"], zeros(LANE - MLA_ROPE - SSD_HEADS), pieces["cq"], pieces["xbc"],
        pieces["ckv"], _swap_halves(pieces["kr"]), zeros(LANE - MLA_ROPE)], axis=1).astype(BF16)
    conv_w8 = jnp.concatenate([conv_w.T, jnp.zeros((SUBLANE - SSD_CONV, SSD_CONV_DIM), F32)], axis=0)
    lane_row = lambda v: jnp.zeros((1, LANE), F32).at[0, DT_LANE0:DT_LANE0 + SSD_HEADS].set(v)
    dirs = [(lane_row(dt_bias[k]), lane_row(-jnp.exp(a_log[k]))) for k in range(2)]
    expand = jnp.zeros((LANE, SSD_INNER), F32).at[DT_LANE0:DT_LANE0 + SSD_HEADS].set(
        jnp.repeat(jnp.eye(SSD_HEADS, dtype=F32), SSD_HEAD_DIM, axis=1)).astype(BF16)
    wq3 = w_qb.reshape(MLA_Q_RANK, MLA_HEADS, MLA_NOPE + MLA_ROPE)
    zq = jnp.zeros((MLA_Q_RANK, MLA_HEADS, LANE - MLA_ROPE), w_qb.dtype)
    q_main = jnp.concatenate([wq3, zq], axis=-1).reshape(MLA_Q_RANK, MLA_HEADS * MLA_QK_PAD)
    q_swap = jnp.concatenate([_swap_halves(wq3[..., MLA_NOPE:]), zq], axis=-1).reshape(MLA_Q_RANK, MLA_HEADS * LANE)
    wq = jnp.concatenate([q_main, q_swap], axis=1).astype(BF16)
    wkv3 = w_kvb.reshape(MLA_KV_RANK, MLA_HEADS, MLA_NOPE + MLA_V)
    wkv = jnp.concatenate([wkv3[..., :MLA_NOPE].reshape(MLA_KV_RANK, -1), wkv3[..., MLA_NOPE:].reshape(MLA_KV_RANK, -1)],
                          axis=1).astype(BF16)
    return dict(
        w_hyb=w_hyb, conv_w8=conv_w8, conv_b=conv_b[None, :], dirs=dirs, expand=expand,
        dskip=jnp.repeat(d_skip, SSD_HEAD_DIM)[None, :], normg=norm_g[None, :],
        q_norm_g=q_norm_g[None, :], wq=wq, kv_norm_g=kv_norm_g[None, :], wkv=wkv,
        w_out1=w_out[:SSD_INNER].astype(BF16), w_out2=w_out[SSD_INNER:].astype(BF16))


def _hybrid_layer(xl, xc, batch, g1, mod_l, mod_c, hp, ropes, update_ctx):
    n_lat, n_ctx = xl.shape[0] // batch, xc.shape[0] // batch
    (cos_l, sin_l), (cos_c, sin_c) = ropes
    q = SSD_CHUNK
    tri_f = jnp.tril(jnp.ones((q, q), F32))
    tri_b = jnp.triu(jnp.ones((q, q), F32))
    pl_ = _nmm(xl, g1, mod_l, 0, 1, n_lat, hp["w_hyb"], 1024, HYB_COLS // 3)
    pc_ = _nmm(xc, g1, mod_c, 0, 1, xc.shape[0], hp["w_hyb"], 256, HYB_COLS // 3)
    pl3, pc3 = pl_.reshape(batch, n_lat, HYB_COLS), pc_.reshape(batch, n_ctx, HYB_COLS)
    xbc_l = _conv_silu(pl3, hp["conv_w8"], hp["conv_b"], 512)
    xbc_c = _conv_silu(pc3, hp["conv_w8"], hp["conv_b"], 512)
    h0 = jnp.zeros((batch, SSD_STATE, SSD_INNER), F32)
    (bias0, a0), (bias1, a1) = hp["dirs"]
    yc0, hc0 = _ssd_scan(xbc_c, pc3, h0, tri_f, hp["expand"], bias0, a0, False)
    yl0, _ = _ssd_scan(xbc_l, pl3, hc0, tri_f, hp["expand"], bias0, a0, False)
    fin = lambda y0: (y0, hp["dskip"], hp["normg"])
    sc, hc1 = _ssd_scan(xbc_c, pc3, h0, tri_b, hp["expand"], bias1, a1, True, fin(yc0))
    sl, _ = _ssd_scan(xbc_l, pl3, hc1, tri_b, hp["expand"], bias1, a1, True, fin(yl0))
    ql = _qproj(pl_, hp["q_norm_g"], hp["wq"], cos_l, sin_l, n_lat, 1024)
    kl, vl = _kvproj(pl_, hp["kv_norm_g"], hp["wkv"], cos_l, sin_l, n_lat, 1024)
    kc, vc = _kvproj(pc_, hp["kv_norm_g"], hp["wkv"], cos_c, sin_c, n_ctx, 256)
    att_l = _attention(ql, kc, vc, kl, vl, batch, 2048, 256)
    xl_new = _outproj(sl.reshape(-1, SSD_INNER), att_l, hp["w_out1"], hp["w_out2"], xl, mod_l, 2, n_lat, 1024)
    xc_new = xc
    if update_ctx:
        qc = _qproj(pc_, hp["q_norm_g"], hp["wq"], cos_c, sin_c, n_ctx, 256)
        att_c = _attention(qc, kc, vc, None, None, batch, 256, 256)
        xc_new = _outproj(sc.reshape(-1, SSD_INNER), att_c, hp["w_out1"], hp["w_out2"], xc, mod_c, 2, xc.shape[0], 256)
    return xl_new, xc_new


def _peer_layer(x, g2, mod, rows_per_batch, pp):
    tt, an, br = _peer_stats(x, g2, mod, rows_per_batch, pp["wq"], pp["k1"], pp["k2"], PEER_STATS_TM)
    return _peer_experts(tt, pp["u"], pp["vt"], an, br, x, mod, 5, rows_per_batch, PEER_TM, PEER_TE)


def kernel(x, c, ctx, c_ctx, ada_w, ada_b, norm1_g, norm2_g, hyb_w_in, ssd_conv_w, ssd_conv_b, ssd_a_log,
           ssd_dt_bias, ssd_d, ssd_norm_g, mla_q_norm_g, mla_w_qb, mla_kv_norm_g, mla_w_kvb, hyb_w_out, gm_w_in,
           gm_ln_g, gm_ln_b, gm_ws, gm_bs, gm_w_out, peer_wq, peer_k1, peer_k2, peer_u, peer_v, final_norm_g):
    batch, n_lat, d = x.shape
    n_ctx = ctx.shape[1]
    depth = ada_w.shape[0]
    xl = x.reshape(batch * n_lat, d)
    xc = ctx.reshape(batch * n_ctx, d)
    n_cond = -(-(batch + 1) // SUBLANE) * SUBLANE
    conds = jnp.zeros((n_cond, d), F32).at[:batch].set(c).at[batch].set(c_ctx)
    mods = _mods(conds, ada_w, ada_b)
    ropes = _rope_tables(n_lat, n_ctx)
    for layer in range(depth):
        i = layer // 2
        even = layer % 2 == 0
        keep_ctx = any(j % 2 == 0 for j in range(layer + 1, depth))
        mod_l = mods[layer, :batch].reshape(batch * N_MOD, 1, d)
        mod_c = mods[layer, batch].reshape(N_MOD, 1, d)
        g1, g2 = norm1_g[layer][None, :], norm2_g[layer][None, :]
        if even:
            hp = _prep_hybrid(hyb_w_in[i], ssd_conv_w[i], ssd_conv_b[i], ssd_a_log[i], ssd_dt_bias[i], ssd_d[i],
                              ssd_norm_g[i], mla_q_norm_g[i], mla_w_qb[i], mla_kv_norm_g[i], mla_w_kvb[i], hyb_w_out[i])
            xl, xc = _hybrid_layer(xl, xc, batch, g1, mod_l, mod_c, hp, ropes, keep_ctx)
        else:
            gp = (gm_w_in[i].astype(BF16), gm_ln_g[i][None, :], gm_ln_b[i][None, :], gm_ws[i].astype(BF16),
                  jnp.repeat(gm_bs[i].T, GM_INNER // GM_GROUPS, axis=1), gm_w_out[i].astype(BF16))
            xl = _gmlp(xl, g1, mod_l, n_lat, *gp, 512)
            if keep_ctx:
                xc = _gmlp(xc, g1, mod_c, xc.shape[0], *gp, 256)
        pp = dict(wq=peer_wq[layer].astype(BF16), k1=peer_k1[layer].astype(BF16), k2=peer_k2[layer].astype(BF16),
                  u=peer_u[layer].astype(BF16),
                  vt=jnp.transpose(peer_v[layer].astype(BF16).reshape(-1, PEER_TE, d), (0, 2, 1)))
        xl = _peer_layer(xl, g2, mod_l, n_lat, pp)
        if keep_ctx:
            xc = _peer_layer(xc, g2, mod_c, xc.shape[0], pp)
    return _final_norm(xl, final_norm_g[None, :], 512).reshape(batch, n_lat, d)
```
